```python
import jax, jax.numpy as jnp
from jax import lax
import numpy as np

D_MODEL = 2048
BATCH = 2
SEQ = 4096
DEPTH = 2
DEC_BATCH = 8
DEC_SEQ = 4
PAST_LEN = 16384
PAGE_SIZE = 128

POOL_WIDTH = D_MODEL // 2
POOL_WINDOWS = (2, 4, 8, 16)
POOL_GROUPS = len(POOL_WINDOWS)
POOL_GROUP_DIM = POOL_WIDTH // POOL_GROUPS
POOL_STATE = max(POOL_WINDOWS) - 1
N_HEADS = 16
HEAD_DIM = 64
N_KV_HEADS = 4
GQA = N_HEADS // N_KV_HEADS
ATT_WIDTH = N_HEADS * HEAD_DIM
KV_WIDTH = N_KV_HEADS * HEAD_DIM
N_KV_ROWS = 4
N_KV_PROJ = 6
BLOCK = 64
TOP_N = 16
WINDOW = 512
Q_BLOCK = 64
N_ATT_GATES = 3
SSM_WIDTH = D_MODEL // 2
SSM_GROUP_DIM = 16
SSM_GROUPS = SSM_WIDTH // SSM_GROUP_DIM
SSM_STATE = 64
DT_MIN = 0.001
DT_MAX = 0.1
N_BRANCH = 3
EPS = 1e-6
NEG = -1e30
FORCE = 1e4
IN_SPLITS = (POOL_WIDTH, POOL_WIDTH, ATT_WIDTH, N_KV_PROJ * KV_WIDTH, N_HEADS * N_ATT_GATES,
             ATT_WIDTH, SSM_WIDTH, SSM_WIDTH, N_BRANCH * D_MODEL)
D_IN = sum(IN_SPLITS)
IN_OFFSETS = tuple(int(v) for v in np.cumsum(IN_SPLITS)[:-1])

kernel_name = 'hybrid_pool_nsa_s5_decode_step'


def _rmsnorm(x, g):
    x32 = x.astype(jnp.float32)
    y = x32 * lax.rsqrt(jnp.mean(x32 * x32, axis=-1, keepdims=True) + EPS)
    return (y * g.astype(jnp.float32)).astype(x.dtype)


def _pool_mixer(u, prefix, q0, w_pool, pool_scale):
    b, s, _ = u.shape
    ext = jnp.concatenate([prefix.astype(u.dtype), u], axis=1)
    e32 = ext.astype(jnp.float32)
    cs = jnp.cumsum(e32, axis=1)
    cs = jnp.concatenate([jnp.zeros_like(cs[:, :1]), cs], axis=1)
    pos = q0 + jnp.arange(s, dtype=jnp.int32)
    end = POOL_STATE + 1
    means = []
    for gi, w in enumerate(POOL_WINDOWS):
        lo, hi = gi * POOL_GROUP_DIM, (gi + 1) * POOL_GROUP_DIM
        tot = cs[:, end:end + s, lo:hi] - cs[:, end - w:end - w + s, lo:hi]
        cnt = jnp.minimum(pos + 1, w).astype(jnp.float32)
        means.append(tot / cnt[None, :, None])
    diff = jnp.concatenate(means, axis=-1) - e32[:, POOL_STATE:]
    diff = diff.reshape(b, s, POOL_GROUPS, POOL_GROUP_DIM)
    y = jnp.einsum('bsgc,gcd->bsgd', diff, w_pool.astype(jnp.float32)).reshape(b, s, POOL_WIDTH)
    y = y * pool_scale.astype(jnp.float32)
    return y.astype(u.dtype), ext[:, -POOL_STATE:]


def _block_gather(kb, ix):
    return kb[ix]


_gather_bk = jax.vmap(jax.vmap(_block_gather))


def _nsa_mixer(q, kv_new, gate_logits, kv_past, win_prefix, win_pos0, win_keep, pe_cmp, w_phi):
    b, s, _ = q.shape
    f32 = jnp.float32
    t0 = kv_past.shape[1]
    scale = HEAD_DIM ** -0.5
    q = q.reshape(b, s, N_KV_HEADS, GQA, HEAD_DIM)
    kvn = kv_new.reshape(b, s, N_KV_PROJ, N_KV_HEADS, HEAD_DIM)
    full = jnp.concatenate([kv_past.astype(kvn.dtype), kvn[:, :, :N_KV_ROWS]], axis=1)
    t = t0 + s
    nb = -(-t // BLOCK)
    full = jnp.pad(full, ((0, 0), (0, nb * BLOCK - t), (0, 0), (0, 0), (0, 0)))
    blk = full.reshape(b, nb, BLOCK, N_KV_ROWS, N_KV_HEADS, HEAD_DIM)
    kc = jnp.einsum('bnlkd,lde->bnke', blk[:, :, :, 0] + pe_cmp[0][:, None, :], w_phi[0])
    vc = jnp.einsum('bnlkd,lde->bnke', blk[:, :, :, 1] + pe_cmp[1][:, None, :], w_phi[1])
    pos = t0 + jnp.arange(s, dtype=jnp.int32)
    nidx = jnp.arange(nb, dtype=jnp.int32)
    sc = jnp.einsum('bskgd,bnkd->bkgsn', q, kc).astype(f32) * scale
    cvalid = nidx[None, :] < ((pos + 1) // BLOCK)[:, None]
    pc = jax.nn.softmax(jnp.where(cvalid, sc, NEG), axis=-1) * cvalid
    o_cmp = jnp.einsum('bkgsn,bnkd->bskgd', pc.astype(vc.dtype), vc)
    imp = pc.sum(axis=2)
    cur = (pos // BLOCK)[:, None]
    forced = (nidx[None] == 0) | (nidx[None] == cur) | (nidx[None] == cur - 1)
    score = jnp.where(forced, FORCE, jnp.where(nidx[None] < cur, imp, NEG))
    n_sel = min(TOP_N, nb)
    top_val, top_idx = lax.top_k(score, n_sel)
    sel_ok = top_val > NEG / 2
    ks_b = jnp.moveaxis(blk[:, :, :, 2], 3, 1)
    vs_b = jnp.moveaxis(blk[:, :, :, 3], 3, 1)
    kw_full = jnp.concatenate([win_prefix[:, :, 0].astype(kvn.dtype), kvn[:, :, 4]], axis=1)
    vw_full = jnp.concatenate([win_prefix[:, :, 1].astype(kvn.dtype), kvn[:, :, 5]], axis=1)
    lp = win_prefix.shape[1]
    qb = Q_BLOCK if s % Q_BLOCK == 0 else s
    nq = s // qb
    q_blocks = jnp.moveaxis(q.reshape(b, nq, qb, N_KV_HEADS, GQA, HEAD_DIM), 1, 0)
    idx_blocks = jnp.moveaxis(top_idx.reshape(b, N_KV_HEADS, nq, qb, n_sel), 2, 0)
    ok_blocks = jnp.moveaxis(sel_ok.reshape(b, N_KV_HEADS, nq, qb, n_sel), 2, 0)
    pos_blocks = pos.reshape(nq, qb)
    starts = jnp.arange(nq, dtype=jnp.int32) * qb
    coff = jnp.arange(BLOCK, dtype=jnp.int32)
    woff = jnp.arange(lp + qb, dtype=jnp.int32)

    def sparse_block(args):
        qx, ix, ok, pq, st = args
        kg = _gather_bk(ks_b, ix)
        vg = _gather_bk(vs_b, ix)
        ss = jnp.einsum('bqkgd,bkqjcd->bkgqjc', qx, kg).astype(f32) * scale
        kpos = ix[..., None] * BLOCK + coff
        m = ok[..., None] & (kpos <= pq[None, None, :, None, None])
        ss = jnp.where(m[:, :, None], ss, NEG)
        ps = jax.nn.softmax(ss.reshape(ss.shape[:4] + (-1,)), axis=-1).reshape(ss.shape)
        o_sel = jnp.einsum('bkgqjc,bkqjcd->bqkgd', ps.astype(vg.dtype), vg)
        kw = lax.dynamic_slice_in_dim(kw_full, st, lp + qb, axis=1)
        vw = lax.dynamic_slice_in_dim(vw_full, st, lp + qb, axis=1)
        kp = win_pos0 + st + woff
        sw = jnp.einsum('bqkgd,btkd->bkgqt', qx, kw).astype(f32) * scale
        mw = (kp[None] >= 0) & (kp[None] <= pq[:, None]) & (kp[None] > pq[:, None] - WINDOW)
        pw = jax.nn.softmax(jnp.where(mw, sw, NEG), axis=-1)
        o_win = jnp.einsum('bkgqt,btkd->bqkgd', pw.astype(vw.dtype), vw)
        return o_sel, o_win

    o_sel, o_win = lax.map(sparse_block, (q_blocks, idx_blocks, ok_blocks, pos_blocks, starts))
    o_sel = jnp.moveaxis(o_sel, 0, 1).reshape(b, s, N_KV_HEADS, GQA, HEAD_DIM)
    o_win = jnp.moveaxis(o_win, 0, 1).reshape(b, s, N_KV_HEADS, GQA, HEAD_DIM)
    g = jax.nn.sigmoid(gate_logits.astype(f32)).reshape(b, s, N_KV_HEADS, GQA, N_ATT_GATES)
    o = (g[..., 0:1] * o_cmp.astype(f32) + g[..., 1:2] * o_sel.astype(f32)
         + g[..., 2:3] * o_win.astype(f32))
    win_state = jnp.stack([kw_full[:, -win_keep:], vw_full[:, -win_keep:]], axis=2)
    return o.reshape(b, s, ATT_WIDTH).astype(kvn.dtype), kvn[:, :, :N_KV_ROWS], win_state


def _complex_affine_combine(e1, e2):
    a1r, a1i, b1r, b1i = e1
    a2r, a2i, b2r, b2i = e2
    return (a2r * a1r - a2i * a1i, a2r * a1i + a2i * a1r,
            a2r * b1r - a2i * b1i + b2r, a2r * b1i + a2i * b1r + b2i)


def _ssm_mixer(u, h0, lam_re, lam_im, log_step, b_re, b_im, c_re, c_im, d_skip, w_glu):
    b, s, _ = u.shape
    f32 = jnp.float32
    u32 = u.astype(f32).reshape(b, s, SSM_GROUPS, SSM_GROUP_DIM)
    lr, li = lam_re.astype(f32), lam_im.astype(f32)
    dt = jnp.exp(log_step.astype(f32))[:, None]
    mag = jnp.exp(lr * dt)
    ab_re, ab_im = mag * jnp.cos(li * dt), mag * jnp.sin(li * dt)
    den = lr * lr + li * li
    co_re = ((ab_re - 1.0) * lr + ab_im * li) / den
    co_im = (ab_im * lr - (ab_re - 1.0) * li) / den
    br, bi = b_re.astype(f32), b_im.astype(f32)
    bb_re = co_re[..., None] * br - co_im[..., None] * bi
    bb_im = co_re[..., None] * bi + co_im[..., None] * br
    bu_re = jnp.einsum('bsgc,gnc->bsgn', u32, bb_re)
    bu_im = jnp.einsum('bsgc,gnc->bsgn', u32, bb_im)
    h0r, h0i = h0[:, 0].astype(f32), h0[:, 1].astype(f32)
    bu_re = bu_re.at[:, 0].add(ab_re * h0r - ab_im * h0i)
    bu_im = bu_im.at[:, 0].add(ab_re * h0i + ab_im * h0r)
    a_re = jnp.broadcast_to(ab_re, bu_re.shape)
    a_im = jnp.broadcast_to(ab_im, bu_im.shape)
    _, _, hr, hi = lax.associative_scan(_complex_affine_combine, (a_re, a_im, bu_re, bu_im), axis=1)
    y = (jnp.einsum('gcn,bsgn->bsgc', c_re.astype(f32), hr)
         - jnp.einsum('gcn,bsgn->bsgc', c_im.astype(f32), hi))
    y = y.reshape(b, s, SSM_WIDTH) + d_skip.astype(f32) * u.astype(f32)
    z = jax.nn.gelu(y)
    out = z * jax.nn.sigmoid(z @ w_glu.astype(f32))
    new_state = jnp.stack([hr[:, -1], hi[:, -1]], axis=1)
    return out.astype(u.dtype), new_state.astype(h0.dtype)


def _layer(x, kv_past, win_prefix, win_pos0, win_keep, pool_prefix, ssm_h0, lw):
    (g_pre, g_post, w_in, w_pool, pool_scale, pe_cmp, w_phi, lam_re, lam_im, log_step,
     b_re, b_im, c_re, c_im, d_skip, w_glu, w_br_pool, w_br_nsa, w_br_ssm, w_out) = lw
    b, s, _ = x.shape
    q0 = kv_past.shape[1]
    h = _rmsnorm(x, g_pre)
    proj = h @ w_in
    pu, pz, q, kv, ag, az, su, sz, mg = jnp.split(proj, IN_OFFSETS, axis=-1)
    y_pool, pool_state = _pool_mixer(pu, pool_prefix, q0, w_pool, pool_scale)
    y_att, kv_rows, win_state = _nsa_mixer(q, kv, ag, kv_past, win_prefix, win_pos0, win_keep,
                                           pe_cmp, w_phi)
    y_ssm, ssm_state = _ssm_mixer(su, ssm_h0, lam_re, lam_im, log_step, b_re, b_im, c_re, c_im,
                                  d_skip, w_glu)
    br_pool = (y_pool * jax.nn.silu(pz)) @ w_br_pool
    br_att = (y_att * jax.nn.silu(az)) @ w_br_nsa
    br_ssm = (y_ssm * jax.nn.silu(sz)) @ w_br_ssm
    gm = jax.nn.sigmoid(mg.reshape(b, s, N_BRANCH, D_MODEL))
    merged = gm[:, :, 0] * br_pool + gm[:, :, 1] * br_att + gm[:, :, 2] * br_ssm
    out = merged @ w_out
    return x + _rmsnorm(out, g_post), kv_rows, win_state, pool_state, ssm_state


def setup_inputs(seed: int = 0) -> dict:
    key = jax.random.key(seed)
    ks = jax.random.split(key, 32)
    f32 = jnp.float32
    n_pages = PAST_LEN // PAGE_SIZE
    n_pool = (DEC_BATCH * n_pages * 5 + 3) // 4
    win_buf = min(WINDOW, PAST_LEN)

    def nrm(k, shape, sc):
        return jax.random.normal(k, shape, f32) * sc

    perm = jax.random.permutation(ks[3], n_pool)[:DEC_BATCH * n_pages]
    lam_im = jnp.broadcast_to(jnp.pi * jnp.arange(SSM_STATE, dtype=f32), (DEPTH, SSM_GROUPS, SSM_STATE))
    return {
        'x_prompt': nrm(ks[0], (BATCH, SEQ, D_MODEL), 1.0),
        'x_sample': nrm(ks[1], (DEC_BATCH, DEC_SEQ, D_MODEL), 1.0),
        'cache_kv': nrm(ks[2], (DEPTH, n_pool, PAGE_SIZE, N_KV_ROWS, N_KV_HEADS, HEAD_DIM), 1.0),
        'page_table': perm.reshape(DEC_BATCH, n_pages).astype(jnp.int32),
        'state_win_kv': nrm(ks[4], (DEPTH, DEC_BATCH, win_buf, 2, N_KV_HEADS, HEAD_DIM), 1.0),
        'state_pool': nrm(ks[5], (DEPTH, DEC_BATCH, POOL_STATE, POOL_WIDTH), 1.0),
        'state_ssm': nrm(ks[6], (DEPTH, DEC_BATCH, 2, SSM_GROUPS, SSM_STATE), 0.5),
        'g_pre': 1.0 + nrm(ks[7], (DEPTH, D_MODEL), 0.02),
        'g_post': 1.0 + nrm(ks[8], (DEPTH, D_MODEL), 0.02),
        'w_in': nrm(ks[9], (DEPTH, D_MODEL, D_IN), D_MODEL ** -0.5),
        'w_pool': nrm(ks[10], (DEPTH, POOL_GROUPS, POOL_GROUP_DIM, POOL_GROUP_DIM), POOL_GROUP_DIM ** -0.5),
        'pool_scale': 1.0 + nrm(ks[11], (DEPTH, POOL_WIDTH), 0.1),
        'pe_cmp': nrm(ks[12], (DEPTH, 2, BLOCK, HEAD_DIM), 0.02),
        'w_phi': nrm(ks[13], (DEPTH, 2, BLOCK, HEAD_DIM, HEAD_DIM), (BLOCK * HEAD_DIM) ** -0.5),
        'lam_re': -0.5 + nrm(ks[14], (DEPTH, SSM_GROUPS, SSM_STATE), 0.01),
        'lam_im': lam_im,
        'log_step': jax.random.uniform(ks[15], (DEPTH, SSM_GROUPS), f32,
                                       minval=float(np.log(DT_MIN)), maxval=float(np.log(DT_MAX))),
        'b_re': nrm(ks[16], (DEPTH, SSM_GROUPS, SSM_STATE, SSM_GROUP_DIM), (2 * SSM_GROUP_DIM) ** -0.5),
        'b_im': nrm(ks[17], (DEPTH, SSM_GROUPS, SSM_STATE, SSM_GROUP_DIM), (2 * SSM_GROUP_DIM) ** -0.5),
        'c_re': nrm(ks[18], (DEPTH, SSM_GROUPS, SSM_GROUP_DIM, SSM_STATE), (2 * SSM_STATE) ** -0.5),
        'c_im': nrm(ks[19], (DEPTH, SSM_GROUPS, SSM_GROUP_DIM, SSM_STATE), (2 * SSM_STATE) ** -0.5),
        'd_skip': nrm(ks[20], (DEPTH, SSM_WIDTH), 1.0),
        'w_glu': nrm(ks[21], (DEPTH, SSM_WIDTH, SSM_WIDTH), SSM_WIDTH ** -0.5),
        'w_br_pool': nrm(ks[22], (DEPTH, POOL_WIDTH, D_MODEL), POOL_WIDTH ** -0.5),
        'w_br_nsa': nrm(ks[23], (DEPTH, ATT_WIDTH, D_MODEL), ATT_WIDTH ** -0.5),
        'w_br_ssm': nrm(ks[24], (DEPTH, SSM_WIDTH, D_MODEL), SSM_WIDTH ** -0.5),
        'w_out': nrm(ks[25], (DEPTH, D_MODEL, D_MODEL), D_MODEL ** -0.5),
    }


def reference(x_prompt, x_sample, cache_kv, page_table, state_win_kv, state_pool, state_ssm,
              g_pre, g_post, w_in, w_pool, pool_scale, pe_cmp, w_phi, lam_re, lam_im, log_step,
              b_re, b_im, c_re, c_im, d_skip, w_glu, w_br_pool, w_br_nsa, w_br_ssm, w_out):
    bp, sp, _ = x_prompt.shape
    bd = x_sample.shape[0]
    past_len = page_table.shape[1] * cache_kv.shape[2]
    win_keep = state_win_kv.shape[2]
    dt = x_prompt.dtype
    yp, ys = x_prompt, x_sample
    kvp, kvs, wpr, wsa, ppr, psa, hpr, hsa = [], [], [], [], [], [], [], []
    for l in range(DEPTH):
        lw = (g_pre[l], g_post[l], w_in[l], w_pool[l], pool_scale[l], pe_cmp[l], w_phi[l],
              lam_re[l], lam_im[l], log_step[l], b_re[l], b_im[l], c_re[l], c_im[l],
              d_skip[l], w_glu[l], w_br_pool[l], w_br_nsa[l], w_br_ssm[l], w_out[l])
        yp, r_kv, r_win, r_pool, r_ssm = _layer(
            yp, jnp.zeros((bp, 0, N_KV_ROWS, N_KV_HEADS, HEAD_DIM), dt),
            jnp.zeros((bp, WINDOW, 2, N_KV_HEADS, HEAD_DIM), dt), -WINDOW, min(WINDOW, sp),
            jnp.zeros((bp, POOL_STATE, POOL_WIDTH), dt),
            jnp.zeros((bp, 2, SSM_GROUPS, SSM_STATE), dt), lw)
        past = cache_kv[l][page_table].reshape(bd, past_len, N_KV_ROWS, N_KV_HEADS, HEAD_DIM)
        ys, s_kv, s_win, s_pool, s_ssm = _layer(
            ys, past, state_win_kv[l], past_len - win_keep, win_keep,
            state_pool[l], state_ssm[l], lw)
        kvp.append(r_kv); kvs.append(s_kv); wpr.append(r_win); wsa.append(s_win)
        ppr.append(r_pool); psa.append(s_pool); hpr.append(r_ssm); hsa.append(s_ssm)
    return (yp, ys, jnp.stack(kvp), jnp.stack(kvs), jnp.stack(wpr), jnp.stack(wsa),
            jnp.stack(ppr), jnp.stack(psa), jnp.stack(hpr), jnp.stack(hsa))
```

```python
import functools

import jax
import jax.numpy as jnp
from jax import lax
from jax.experimental import pallas as pl
from jax.experimental.pallas import tpu as pltpu

F32 = jnp.float32
BF16 = jnp.bfloat16
I32 = jnp.int32

D_MODEL = 2048
PAGE_SIZE = 128
POOL_WIDTH = D_MODEL // 2
POOL_WINDOWS = (2, 4, 8, 16)
POOL_GROUP_DIM = POOL_WIDTH // len(POOL_WINDOWS)
POOL_STATE = max(POOL_WINDOWS) - 1
N_HEADS = 16
HEAD_DIM = 64
N_KV_HEADS = 4
GQA = N_HEADS // N_KV_HEADS
ATT_WIDTH = N_HEADS * HEAD_DIM
KV_WIDTH = N_KV_HEADS * HEAD_DIM
BLOCK = 64
TOP_N = 16
WINDOW = 512
N_ATT_GATES = 3
SSM_WIDTH = D_MODEL // 2
SSM_GROUP_DIM = 16
SSM_GROUPS = SSM_WIDTH // SSM_GROUP_DIM
SSM_STATE = 64
SSM_CH = SSM_GROUPS * SSM_STATE
N_BRANCH = 3
EPS = 1e-6
NEG = -1e30
FORCE = 1e4
ATT_SCALE = HEAD_DIM ** -0.5

C_PU, C_PZ, C_Q, C_AZ, C_SU, C_SZ = (i * 1024 for i in range(6))
C_MG = 6 * 1024
C_KV = C_MG + N_BRANCH * D_MODEL
N_MAIN = C_KV + 6 * KV_WIDTH
N_GATE_PAD = 128

VMEM_LIMIT_BYTES = 52 * 1024 * 1024
SUBLANES = 8
KEY_TILE = 256
SSM_LANE_CHUNK = 1024

NT_DIMS = (((1,), (1,)), ((), ()))
TN_DIMS = (((0,), (0,)), ((), ()))


def _params(*sem):
    return pltpu.CompilerParams(dimension_semantics=sem, vmem_limit_bytes=VMEM_LIMIT_BYTES)


def _silu(x):
    return x * jax.nn.sigmoid(x)


def _inproj_kernel(x_ref, g_ref, w_ref, o_ref, h_ref):
    @pl.when(pl.program_id(1) == 0)
    def _():
        x = x_ref[...]
        r = lax.rsqrt(jnp.mean(x * x, axis=-1, keepdims=True) + EPS)
        h_ref[...] = (x * r * g_ref[...]).astype(BF16)

    o_ref[...] = jnp.dot(h_ref[...], w_ref[...], preferred_element_type=F32)


def _inproj(x2d, g, w, tn):
    m, d = x2d.shape
    n = w.shape[1]
    tm = min(m, 1024)
    return pl.pallas_call(
        _inproj_kernel,
        grid=(m // tm, n // tn),
        in_specs=[pl.BlockSpec((tm, d), lambda i, j: (i, 0)),
                  pl.BlockSpec((1, d), lambda i, j: (0, 0)),
                  pl.BlockSpec((d, tn), lambda i, j: (0, j))],
        out_specs=pl.BlockSpec((tm, tn), lambda i, j: (i, j)),
        out_shape=jax.ShapeDtypeStruct((m, n), F32),
        scratch_shapes=[pltpu.VMEM((tm, d), BF16)],
        compiler_params=_params("parallel", "arbitrary"),
        name="inproj",
    )(x2d, g, w)


def _pool_kernel(pu_ref, pz_ref, pre_ref, wp_ref, sc_ref, o_ref, st_ref, e_ref, *, tt, q0, rows_last):
    t = pl.program_id(1)
    hist = POOL_STATE + 1

    @pl.when(t == 0)
    def _():
        e_ref[0:hist, :] = pre_ref[0]

    @pl.when(t > 0)
    def _():
        e_ref[0:hist, :] = e_ref[tt:tt + hist, :]

    u = pu_ref[0]
    e_ref[hist:hist + tt, :] = u
    pos = q0 + t * tt + lax.broadcasted_iota(I32, (tt, 1), 0)
    ys = []
    for gi, w in enumerate(POOL_WINDOWS):
        lo = gi * POOL_GROUP_DIM
        tot = e_ref[hist:hist + tt, lo:lo + POOL_GROUP_DIM]
        for k in range(1, w):
            tot = tot + e_ref[hist - k:hist - k + tt, lo:lo + POOL_GROUP_DIM]
        cnt = jnp.minimum(pos + 1, w).astype(F32)
        diff = tot / cnt - u[:, lo:lo + POOL_GROUP_DIM]
        ys.append(jnp.dot(diff.astype(BF16), wp_ref[gi], preferred_element_type=F32))
    y = jnp.concatenate(ys, axis=-1) * sc_ref[...]
    o_ref[0] = (y * _silu(pz_ref[0])).astype(BF16)

    @pl.when(t == pl.num_programs(1) - 1)
    def _():
        st_ref[0] = e_ref[rows_last:rows_last + hist, :]


def _pool(proj3, prefix16, w_pool, pool_scale, *, q0, s_valid):
    b, s, _ = proj3.shape
    tt = min(s, 512)
    hist = POOL_STATE + 1
    rows_last = ((s_valid - 1) % tt) + 1
    kern = functools.partial(_pool_kernel, tt=tt, q0=q0, rows_last=rows_last)
    return pl.pallas_call(
        kern,
        grid=(b, s // tt),
        in_specs=[pl.BlockSpec((1, tt, POOL_WIDTH), lambda i, t: (i, t, C_PU // POOL_WIDTH)),
                  pl.BlockSpec((1, tt, POOL_WIDTH), lambda i, t: (i, t, C_PZ // POOL_WIDTH)),
                  pl.BlockSpec((1, hist, POOL_WIDTH), lambda i, t: (i, 0, 0)),
                  pl.BlockSpec((len(POOL_WINDOWS), POOL_GROUP_DIM, POOL_GROUP_DIM), lambda i, t: (0, 0, 0)),
                  pl.BlockSpec((1, POOL_WIDTH), lambda i, t: (0, 0))],
        out_specs=[pl.BlockSpec((1, tt, POOL_WIDTH), lambda i, t: (i, t, 0)),
                   pl.BlockSpec((1, hist, POOL_WIDTH), lambda i, t: (i, 0, 0))],
        out_shape=[jax.ShapeDtypeStruct((b, s, POOL_WIDTH), BF16),
                   jax.ShapeDtypeStruct((b, hist, POOL_WIDTH), F32)],
        scratch_shapes=[pltpu.VMEM((hist + tt, POOL_WIDTH), F32)],
        compiler_params=_params("parallel", "arbitrary"),
        name="pool",
    )(proj3, proj3, prefix16, w_pool, pool_scale)


def _compress_kernel(x_ref, pe_ref, w_ref, o_ref):
    @pl.when(pl.program_id(1) == 0)
    def _():
        o_ref[...] = jnp.zeros_like(o_ref)

    o_ref[...] += jnp.dot((x_ref[...] + pe_ref[...]).astype(BF16), w_ref[...], preferred_element_type=F32)


def _compress(x2, pe, w):
    nb, kdim = x2.shape
    tm = min(nb, 256)
    tk = 2048
    return pl.pallas_call(
        _compress_kernel,
        grid=(nb // tm, kdim // tk),
        in_specs=[pl.BlockSpec((tm, tk), lambda i, k: (i, k)),
                  pl.BlockSpec((1, tk), lambda i, k: (0, k)),
                  pl.BlockSpec((tk, KV_WIDTH), lambda i, k: (k, 0))],
        out_specs=pl.BlockSpec((tm, KV_WIDTH), lambda i, k: (i, 0)),
        out_shape=jax.ShapeDtypeStruct((nb, KV_WIDTH), F32),
        compiler_params=_params("parallel", "arbitrary"),
        name="compress",
    )(x2, pe, w)


def _cmpsel_kernel(q_ref, kc_ref, vc_ref, ocmp_ref, sel_ref, s_ref, *, tq, nbp, pos0, n_sel):
    j = pl.program_id(1)
    qs = (q_ref[0] * ATT_SCALE).astype(BF16)
    kc = kc_ref[0].astype(BF16)
    vc = vc_ref[0].astype(BF16)
    nidx = lax.broadcasted_iota(I32, (nbp, tq), 0)
    pos = pos0 + j * tq + lax.broadcasted_iota(I32, (nbp, tq), 1)
    cvalid = nidx < lax.shift_right_logical(pos + 1, 6)
    cur = lax.shift_right_logical(pos, 6)
    forced = (nidx == 0) | (nidx == cur) | (nidx == cur - 1)
    outs = []
    for k in range(N_KV_HEADS):
        ksl = slice(k * HEAD_DIM, (k + 1) * HEAD_DIM)
        imp = jnp.zeros((nbp, tq), F32)
        for g in range(GQA):
            h = k * GQA + g
            sc = lax.dot_general(kc[:, ksl], qs[:, h * HEAD_DIM:(h + 1) * HEAD_DIM], NT_DIMS,
                                 preferred_element_type=F32)
            sc = jnp.where(cvalid, sc, NEG)
            e = jnp.exp(sc - jnp.max(sc, axis=0, keepdims=True))
            pc = jnp.where(cvalid, e / jnp.sum(e, axis=0, keepdims=True), 0.0)
            imp = imp + pc
            outs.append(lax.dot_general(pc.astype(BF16), vc[:, ksl], TN_DIMS, preferred_element_type=F32))
        s_ref[:, k * tq:(k + 1) * tq] = jnp.where(forced, FORCE, jnp.where(nidx < cur, imp, NEG))
    ocmp_ref[0] = jnp.concatenate(outs, axis=-1)

    score = s_ref[...]
    rows = lax.broadcasted_iota(I32, score.shape, 0)

    def pick_one(_, carry):
        work, sel = carry
        top = jnp.max(work, axis=0, keepdims=True)
        first = jnp.min(jnp.where(work == top, rows, nbp), axis=0, keepdims=True)
        hit = rows == first
        return jnp.where(hit, -jnp.inf, work), jnp.where(hit, 1.0, sel)

    _, sel = lax.fori_loop(0, n_sel, pick_one, (score, jnp.zeros_like(score)))
    sel_ref[0] = jnp.where(score > NEG / 2, sel, 0.0).astype(BF16)


def _cmpsel(proj3, kc, vc, *, pos0, tq, nb_total):
    b, s, _ = proj3.shape
    nbp = kc.shape[1]
    kern = functools.partial(_cmpsel_kernel, tq=tq, nbp=nbp, pos0=pos0, n_sel=min(TOP_N, nb_total))
    lanes = N_KV_HEADS * tq
    return pl.pallas_call(
        kern,
        grid=(b, s // tq),
        in_specs=[pl.BlockSpec((1, tq, ATT_WIDTH), lambda i, j: (i, j, C_Q // ATT_WIDTH)),
                  pl.BlockSpec((1, nbp, KV_WIDTH), lambda i, j: (i, 0, 0)),
                  pl.BlockSpec((1, nbp, KV_WIDTH), lambda i, j: (i, 0, 0))],
        out_specs=[pl.BlockSpec((1, tq, ATT_WIDTH), lambda i, j: (i, j, 0)),
                   pl.BlockSpec((1, nbp, lanes), lambda i, j: (i, 0, j))],
        out_shape=[jax.ShapeDtypeStruct((b, s, ATT_WIDTH), F32),
                   jax.ShapeDtypeStruct((b, nbp, N_KV_HEADS * s), BF16)],
        scratch_shapes=[pltpu.VMEM((nbp, lanes), F32)],
        compiler_params=_params("parallel", "parallel"),
        name="cmpsel",
    )(proj3, kc, vc)


def _attn_init(q_ref, qs_ref, m_ref, l_ref, acc_ref, tq):
    qs = (q_ref[0] * ATT_SCALE).astype(BF16)
    for k in range(N_KV_HEADS):
        for g in range(GQA):
            h = k * GQA + g
            qs_ref[k, g * tq:(g + 1) * tq, :] = qs[:, h * HEAD_DIM:(h + 1) * HEAD_DIM]
    m_ref[...] = jnp.full_like(m_ref, NEG)
    l_ref[...] = jnp.zeros_like(l_ref)
    acc_ref[...] = jnp.zeros_like(acc_ref)


def _attn_step(k_tile, v_tile, ok_of_head, qs_ref, m_ref, l_ref, acc_ref, tq):
    tk = k_tile.shape[0]
    for k in range(N_KV_HEADS):
        ksl = slice(k * HEAD_DIM, (k + 1) * HEAD_DIM)
        s = lax.dot_general(qs_ref[k], k_tile[:, ksl], NT_DIMS, preferred_element_type=F32)
        ok = ok_of_head(k)[None]
        s = jnp.where(ok, s.reshape(GQA, tq, tk), NEG)
        m_old = m_ref[k].reshape(GQA, tq, 1)
        m_new = jnp.maximum(m_old, jnp.max(s, axis=-1, keepdims=True))
        p = jnp.where(ok, jnp.exp(s - m_new), 0.0)
        alpha = jnp.exp(m_old - m_new)
        l_ref[k] = (alpha * l_ref[k].reshape(GQA, tq, 1) + jnp.sum(p, axis=-1, keepdims=True)).reshape(GQA * tq, 1)
        m_ref[k] = m_new.reshape(GQA * tq, 1)
        pv = jnp.dot(p.reshape(GQA * tq, tk).astype(BF16), v_tile[:, ksl], preferred_element_type=F32)
        acc_ref[k] = alpha.reshape(GQA * tq, 1) * acc_ref[k] + pv


def _attn_finish(o_ref, l_ref, acc_ref, tq):
    outs = []
    for k in range(N_KV_HEADS):
        o = acc_ref[k] / l_ref[k]
        for g in range(GQA):
            outs.append(o[g * tq:(g + 1) * tq, :])
    o_ref[0] = jnp.concatenate(outs, axis=-1)


def _selattn_kernel(q_ref, k_ref, v_ref, sel_ref, e_ref, o_ref, qs_ref, m_ref, l_ref, acc_ref, *, tq, pos0):
    j = pl.program_id(1)
    kt = pl.program_id(2)

    @pl.when(kt == 0)
    def _():
        _attn_init(q_ref, qs_ref, m_ref, l_ref, acc_ref, tq)

    last_kt = (pos0 + (j + 1) * tq - 1) // KEY_TILE

    @pl.when(kt <= last_kt)
    def _():
        causal = (kt * KEY_TILE + lax.broadcasted_iota(I32, (tq, KEY_TILE), 1)
                  <= pos0 + j * tq + lax.broadcasted_iota(I32, (tq, KEY_TILE), 0))
        e_tile = e_ref[...]

        def ok_of_head(k):
            chosen = lax.dot_general(sel_ref[0, :, k * tq:(k + 1) * tq], e_tile, TN_DIMS,
                                     preferred_element_type=F32)
            return (chosen > 0.5) & causal

        _attn_step(k_ref[0].astype(BF16), v_ref[0].astype(BF16), ok_of_head, qs_ref, m_ref, l_ref, acc_ref, tq)

    @pl.when(kt == pl.num_programs(2) - 1)
    def _():
        _attn_finish(o_ref, l_ref, acc_ref, tq)


def _attn_scratch(tq):
    r = GQA * tq
    return [pltpu.VMEM((N_KV_HEADS, r, HEAD_DIM), BF16),
            pltpu.VMEM((N_KV_HEADS, r, 1), F32),
            pltpu.VMEM((N_KV_HEADS, r, 1), F32),
            pltpu.VMEM((N_KV_HEADS, r, HEAD_DIM), F32)]


def _selattn(proj3, karr, varr, kcol, vcol, sel, expand, *, pos0, tq):
    b, s, _ = proj3.shape
    nkt = expand.shape[1] // KEY_TILE
    nbp = expand.shape[0]
    lanes = N_KV_HEADS * tq

    def kv_index(col):
        def index(i, j, kt):
            return (i, jnp.minimum(kt, (pos0 + (j + 1) * tq - 1) // KEY_TILE), col)
        return index

    kern = functools.partial(_selattn_kernel, tq=tq, pos0=pos0)
    return pl.pallas_call(
        kern,
        grid=(b, s // tq, nkt),
        in_specs=[pl.BlockSpec((1, tq, ATT_WIDTH), lambda i, j, kt: (i, j, C_Q // ATT_WIDTH)),
                  pl.BlockSpec((1, KEY_TILE, KV_WIDTH), kv_index(kcol)),
                  pl.BlockSpec((1, KEY_TILE, KV_WIDTH), kv_index(vcol)),
                  pl.BlockSpec((1, nbp, lanes), lambda i, j, kt: (i, 0, j)),
                  pl.BlockSpec((nbp, KEY_TILE), lambda i, j, kt: (0, kt))],
        out_specs=pl.BlockSpec((1, tq, ATT_WIDTH), lambda i, j, kt: (i, j, 0)),
        out_shape=jax.ShapeDtypeStruct((b, s, ATT_WIDTH), F32),
        scratch_shapes=_attn_scratch(tq),
        compiler_params=_params("parallel", "parallel", "arbitrary"),
        name="selattn",
    )(proj3, karr, varr, sel, expand)


def _winattn_kernel(q_ref, k_ref, v_ref, o_ref, qs_ref, m_ref, l_ref, acc_ref, *, tq, pos0, row_pos0):
    j = pl.program_id(1)
    i = pl.program_id(2)

    @pl.when(i == 0)
    def _():
        _attn_init(q_ref, qs_ref, m_ref, l_ref, acc_ref, tq)

    kp = row_pos0 + ((j * tq) // KEY_TILE + i) * KEY_TILE + lax.broadcasted_iota(I32, (tq, KEY_TILE), 1)
    pq = pos0 + j * tq + lax.broadcasted_iota(I32, (tq, KEY_TILE), 0)
    ok = (kp >= 0) & (kp <= pq) & (kp > pq - WINDOW)
    _attn_step(k_ref[0].astype(BF16), v_ref[0].astype(BF16), lambda k: ok, qs_ref, m_ref, l_ref, acc_ref, tq)

    @pl.when(i == pl.num_programs(2) - 1)
    def _():
        _attn_finish(o_ref, l_ref, acc_ref, tq)


def _winattn(proj3, kwin, vwin, *, pos0, row_pos0, tq):
    b, s, _ = proj3.shape
    assert tq % KEY_TILE == 0 or s == tq
    n_sub = -(-(WINDOW + tq) // KEY_TILE)
    assert kwin.shape[1] >= ((s - tq) // KEY_TILE + n_sub) * KEY_TILE

    def kv_index(i, j, t):
        return (i, (j * tq) // KEY_TILE + t, 0)

    kern = functools.partial(_winattn_kernel, tq=tq, pos0=pos0, row_pos0=row_pos0)
    return pl.pallas_call(
        kern,
        grid=(b, s // tq, n_sub),
        in_specs=[pl.BlockSpec((1, tq, ATT_WIDTH), lambda i, j, t: (i, j, C_Q // ATT_WIDTH)),
                  pl.BlockSpec((1, KEY_TILE, KV_WIDTH), kv_index),
                  pl.BlockSpec((1, KEY_TILE, KV_WIDTH), kv_index)],
        out_specs=pl.BlockSpec((1, tq, ATT_WIDTH), lambda i, j, t: (i, j, 0)),
        out_shape=jax.ShapeDtypeStruct((b, s, ATT_WIDTH), F32),
        scratch_shapes=_attn_scratch(tq),
        compiler_params=_params("parallel", "parallel", "arbitrary"),
        name="winattn",
    )(proj3, kwin, vwin)


def _attcomb_kernel(oc_ref, os_ref, ow_ref, ag_ref, az_ref, o_ref):
    gate = jax.nn.sigmoid(ag_ref[0])
    oc, osel, ow = oc_ref[0], os_ref[0], ow_ref[0]
    outs = []
    for h in range(N_HEADS):
        sl = slice(h * HEAD_DIM, (h + 1) * HEAD_DIM)
        c = h * N_ATT_GATES
        outs.append(gate[:, c:c + 1] * oc[:, sl] + gate[:, c + 1:c + 2] * osel[:, sl]
                    + gate[:, c + 2:c + 3] * ow[:, sl])
    o_ref[0] = (jnp.concatenate(outs, axis=-1) * _silu(az_ref[0])).astype(BF16)


def _attcomb(ocmp, osel, owin, gates3, proj3):
    b, s, _ = ocmp.shape
    tt = min(s, 256)
    att = pl.BlockSpec((1, tt, ATT_WIDTH), lambda i, t: (i, t, 0))
    return pl.pallas_call(
        _attcomb_kernel,
        grid=(b, s // tt),
        in_specs=[att, att, att,
                  pl.BlockSpec((1, tt, N_GATE_PAD), lambda i, t: (i, t, 0)),
                  pl.BlockSpec((1, tt, ATT_WIDTH), lambda i, t: (i, t, C_AZ // ATT_WIDTH))],
        out_specs=att,
        out_shape=jax.ShapeDtypeStruct((b, s, ATT_WIDTH), BF16),
        compiler_params=_params("parallel", "parallel"),
        name="attcomb",
    )(ocmp, osel, owin, gates3, proj3)


def _gather_kernel(pt_ref, pa_ref, pb_ref, new_ref, kc_ref, vc_ref, ks_ref, vs_ref, *, n_steps, s_new):
    del pt_ref
    step = pl.program_id(1)

    @pl.when(step < n_steps)
    def _():
        for half, page in enumerate((pa_ref, pb_ref)):
            rows = slice(half * PAGE_SIZE, (half + 1) * PAGE_SIZE)
            for comp, dst in enumerate((kc_ref, vc_ref, ks_ref, vs_ref)):
                dst[0, rows, :] = page[0, 0, :, comp * KV_WIDTH:(comp + 1) * KV_WIDTH]

    @pl.when(step == n_steps)
    def _():
        for comp, dst in ((2, ks_ref), (3, vs_ref)):
            dst[0] = jnp.zeros(dst.shape[1:], F32)
            dst[0, 0:s_new, :] = new_ref[0, :, comp * KV_WIDTH:(comp + 1) * KV_WIDTH]


def _gather_past(cache4, layer, page_table, proj3):
    b, n_pages = page_table.shape
    s_new = proj3.shape[1]
    past_len = n_pages * PAGE_SIZE
    n_steps = n_pages // 2
    row_width = cache4.shape[-1]

    def page_index(half):
        def index(i, st, pt):
            return (layer, pt[i, jnp.minimum(2 * st + half, n_pages - 1)], 0, 0)
        return index

    page_spec = lambda half: pl.BlockSpec((1, 1, PAGE_SIZE, row_width), page_index(half))
    cmp_spec = pl.BlockSpec((1, 2 * PAGE_SIZE, KV_WIDTH), lambda i, st, pt: (i, jnp.minimum(st, n_steps - 1), 0))
    sel_spec = pl.BlockSpec((1, 2 * PAGE_SIZE, KV_WIDTH), lambda i, st, pt: (i, st, 0))
    kern = functools.partial(_gather_kernel, n_steps=n_steps, s_new=s_new)
    return pl.pallas_call(
        kern,
        grid_spec=pltpu.PrefetchScalarGridSpec(
            num_scalar_prefetch=1,
            grid=(b, n_steps + 1),
            in_specs=[page_spec(0), page_spec(1),
                      pl.BlockSpec((1, s_new, 4 * KV_WIDTH), lambda i, st, pt: (i, 0, C_KV // (4 * KV_WIDTH)))],
            out_specs=[cmp_spec, cmp_spec, sel_spec, sel_spec]),
        out_shape=[jax.ShapeDtypeStruct((b, past_len, KV_WIDTH), F32)] * 2
        + [jax.ShapeDtypeStruct((b, past_len + 2 * PAGE_SIZE, KV_WIDTH), F32)] * 2,
        compiler_params=_params("parallel", "arbitrary"),
        name="gather_past",
    )(page_table, cache4, cache4, proj3)


def _ssm_disc_kernel(lr_ref, li_ref, ls_ref, brt_ref, bit_ref, pr_ref, pi_ref, bbr_ref, bbi_ref):
    lr, li = lr_ref[...], li_ref[...]
    dt = jnp.exp(ls_ref[...])
    mag = jnp.exp(lr * dt)
    ab_re, ab_im = mag * jnp.cos(li * dt), mag * jnp.sin(li * dt)
    den = lr * lr + li * li
    co_re = ((ab_re - 1.0) * lr + ab_im * li) / den
    co_im = (ab_im * lr - (ab_re - 1.0) * li) / den
    brt, bit = brt_ref[...], bit_ref[...]
    bbr_ref[...] = co_re[:, None, :] * brt - co_im[:, None, :] * bit
    bbi_ref[...] = co_re[:, None, :] * bit + co_im[:, None, :] * brt
    pr, pi = ab_re, ab_im
    pr_ref[0], pi_ref[0] = pr, pi
    for r in range(1, SUBLANES):
        pr, pi = pr * ab_re - pi * ab_im, pr * ab_im + pi * ab_re
        pr_ref[r], pi_ref[r] = pr, pi


def _ssm_disc(lam_re, lam_im, log_step, b_re, b_im):
    g, n = lam_re.shape
    brt = jnp.swapaxes(b_re, 1, 2)
    bit = jnp.swapaxes(b_im, 1, 2)
    pw = jax.ShapeDtypeStruct((SUBLANES, g, n), F32)
    bb = jax.ShapeDtypeStruct(brt.shape, F32)
    return pl.pallas_call(_ssm_disc_kernel, out_shape=[pw, pw, bb, bb], name="ssm_disc")(
        lam_re, lam_im, log_step.reshape(g, 1), brt, bit)


def _ssm_kernel(su_ref, sz_ref, h0_ref, tab_ref, bmr_ref, bmi_ref, cm_ref, ds_ref, wg_ref,
                o_ref, st_ref, hr_ref, hi_ref, c_ref, *, tt, row_last):
    t = pl.program_id(1)

    @pl.when(t == 0)
    def _():
        c_ref[...] = h0_ref[0]

    u = su_ref[0]
    ub = u.astype(BF16)
    n_mm = SSM_WIDTH // 128
    for j in range(n_mm):
        uj = ub[:, j * 128:(j + 1) * 128]
        hr_ref[:, j * 512:(j + 1) * 512] = jnp.dot(uj, bmr_ref[j], preferred_element_type=F32)
        hi_ref[:, j * 512:(j + 1) * 512] = jnp.dot(uj, bmi_ref[j], preferred_element_type=F32)

    for cc in range(SSM_CH // SSM_LANE_CHUNK):
        lanes = slice(cc * SSM_LANE_CHUNK, (cc + 1) * SSM_LANE_CHUNK)
        tabs = [(tab_ref[2 * i, :, lanes], tab_ref[2 * i + 1, :, lanes]) for i in range(4)]

        def group(gi, carry):
            cr, ci = carry
            rows = pl.ds(pl.multiple_of(gi * SUBLANES, SUBLANES), SUBLANES)
            xr, xi = hr_ref[rows, lanes], hi_ref[rows, lanes]
            for lvl, shift in enumerate((1, 2, 4)):
                ar, ai = tabs[lvl]
                sr, si = pltpu.roll(xr, shift, axis=0), pltpu.roll(xi, shift, axis=0)
                xr, xi = xr + (ar * sr - ai * si), xi + (ar * si + ai * sr)
            pr, pi = tabs[3]
            xr, xi = xr + (pr * cr - pi * ci), xi + (pr * ci + pi * cr)
            hr_ref[rows, lanes], hi_ref[rows, lanes] = xr, xi
            return xr[SUBLANES - 1:SUBLANES, :], xi[SUBLANES - 1:SUBLANES, :]

        cr, ci = lax.fori_loop(0, tt // SUBLANES, group, (c_ref[0:1, lanes], c_ref[1:2, lanes]))
        c_ref[0:1, lanes], c_ref[1:2, lanes] = cr, ci

    @pl.when(t == pl.num_programs(1) - 1)
    def _():
        st_ref[0, 0:1, :] = hr_ref[row_last:row_last + 1, :]
        st_ref[0, 1:2, :] = hi_ref[row_last:row_last + 1, :]

    ys = []
    for j in range(n_mm):
        hcat = jnp.concatenate([hr_ref[:, j * 512:(j + 1) * 512], hi_ref[:, j * 512:(j + 1) * 512]], axis=-1)
        ys.append(jnp.dot(hcat.astype(BF16), cm_ref[j], preferred_element_type=F32))
    y = jnp.concatenate(ys, axis=-1) + ds_ref[...] * u
    z = jax.nn.gelu(y)
    out = z * jax.nn.sigmoid(jnp.dot(z.astype(BF16), wg_ref[...], preferred_element_type=F32))
    o_ref[0] = (out * _silu(sz_ref[0])).astype(BF16)


def _ssm(proj3, h0, tab, bmr, bmi, cm, d_skip, w_glu, *, s_valid):
    b, s, _ = proj3.shape
    tt = min(s, 256)
    row_last = (s_valid - 1) % tt
    n_mm = SSM_WIDTH // 128
    kern = functools.partial(_ssm_kernel, tt=tt, row_last=row_last)
    const3 = lambda i, t: (0, 0, 0)
    return pl.pallas_call(
        kern,
        grid=(b, s // tt),
        in_specs=[pl.BlockSpec((1, tt, SSM_WIDTH), lambda i, t: (i, t, C_SU // SSM_WIDTH)),
                  pl.BlockSpec((1, tt, SSM_WIDTH), lambda i, t: (i, t, C_SZ // SSM_WIDTH)),
                  pl.BlockSpec((1, 2, SSM_CH), lambda i, t: (i, 0, 0)),
                  pl.BlockSpec((8, SUBLANES, SSM_CH), const3),
                  pl.BlockSpec((n_mm, 128, 512), const3),
                  pl.BlockSpec((n_mm, 128, 512), const3),
                  pl.BlockSpec((n_mm, 1024, 128), const3),
                  pl.BlockSpec((1, SSM_WIDTH), lambda i, t: (0, 0)),
                  pl.BlockSpec((SSM_WIDTH, SSM_WIDTH), lambda i, t: (0, 0))],
        out_specs=[pl.BlockSpec((1, tt, SSM_WIDTH), lambda i, t: (i, t, 0)),
                   pl.BlockSpec((1, 2, SSM_CH), lambda i, t: (i, 0, 0))],
        out_shape=[jax.ShapeDtypeStruct((b, s, SSM_WIDTH), BF16),
                   jax.ShapeDtypeStruct((b, 2, SSM_CH), F32)],
        scratch_shapes=[pltpu.VMEM((tt, SSM_CH), F32), pltpu.VMEM((tt, SSM_CH), F32),
                        pltpu.VMEM((2, SSM_CH), F32)],
        compiler_params=_params("parallel", "arbitrary"),
        name="ssm",
    )(proj3, proj3, h0, tab, bmr, bmi, cm, d_skip, w_glu)


def _ssm_tables(lam_re, lam_im, log_step, b_re, b_im, c_re, c_im):
    pw_re, pw_im, bbr, bbi = _ssm_disc(lam_re, lam_im, log_step, b_re, b_im)
    pw_re = pw_re.reshape(SUBLANES, SSM_CH)
    pw_im = pw_im.reshape(SUBLANES, SSM_CH)
    row = jnp.arange(SUBLANES)[:, None]
    tabs = []
    for shift in (1, 2, 4):
        keep = row >= shift
        tabs += [jnp.where(keep, pw_re[shift - 1][None], 0.0), jnp.where(keep, pw_im[shift - 1][None], 0.0)]
    tab = jnp.stack(tabs + [pw_re, pw_im])
    eye = jnp.eye(8, dtype=F32)

    def b_blocks(bb):
        x = bb.reshape(SSM_GROUPS // 8, 8, SSM_GROUP_DIM, SSM_STATE)
        return jnp.einsum("jgcn,gh->jgchn", x, eye).reshape(SSM_GROUPS // 8, 128, 512).astype(BF16)

    def c_blocks(c):
        x = c.reshape(SSM_GROUPS // 8, 8, SSM_GROUP_DIM, SSM_STATE)
        return jnp.einsum("jgcn,gh->jgnhc", x, eye).reshape(SSM_GROUPS // 8, 512, 128)

    cm = jnp.concatenate([c_blocks(c_re), -c_blocks(c_im)], axis=1).astype(BF16)
    return tab, b_blocks(bbr), b_blocks(bbi), cm


def _merge_kernel(ap_ref, aa_ref, as_ref, wp_ref, wa_ref, ws_ref, g0_ref, g1_ref, g2_ref, o_ref):
    bp = jnp.dot(ap_ref[...], wp_ref[...], preferred_element_type=F32)
    ba = jnp.dot(aa_ref[...], wa_ref[...], preferred_element_type=F32)
    bs = jnp.dot(as_ref[...], ws_ref[...], preferred_element_type=F32)
    o_ref[...] = (jax.nn.sigmoid(g0_ref[...]) * bp + jax.nn.sigmoid(g1_ref[...]) * ba
                  + jax.nn.sigmoid(g2_ref[...]) * bs).astype(BF16)


def _merge(a_pool, a_att, a_ssm, w_p, w_a, w_s, proj2):
    m = a_pool.shape[0]
    tm = min(m, 512)
    tn = 1024
    act = pl.BlockSpec((tm, 1024), lambda i, n: (i, 0))
    wsp = pl.BlockSpec((1024, tn), lambda i, n: (0, n))
    gate = lambda r: pl.BlockSpec((tm, tn), lambda i, n: (i, (C_MG + r * D_MODEL) // tn + n))
    return pl.pallas_call(
        _merge_kernel,
        grid=(m // tm, D_MODEL // tn),
        in_specs=[act, act, act, wsp, wsp, wsp, gate(0), gate(1), gate(2)],
        out_specs=pl.BlockSpec((tm, tn), lambda i, n: (i, n)),
        out_shape=jax.ShapeDtypeStruct((m, D_MODEL), BF16),
        compiler_params=_params("parallel", "arbitrary"),
        name="merge",
    )(a_pool, a_att, a_ssm, w_p, w_a, w_s, proj2, proj2, proj2)


def _outproj_kernel(m_ref, w_ref, g_ref, x_ref, o_ref):
    y = jnp.dot(m_ref[...], w_ref[...], preferred_element_type=F32)
    r = lax.rsqrt(jnp.mean(y * y, axis=-1, keepdims=True) + EPS)
    o_ref[...] = x_ref[...] + y * r * g_ref[...]


def _outproj(merged, w_out, g_post, x2d):
    m = merged.shape[0]
    tm = min(m, 512)
    row = pl.BlockSpec((tm, D_MODEL), lambda i: (i, 0))
    return pl.pallas_call(
        _outproj_kernel,
        grid=(m // tm,),
        in_specs=[row, pl.BlockSpec((D_MODEL, D_MODEL), lambda i: (0, 0)),
                  pl.BlockSpec((1, D_MODEL), lambda i: (0, 0)), row],
        out_specs=row,
        out_shape=jax.ShapeDtypeStruct((m, D_MODEL), F32),
        compiler_params=_params("parallel"),
        name="outproj",
    )(merged, w_out, g_post, x2d)


def _prep_layer_weights(l, g_pre, g_post, w_in, w_pool, pool_scale, pe_cmp, w_phi, lam_re, lam_im, log_step,
                        b_re, b_im, c_re, c_im, d_skip, w_glu, w_br_pool, w_br_nsa, w_br_ssm, w_out):
    w = w_in[l]
    kv0 = 2 * POOL_WIDTH + ATT_WIDTH
    ag0 = kv0 + 6 * KV_WIDTH
    az0 = ag0 + N_HEADS * N_ATT_GATES
    w_main = jnp.concatenate([w[:, :kv0], w[:, az0:], w[:, kv0:ag0]], axis=1).astype(BF16)
    w_gate = jnp.pad(w[:, ag0:az0], ((0, 0), (0, N_GATE_PAD - N_HEADS * N_ATT_GATES))).astype(BF16)
    eye = jnp.eye(N_KV_HEADS, dtype=F32)
    w_cmp = [jnp.einsum("lde,kh->lkdhe", w_phi[l, c], eye).reshape(BLOCK * KV_WIDTH, KV_WIDTH).astype(BF16)
             for c in range(2)]
    pe = [jnp.broadcast_to(pe_cmp[l, c][:, None, :], (BLOCK, N_KV_HEADS, HEAD_DIM)).reshape(1, BLOCK * KV_WIDTH)
          for c in range(2)]
    tab, bmr, bmi, cm = _ssm_tables(lam_re[l], lam_im[l], log_step[l], b_re[l], b_im[l], c_re[l], c_im[l])
    return dict(
        g_pre=g_pre[l].reshape(1, D_MODEL), g_post=g_post[l].reshape(1, D_MODEL),
        w_main=w_main, w_gate=w_gate, w_pool=w_pool[l].astype(BF16), pool_scale=pool_scale[l].reshape(1, POOL_WIDTH),
        w_cmp=w_cmp, pe=pe, tab=tab, bmr=bmr, bmi=bmi, cm=cm, d_skip=d_skip[l].reshape(1, SSM_WIDTH),
        w_glu=w_glu[l].astype(BF16), w_br_pool=w_br_pool[l].astype(BF16), w_br_nsa=w_br_nsa[l].astype(BF16),
        w_br_ssm=w_br_ssm[l].astype(BF16), w_out=w_out[l].astype(BF16))


def _expand_matrix(nbp, n_keys):
    return (jnp.arange(nbp)[:, None] == (jnp.arange(n_keys) // BLOCK)[None, :]).astype(BF16)


def _layer(x3, lw, *, pos0, s_valid, pool_prefix, ssm_h0, win_prefix, paged):
    b, s, _ = x3.shape
    x2 = x3.reshape(b * s, D_MODEL)
    proj2 = _inproj(x2, lw["g_pre"], lw["w_main"], 768)
    gates3 = _inproj(x2, lw["g_pre"], lw["w_gate"], N_GATE_PAD).reshape(b, s, N_GATE_PAD)
    proj3 = proj2.reshape(b, s, N_MAIN)

    prefix16 = jnp.pad(pool_prefix, ((0, 0), (1, 0), (0, 0)))
    a_pool, pool_tail = _pool(proj3, prefix16, lw["w_pool"], lw["pool_scale"], q0=pos0, s_valid=s_valid)
    pool_state = pool_tail[:, 1:]

    kvn = proj3[:, :, C_KV:]
    if paged is None:
        tq = 128
        nb_total = s // BLOCK
        nbp = nb_total
        x_k = kvn[:, :, 0:KV_WIDTH].reshape(b * nb_total, BLOCK * KV_WIDTH)
        x_v = kvn[:, :, KV_WIDTH:2 * KV_WIDTH].reshape(b * nb_total, BLOCK * KV_WIDTH)
        n_keys = s
        k_sel, v_sel = proj3, proj3
        kcol, vcol = (C_KV + 2 * KV_WIDTH) // KV_WIDTH, (C_KV + 3 * KV_WIDTH) // KV_WIDTH
        n_cmp = nb_total
    else:
        tq = s
        cache4, layer, page_table = paged
        past_len = page_table.shape[1] * PAGE_SIZE
        x_k, x_v, k_sel, v_sel = _gather_past(cache4, layer, page_table, proj3)
        n_cmp = past_len // BLOCK
        x_k = x_k.reshape(b * n_cmp, BLOCK * KV_WIDTH)
        x_v = x_v.reshape(b * n_cmp, BLOCK * KV_WIDTH)
        nb_total = -(-(past_len + s_valid) // BLOCK)
        nbp = -(-(nb_total) // 16) * 16
        n_keys = k_sel.shape[1]
        kcol = vcol = 0
    kc = _compress(x_k, lw["pe"][0], lw["w_cmp"][0]).reshape(b, n_cmp, KV_WIDTH)
    vc = _compress(x_v, lw["pe"][1], lw["w_cmp"][1]).reshape(b, n_cmp, KV_WIDTH)
    if nbp > n_cmp:
        kc = jnp.pad(kc, ((0, 0), (0, nbp - n_cmp), (0, 0)))
        vc = jnp.pad(vc, ((0, 0), (0, nbp - n_cmp), (0, 0)))
    o_cmp, sel = _cmpsel(proj3, kc, vc, pos0=pos0, tq=tq, nb_total=nb_total)
    o_sel = _selattn(proj3, k_sel, v_sel, kcol, vcol, sel, _expand_matrix(nbp, n_keys), pos0=pos0, tq=tq)

    tq_win = min(s, 256)
    n_rows = ((s - tq_win) // KEY_TILE + -(-(WINDOW + tq_win) // KEY_TILE)) * KEY_TILE
    k_new, v_new = kvn[:, :, 4 * KV_WIDTH:5 * KV_WIDTH], kvn[:, :, 5 * KV_WIDTH:6 * KV_WIDTH]
    pad_rows = n_rows - WINDOW - s
    kwin = jnp.pad(jnp.concatenate([win_prefix[:, :, 0].reshape(b, WINDOW, KV_WIDTH), k_new], axis=1),
                   ((0, 0), (0, pad_rows), (0, 0)))
    vwin = jnp.pad(jnp.concatenate([win_prefix[:, :, 1].reshape(b, WINDOW, KV_WIDTH), v_new], axis=1),
                   ((0, 0), (0, pad_rows), (0, 0)))
    o_win = _winattn(proj3, kwin, vwin, pos0=pos0, row_pos0=pos0 - WINDOW, tq=tq_win)
    a_att = _attcomb(o_cmp, o_sel, o_win, gates3, proj3)

    a_ssm, ssm_state = _ssm(proj3, ssm_h0.reshape(b, 2, SSM_CH), lw["tab"], lw["bmr"], lw["bmi"], lw["cm"],
                            lw["d_skip"], lw["w_glu"], s_valid=s_valid)

    merged = _merge(a_pool.reshape(b * s, POOL_WIDTH), a_att.reshape(b * s, ATT_WIDTH),
                    a_ssm.reshape(b * s, SSM_WIDTH), lw["w_br_pool"], lw["w_br_nsa"], lw["w_br_ssm"], proj2)
    x_new = _outproj(merged, lw["w_out"], lw["g_post"], x2).reshape(b, s, D_MODEL)

    kv_rows = kvn[:, :s_valid, :4 * KV_WIDTH].reshape(b, s_valid, 4, N_KV_HEADS, HEAD_DIM)
    win_k = kwin[:, s_valid:s_valid + WINDOW].reshape(b, WINDOW, N_KV_HEADS, HEAD_DIM)
    win_v = vwin[:, s_valid:s_valid + WINDOW].reshape(b, WINDOW, N_KV_HEADS, HEAD_DIM)
    win_state = jnp.stack([win_k, win_v], axis=2)
    return x_new, kv_rows, win_state, pool_state, ssm_state.reshape(b, 2, SSM_GROUPS, SSM_STATE)


def kernel(x_prompt, x_sample, cache_kv, page_table, state_win_kv, state_pool, state_ssm, g_pre, g_post, w_in, w_pool, pool_scale, pe_cmp, w_phi, lam_re, lam_im, log_step, b_re, b_im, c_re, c_im, d_skip, w_glu, w_br_pool, w_br_nsa, w_br_ssm, w_out):
    depth = w_in.shape[0]
    bp, sp, _ = x_prompt.shape
    bd, sd, _ = x_sample.shape
    n_pool = cache_kv.shape[1]
    past_len = page_table.shape[1] * PAGE_SIZE
    assert state_win_kv.shape[2] == WINDOW and past_len >= WINDOW and sp >= WINDOW
    cache4 = cache_kv.reshape(depth, n_pool, PAGE_SIZE, 4 * KV_WIDTH)
    sd_pad = -(-sd // SUBLANES) * SUBLANES
    yp = x_prompt
    ys = jnp.pad(x_sample, ((0, 0), (0, sd_pad - sd), (0, 0)))
    zeros_win = jnp.zeros((bp, WINDOW, 2, N_KV_HEADS, HEAD_DIM), F32)
    zeros_pool = jnp.zeros((bp, POOL_STATE, POOL_WIDTH), F32)
    zeros_ssm = jnp.zeros((bp, 2, SSM_GROUPS, SSM_STATE), F32)
    outs_p, outs_s = [], []
    for l in range(depth):
        lw = _prep_layer_weights(l, g_pre, g_post, w_in, w_pool, pool_scale, pe_cmp, w_phi, lam_re, lam_im, log_step,
                                 b_re, b_im, c_re, c_im, d_skip, w_glu, w_br_pool, w_br_nsa, w_br_ssm, w_out)
        yp, *rp = _layer(yp, lw, pos0=0, s_valid=sp, pool_prefix=zeros_pool, ssm_h0=zeros_ssm,
                         win_prefix=zeros_win, paged=None)
        ys, *rs = _layer(ys, lw, pos0=past_len, s_valid=sd, pool_prefix=state_pool[l], ssm_h0=state_ssm[l],
                         win_prefix=state_win_kv[l], paged=(cache4, l, page_table))
        outs_p.append(rp)
        outs_s.append(rs)
    stack = lambda outs, i: jnp.stack([o[i] for o in outs])
    return (yp, ys[:, :sd], stack(outs_p, 0), stack(outs_s, 0), stack(outs_p, 1), stack(outs_s, 1),
            stack(outs_p, 2), stack(outs_s, 2), stack(outs_p, 3), stack(outs_s, 3))
```

```python
import functools

import jax
import jax.numpy as jnp
from jax import lax
from jax.experimental import pallas as pl
from jax.experimental.pallas import tpu as pltpu

F32 = jnp.float32
BF16 = jnp.bfloat16
I32 = jnp.int32

D_MODEL = 2048
PAGE_SIZE = 128
POOL_WIDTH = D_MODEL // 2
POOL_WINDOWS = (2, 4, 8, 16)
POOL_GROUP_DIM = POOL_WIDTH // len(POOL_WINDOWS)
POOL_STATE = max(POOL_WINDOWS) - 1
N_HEADS = 16
HEAD_DIM = 64
N_KV_HEADS = 4
GQA = N_HEADS // N_KV_HEADS
ATT_WIDTH = N_HEADS * HEAD_DIM
KV_WIDTH = N_KV_HEADS * HEAD_DIM
BLOCK = 64
TOP_N = 16
WINDOW = 512
N_ATT_GATES = 3
SSM_WIDTH = D_MODEL // 2
SSM_GROUP_DIM = 16
SSM_GROUPS = SSM_WIDTH // SSM_GROUP_DIM
SSM_STATE = 64
SSM_CH = SSM_GROUPS * SSM_STATE
N_BRANCH = 3
EPS = 1e-6
NEG = -1e30
FORCE = 1e4
ATT_SCALE = HEAD_DIM ** -0.5

C_PU, C_PZ, C_Q, C_AZ, C_SU, C_SZ = (i * 1024 for i in range(6))
C_MG = 6 * 1024
C_KV = C_MG + N_BRANCH * D_MODEL
N_MAIN = C_KV + 6 * KV_WIDTH
N_GATE_PAD = 128
KV_COL = C_KV // KV_WIDTH

VMEM_LIMIT_BYTES = 52 * 1024 * 1024
SUBLANES = 8
KEY_TILE = 256
BLOCKS_PER_TILE = KEY_TILE // BLOCK
BLOCKS_PER_PAGE = PAGE_SIZE // BLOCK
PAGES_PER_STEP = 8
SSM_LANE_CHUNK = 1024

NT_DIMS = (((1,), (1,)), ((), ()))


def _params(*sem):
    return pltpu.CompilerParams(dimension_semantics=sem, vmem_limit_bytes=VMEM_LIMIT_BYTES)


def _silu(x):
    return x * jax.nn.sigmoid(x)


def _inproj_kernel(x_ref, g_ref, w_ref, o_ref, h_ref):
    @pl.when(pl.program_id(1) == 0)
    def _():
        x = x_ref[...]
        r = lax.rsqrt(jnp.mean(x * x, axis=-1, keepdims=True) + EPS)
        h_ref[...] = (x * r * g_ref[...]).astype(BF16)

    o_ref[...] = jnp.dot(h_ref[...], w_ref[...], preferred_element_type=F32)


def _inproj(x2d, g, w, tn):
    m, d = x2d.shape
    n = w.shape[1]
    tm = min(m, 1024)
    return pl.pallas_call(
        _inproj_kernel,
        grid=(m // tm, n // tn),
        in_specs=[pl.BlockSpec((tm, d), lambda i, j: (i, 0)),
                  pl.BlockSpec((1, d), lambda i, j: (0, 0)),
                  pl.BlockSpec((d, tn), lambda i, j: (0, j))],
        out_specs=pl.BlockSpec((tm, tn), lambda i, j: (i, j)),
        out_shape=jax.ShapeDtypeStruct((m, n), F32),
        scratch_shapes=[pltpu.VMEM((tm, d), BF16)],
        compiler_params=_params("parallel", "arbitrary"),
        name="inproj",
    )(x2d, g, w)


def _pool_kernel(pu_ref, pz_ref, pre_ref, wp_ref, sc_ref, o_ref, st_ref, e_ref, *, tt, q0, rows_last):
    t = pl.program_id(1)
    hist = POOL_STATE + 1

    @pl.when(t == 0)
    def _():
        e_ref[0:hist, :] = pre_ref[0]

    @pl.when(t > 0)
    def _():
        e_ref[0:hist, :] = e_ref[tt:tt + hist, :]

    u = pu_ref[0]
    e_ref[hist:hist + tt, :] = u
    pos = q0 + t * tt + lax.broadcasted_iota(I32, (tt, 1), 0)
    ys = []
    for gi, w in enumerate(POOL_WINDOWS):
        lo = gi * POOL_GROUP_DIM
        tot = e_ref[hist:hist + tt, lo:lo + POOL_GROUP_DIM]
        for k in range(1, w):
            tot = tot + e_ref[hist - k:hist - k + tt, lo:lo + POOL_GROUP_DIM]
        cnt = jnp.minimum(pos + 1, w).astype(F32)
        diff = tot / cnt - u[:, lo:lo + POOL_GROUP_DIM]
        ys.append(jnp.dot(diff.astype(BF16), wp_ref[gi], preferred_element_type=F32))
    y = jnp.concatenate(ys, axis=-1) * sc_ref[...]
    o_ref[0] = (y * _silu(pz_ref[0])).astype(BF16)

    @pl.when(t == pl.num_programs(1) - 1)
    def _():
        st_ref[0] = e_ref[rows_last:rows_last + hist, :]


def _pool(proj3, prefix16, w_pool, pool_scale, *, q0, s_valid):
    b, s, _ = proj3.shape
    tt = min(s, 512)
    hist = POOL_STATE + 1
    rows_last = ((s_valid - 1) % tt) + 1
    kern = functools.partial(_pool_kernel, tt=tt, q0=q0, rows_last=rows_last)
    return pl.pallas_call(
        kern,
        grid=(b, s // tt),
        in_specs=[pl.BlockSpec((1, tt, POOL_WIDTH), lambda i, t: (i, t, C_PU // POOL_WIDTH)),
                  pl.BlockSpec((1, tt, POOL_WIDTH), lambda i, t: (i, t, C_PZ // POOL_WIDTH)),
                  pl.BlockSpec((1, hist, POOL_WIDTH), lambda i, t: (i, 0, 0)),
                  pl.BlockSpec((len(POOL_WINDOWS), POOL_GROUP_DIM, POOL_GROUP_DIM), lambda i, t: (0, 0, 0)),
                  pl.BlockSpec((1, POOL_WIDTH), lambda i, t: (0, 0))],
        out_specs=[pl.BlockSpec((1, tt, POOL_WIDTH), lambda i, t: (i, t, 0)),
                   pl.BlockSpec((1, hist, POOL_WIDTH), lambda i, t: (i, 0, 0))],
        out_shape=[jax.ShapeDtypeStruct((b, s, POOL_WIDTH), BF16),
                   jax.ShapeDtypeStruct((b, hist, POOL_WIDTH), F32)],
        scratch_shapes=[pltpu.VMEM((hist + tt, POOL_WIDTH), F32)],
        compiler_params=_params("parallel", "arbitrary"),
        name="pool",
    )(proj3, proj3, prefix16, w_pool, pool_scale)


def _compress_kernel(x_ref, pe_ref, w_ref, o_ref):
    def one_row(l, acc):
        x = x_ref[:, l, :] + pe_ref[pl.ds(l, 1), :]
        return acc + jnp.dot(x.astype(BF16), w_ref[l], preferred_element_type=F32)

    o_ref[...] = lax.fori_loop(0, BLOCK, one_row, jnp.zeros(o_ref.shape, F32))


def _compress(x3, col, pe, w):
    nb = x3.shape[0]
    tm = min(nb, 128)
    return pl.pallas_call(
        _compress_kernel,
        grid=(nb // tm,),
        in_specs=[pl.BlockSpec((tm, BLOCK, KV_WIDTH), lambda i: (i, 0, col)),
                  pl.BlockSpec((BLOCK, KV_WIDTH), lambda i: (0, 0)),
                  pl.BlockSpec((BLOCK, KV_WIDTH, KV_WIDTH), lambda i: (0, 0, 0))],
        out_specs=pl.BlockSpec((tm, KV_WIDTH), lambda i: (i, 0)),
        out_shape=jax.ShapeDtypeStruct((nb, KV_WIDTH), F32),
        compiler_params=_params("parallel"),
        name="compress",
    )(x3, pe, w)


def _page_specs(layer, n_pages, comp_block):
    def spec(i):
        def index(bi, st, pt):
            return (layer, pt[bi, jnp.minimum(st * PAGES_PER_STEP + i, n_pages - 1)], comp_block, 0, 0)
        return pl.BlockSpec((1, 1, 2, KV_WIDTH, PAGE_SIZE), index)
    return [spec(i) for i in range(PAGES_PER_STEP)]


def _gather_cmp_kernel(pt_ref, *refs):
    del pt_ref
    pages, (xk_ref, xv_ref) = refs[:PAGES_PER_STEP], refs[PAGES_PER_STEP:]
    for i, page in enumerate(pages):
        for comp, dst in enumerate((xk_ref, xv_ref)):
            rows = page[0, 0, comp].T
            dst[BLOCKS_PER_PAGE * i:BLOCKS_PER_PAGE * (i + 1)] = rows.reshape(BLOCKS_PER_PAGE, BLOCK, KV_WIDTH)


def _gather_cmp(cache5, layer, page_table):
    b, n_pages = page_table.shape
    n_steps = n_pages // PAGES_PER_STEP
    blocks_per_step = PAGES_PER_STEP * BLOCKS_PER_PAGE
    out_spec = pl.BlockSpec((blocks_per_step, BLOCK, KV_WIDTH), lambda bi, st, pt: (bi * n_steps + st, 0, 0))
    out = jax.ShapeDtypeStruct((b * n_pages * BLOCKS_PER_PAGE, BLOCK, KV_WIDTH), F32)
    return pl.pallas_call(
        _gather_cmp_kernel,
        grid_spec=pltpu.PrefetchScalarGridSpec(
            num_scalar_prefetch=1, grid=(b, n_steps),
            in_specs=_page_specs(layer, n_pages, 0), out_specs=[out_spec, out_spec]),
        out_shape=[out, out],
        compiler_params=_params("parallel", "arbitrary"),
        name="gather_cmp",
    )(page_table, *([cache5] * PAGES_PER_STEP))


def _cmpsel_kernel(q_ref, kc_ref, vct_ref, ocmp_ref, sel_ref, s_ref, *, tq, nbp, pos0, n_sel):
    j = pl.program_id(1)
    qs = (q_ref[0] * ATT_SCALE).astype(BF16)
    kc = kc_ref[0].astype(BF16)
    vct = vct_ref[0].astype(BF16)
    nidx = lax.broadcasted_iota(I32, (nbp, tq), 0)
    pos = pos0 + j * tq + lax.broadcasted_iota(I32, (nbp, tq), 1)
    cvalid = nidx < lax.shift_right_logical(pos + 1, 6)
    cur = lax.shift_right_logical(pos, 6)
    forced = (nidx == 0) | (nidx == cur) | (nidx == cur - 1)
    outs = []
    for k in range(N_KV_HEADS):
        ksl = slice(k * HEAD_DIM, (k + 1) * HEAD_DIM)
        imp = jnp.zeros((nbp, tq), F32)
        for g in range(GQA):
            h = k * GQA + g
            sc = lax.dot_general(kc[:, ksl], qs[:, h * HEAD_DIM:(h + 1) * HEAD_DIM], NT_DIMS,
                                 preferred_element_type=F32)
            sc = jnp.where(cvalid, sc, NEG)
            e = jnp.exp(sc - jnp.max(sc, axis=0, keepdims=True))
            pc = jnp.where(cvalid, e / jnp.sum(e, axis=0, keepdims=True), 0.0)
            imp = imp + pc
            outs.append(jnp.dot(vct[ksl, :], pc.astype(BF16), preferred_element_type=F32))
        s_ref[:, k * tq:(k + 1) * tq] = jnp.where(forced, FORCE, jnp.where(nidx < cur, imp, NEG))
    ocmp_ref[0] = jnp.concatenate(outs, axis=0)

    score = s_ref[...]
    rows = lax.broadcasted_iota(I32, score.shape, 0)

    def pick_one(_, carry):
        work, sel = carry
        top = jnp.max(work, axis=0, keepdims=True)
        first = jnp.min(jnp.where(work == top, rows, nbp), axis=0, keepdims=True)
        hit = rows == first
        return jnp.where(hit, -jnp.inf, work), jnp.where(hit, 1.0, sel)

    _, sel = lax.fori_loop(0, n_sel, pick_one, (score, jnp.zeros_like(score)))
    sel_ref[0] = jnp.where(score > NEG / 2, sel, 0.0)


def _cmpsel(proj3, kc, vct, *, pos0, tq, nb_total):
    b, s, _ = proj3.shape
    nbp = kc.shape[1]
    kern = functools.partial(_cmpsel_kernel, tq=tq, nbp=nbp, pos0=pos0, n_sel=min(TOP_N, nb_total))
    lanes = N_KV_HEADS * tq
    return pl.pallas_call(
        kern,
        grid=(b, s // tq),
        in_specs=[pl.BlockSpec((1, tq, ATT_WIDTH), lambda i, j: (i, j, C_Q // ATT_WIDTH)),
                  pl.BlockSpec((1, nbp, KV_WIDTH), lambda i, j: (i, 0, 0)),
                  pl.BlockSpec((1, KV_WIDTH, nbp), lambda i, j: (i, 0, 0))],
        out_specs=[pl.BlockSpec((1, ATT_WIDTH, tq), lambda i, j: (i, 0, j)),
                   pl.BlockSpec((1, nbp, lanes), lambda i, j: (i, 0, j))],
        out_shape=[jax.ShapeDtypeStruct((b, ATT_WIDTH, s), F32),
                   jax.ShapeDtypeStruct((b, nbp, N_KV_HEADS * s), F32)],
        scratch_shapes=[pltpu.VMEM((nbp, lanes), F32)],
        compiler_params=_params("parallel", "parallel"),
        name="cmpsel",
    )(proj3, kc, vct)


def _attn_init(q_ref, qs_ref, m_ref, l_ref, acc_ref, tq):
    qs = (q_ref[0] * ATT_SCALE).astype(BF16)
    for k in range(N_KV_HEADS):
        for g in range(GQA):
            h = k * GQA + g
            qs_ref[k, g * tq:(g + 1) * tq, :] = qs[:, h * HEAD_DIM:(h + 1) * HEAD_DIM]
    m_ref[...] = jnp.full_like(m_ref, NEG)
    l_ref[...] = jnp.zeros_like(l_ref)
    acc_ref[...] = jnp.zeros_like(acc_ref)


def _attn_step(k, k_tile, vt_tile, ok, qs_ref, m_ref, l_ref, acc_ref):
    ksl = slice(k * HEAD_DIM, (k + 1) * HEAD_DIM)
    s = lax.dot_general(k_tile[:, ksl], qs_ref[k], NT_DIMS, preferred_element_type=F32)
    s = jnp.where(ok, s, NEG)
    m_old = m_ref[k]
    m_new = jnp.maximum(m_old, jnp.max(s, axis=0, keepdims=True))
    p = jnp.exp(s - m_new)
    alpha = jnp.exp(m_old - m_new)
    l_ref[k] = alpha * l_ref[k] + jnp.sum(p, axis=0, keepdims=True)
    m_ref[k] = m_new
    acc_ref[k] = alpha * acc_ref[k] + jnp.dot(vt_tile[ksl, :], p.astype(BF16), preferred_element_type=F32)


def _attn_finish(o_ref, l_ref, acc_ref, tq):
    outs = []
    for k in range(N_KV_HEADS):
        o = acc_ref[k] / l_ref[k]
        for g in range(GQA):
            outs.append(o[:, g * tq:(g + 1) * tq])
    o_ref[0] = jnp.concatenate(outs, axis=0)


def _attn_scratch(tq):
    r = GQA * tq
    return [pltpu.VMEM((N_KV_HEADS, r, HEAD_DIM), BF16),
            pltpu.VMEM((N_KV_HEADS, 1, r), F32),
            pltpu.VMEM((N_KV_HEADS, 1, r), F32),
            pltpu.VMEM((N_KV_HEADS, HEAD_DIM, r), F32)]


def _lane_query_pos(shape, tq, first):
    return first + (lax.broadcasted_iota(I32, shape, 1) & (tq - 1))


def _selattn_kernel(q_ref, k_ref, vt_ref, sel_ref, o_ref, qs_ref, m_ref, l_ref, acc_ref, *, tq, pos0):
    j = pl.program_id(1)
    kt = pl.program_id(2)
    r = GQA * tq

    @pl.when(kt == 0)
    def _():
        _attn_init(q_ref, qs_ref, m_ref, l_ref, acc_ref, tq)

    @pl.when(kt <= (pos0 + (j + 1) * tq - 1) // KEY_TILE)
    def _():
        k_tile = k_ref[0].astype(BF16)
        vt_tile = vt_ref[0].astype(BF16)
        causal = (kt * KEY_TILE + lax.broadcasted_iota(I32, (KEY_TILE, r), 0)
                  <= _lane_query_pos((KEY_TILE, r), tq, pos0 + j * tq))
        for k in range(N_KV_HEADS):
            chosen = []
            for i in range(BLOCKS_PER_TILE):
                row = sel_ref[0, pl.ds(kt * BLOCKS_PER_TILE + i, 1), k * tq:(k + 1) * tq]
                chosen.append(jnp.broadcast_to(jnp.concatenate([row] * GQA, axis=1), (BLOCK, r)))
            ok = (jnp.concatenate(chosen, axis=0) > 0.5) & causal
            _attn_step(k, k_tile, vt_tile, ok, qs_ref, m_ref, l_ref, acc_ref)

    @pl.when(kt == pl.num_programs(2) - 1)
    def _():
        _attn_finish(o_ref, l_ref, acc_ref, tq)


def _selattn(proj3, vt, sel, *, pos0, tq):
    b, s, _ = proj3.shape
    assert tq & (tq - 1) == 0 and pos0 % KEY_TILE == 0
    nkt = s // KEY_TILE
    nbp = sel.shape[1]

    def last_tile(j):
        return (pos0 + (j + 1) * tq - 1) // KEY_TILE

    kern = functools.partial(_selattn_kernel, tq=tq, pos0=pos0)
    return pl.pallas_call(
        kern,
        grid=(b, s // tq, nkt),
        in_specs=[pl.BlockSpec((1, tq, ATT_WIDTH), lambda i, j, kt: (i, j, C_Q // ATT_WIDTH)),
                  pl.BlockSpec((1, KEY_TILE, KV_WIDTH), lambda i, j, kt: (i, jnp.minimum(kt, last_tile(j)), KV_COL + 2)),
                  pl.BlockSpec((1, KV_WIDTH, KEY_TILE), lambda i, j, kt: (i, 0, jnp.minimum(kt, last_tile(j)))),
                  pl.BlockSpec((1, nbp, N_KV_HEADS * tq), lambda i, j, kt: (i, 0, j))],
        out_specs=pl.BlockSpec((1, ATT_WIDTH, tq), lambda i, j, kt: (i, 0, j)),
        out_shape=jax.ShapeDtypeStruct((b, ATT_WIDTH, s), F32),
        scratch_shapes=_attn_scratch(tq),
        compiler_params=_params("parallel", "parallel", "arbitrary"),
        name="selattn",
    )(proj3, proj3, vt, sel)


def _winattn_kernel(q_ref, k_ref, vt_ref, o_ref, qs_ref, m_ref, l_ref, acc_ref, *, tq, pos0, row_pos0, n_sub):
    j = pl.program_id(1)
    t = pl.program_id(2)
    r = GQA * tq

    @pl.when(t == 0)
    def _():
        _attn_init(q_ref, qs_ref, m_ref, l_ref, acc_ref, tq)

    tile = (j * tq) // KEY_TILE + (n_sub - 1 - t)
    kp = row_pos0 + tile * KEY_TILE + lax.broadcasted_iota(I32, (KEY_TILE, r), 0)
    pq = _lane_query_pos((KEY_TILE, r), tq, pos0 + j * tq)
    ok = (kp >= 0) & (kp <= pq) & (kp > pq - WINDOW)
    k_tile = k_ref[0].astype(BF16)
    vt_tile = vt_ref[0].astype(BF16)
    for k in range(N_KV_HEADS):
        _attn_step(k, k_tile, vt_tile, ok, qs_ref, m_ref, l_ref, acc_ref)

    @pl.when(t == n_sub - 1)
    def _():
        _attn_finish(o_ref, l_ref, acc_ref, tq)


def _winattn(proj3, kwin, vwin_t, *, pos0, row_pos0, tq):
    b, s, _ = proj3.shape
    assert tq & (tq - 1) == 0 and (tq % KEY_TILE == 0 or s == tq)
    n_sub = -(-(WINDOW + tq) // KEY_TILE)
    assert kwin.shape[1] >= ((s - tq) // KEY_TILE + n_sub) * KEY_TILE

    def tile(j, t):
        return (j * tq) // KEY_TILE + (n_sub - 1 - t)

    kern = functools.partial(_winattn_kernel, tq=tq, pos0=pos0, row_pos0=row_pos0, n_sub=n_sub)
    return pl.pallas_call(
        kern,
        grid=(b, s // tq, n_sub),
        in_specs=[pl.BlockSpec((1, tq, ATT_WIDTH), lambda i, j, t: (i, j, C_Q // ATT_WIDTH)),
                  pl.BlockSpec((1, KEY_TILE, KV_WIDTH), lambda i, j, t: (i, tile(j, t), 0)),
                  pl.BlockSpec((1, KV_WIDTH, KEY_TILE), lambda i, j, t: (i, 0, tile(j, t)))],
        out_specs=pl.BlockSpec((1, ATT_WIDTH, tq), lambda i, j, t: (i, 0, j)),
        out_shape=jax.ShapeDtypeStruct((b, ATT_WIDTH, s), F32),
        scratch_shapes=_attn_scratch(tq),
        compiler_params=_params("parallel", "parallel", "arbitrary"),
        name="winattn",
    )(proj3, kwin, vwin_t)


def _selattn_paged_kernel(pt_ref, *refs, tq, n_steps):
    del pt_ref
    pages = refs[:PAGES_PER_STEP]
    q_ref, new_ref, sel_ref, o_ref, qbd_ref, selr_ref, m_ref, l_ref, acc_ref = refs[PAGES_PER_STEP:]
    st = pl.program_id(1)
    rows = N_HEADS * tq

    def flash(s, ok, v, v_is_transposed):
        s = jnp.where(ok, s, NEG)
        m_old = m_ref[...]
        m_new = jnp.maximum(m_old, jnp.max(s, axis=-1, keepdims=True))
        p = jnp.exp(s - m_new)
        alpha = jnp.exp(m_old - m_new)
        l_ref[...] = alpha * l_ref[...] + jnp.sum(p, axis=-1, keepdims=True)
        m_ref[...] = m_new
        if v_is_transposed:
            pv = lax.dot_general(p.astype(BF16), v, NT_DIMS, preferred_element_type=F32)
        else:
            pv = jnp.dot(p.astype(BF16), v, preferred_element_type=F32)
        acc_ref[...] = alpha * acc_ref[...] + pv

    @pl.when(st == 0)
    def _():
        qs = (q_ref[0] * ATT_SCALE).astype(BF16)
        qbd_ref[...] = jnp.zeros_like(qbd_ref)
        for k in range(N_KV_HEADS):
            for g in range(GQA):
                h = k * GQA + g
                qbd_ref[h * tq:(h + 1) * tq, k * HEAD_DIM:(k + 1) * HEAD_DIM] = qs[:, h * HEAD_DIM:(h + 1) * HEAD_DIM]
                selr_ref[:, h * tq:(h + 1) * tq] = sel_ref[0, :, k * tq:(k + 1) * tq]
        m_ref[...] = jnp.full_like(m_ref, NEG)
        l_ref[...] = jnp.zeros_like(l_ref)
        acc_ref[...] = jnp.zeros_like(acc_ref)

    @pl.when(st < n_steps)
    def _():
        for i, page in enumerate(pages):
            blk = (st * PAGES_PER_STEP + i) * BLOCKS_PER_PAGE
            chosen = jnp.concatenate(
                [jnp.broadcast_to(selr_ref[pl.ds(blk + c, 1), :], (BLOCK, rows)) for c in range(BLOCKS_PER_PAGE)],
                axis=0)
            s = jnp.dot(qbd_ref[...], page[0, 0, 0].astype(BF16), preferred_element_type=F32)
            flash(s, chosen.T > 0.5, page[0, 0, 1].astype(BF16), True)

    @pl.when(st == n_steps)
    def _():
        new = new_ref[0]
        k_new = new[:, 2 * KV_WIDTH:3 * KV_WIDTH].astype(BF16)
        v_new = new[:, 3 * KV_WIDTH:4 * KV_WIDTH].astype(BF16)
        s = lax.dot_general(qbd_ref[...], k_new, NT_DIMS, preferred_element_type=F32)
        blk = n_steps * PAGES_PER_STEP * BLOCKS_PER_PAGE
        chosen = jnp.broadcast_to(selr_ref[pl.ds(blk, 1), :], (tq, rows)).T
        causal = (lax.broadcasted_iota(I32, (rows, tq), 1)
                  <= (lax.broadcasted_iota(I32, (rows, tq), 0) & (tq - 1)))
        flash(s, (chosen > 0.5) & causal, v_new, False)
        o = acc_ref[...] / l_ref[...]
        outs = []
        for k in range(N_KV_HEADS):
            for g in range(GQA):
                h = k * GQA + g
                outs.append(o[h * tq:(h + 1) * tq, k * HEAD_DIM:(k + 1) * HEAD_DIM].T)
        o_ref[0] = jnp.concatenate(outs, axis=0)


def _selattn_paged(proj3, cache5, layer, page_table, sel):
    b, tq, _ = proj3.shape
    _, n_pages = page_table.shape
    assert tq & (tq - 1) == 0 and tq <= BLOCK and n_pages % PAGES_PER_STEP == 0
    n_steps = n_pages // PAGES_PER_STEP
    nbp = sel.shape[1]
    rows = N_HEADS * tq
    kern = functools.partial(_selattn_paged_kernel, tq=tq, n_steps=n_steps)
    return pl.pallas_call(
        kern,
        grid_spec=pltpu.PrefetchScalarGridSpec(
            num_scalar_prefetch=1, grid=(b, n_steps + 1),
            in_specs=_page_specs(layer, n_pages, 1) + [
                pl.BlockSpec((1, tq, ATT_WIDTH), lambda bi, st, pt: (bi, 0, C_Q // ATT_WIDTH)),
                pl.BlockSpec((1, tq, 4 * KV_WIDTH), lambda bi, st, pt: (bi, 0, C_KV // (4 * KV_WIDTH))),
                pl.BlockSpec((1, nbp, N_KV_HEADS * tq), lambda bi, st, pt: (bi, 0, 0))],
            out_specs=pl.BlockSpec((1, ATT_WIDTH, tq), lambda bi, st, pt: (bi, 0, 0)),
            scratch_shapes=[pltpu.VMEM((rows, KV_WIDTH), BF16), pltpu.VMEM((nbp, rows), F32),
                            pltpu.VMEM((rows, 1), F32), pltpu.VMEM((rows, 1), F32),
                            pltpu.VMEM((rows, KV_WIDTH), F32)]),
        out_shape=jax.ShapeDtypeStruct((b, ATT_WIDTH, tq), F32),
        compiler_params=_params("parallel", "arbitrary"),
        name="selattn_paged",
    )(page_table, *([cache5] * PAGES_PER_STEP), proj3, proj3, sel)


def _tcols_kernel(x_ref, o_ref):
    o_ref[0] = x_ref[0].T


def _tcols(proj3, col):
    b, s, _ = proj3.shape
    ts = min(s, 512)
    return pl.pallas_call(
        _tcols_kernel,
        grid=(b, s // ts),
        in_specs=[pl.BlockSpec((1, ts, KV_WIDTH), lambda i, t: (i, t, col))],
        out_specs=pl.BlockSpec((1, KV_WIDTH, ts), lambda i, t: (i, 0, t)),
        out_shape=jax.ShapeDtypeStruct((b, KV_WIDTH, s), F32),
        compiler_params=_params("parallel", "parallel"),
        name="tcols",
    )(proj3)


def _attcomb_kernel(oc_ref, os_ref, ow_ref, ag_ref, az_ref, o_ref):
    gate_t = jax.nn.sigmoid(ag_ref[0]).T
    oc, osel, ow = oc_ref[0], os_ref[0], ow_ref[0]
    outs = []
    for h in range(N_HEADS):
        sl = slice(h * HEAD_DIM, (h + 1) * HEAD_DIM)
        c = h * N_ATT_GATES
        outs.append(gate_t[c:c + 1, :] * oc[sl, :] + gate_t[c + 1:c + 2, :] * osel[sl, :]
                    + gate_t[c + 2:c + 3, :] * ow[sl, :])
    o_ref[0] = (jnp.concatenate(outs, axis=0).T * _silu(az_ref[0])).astype(BF16)


def _attcomb(ocmp_t, osel_t, owin_t, gates3, proj3):
    b, s, _ = proj3.shape
    tt = min(s, 256)
    att_t = pl.BlockSpec((1, ATT_WIDTH, tt), lambda i, t: (i, 0, t))
    return pl.pallas_call(
        _attcomb_kernel,
        grid=(b, s // tt),
        in_specs=[att_t, att_t, att_t,
                  pl.BlockSpec((1, tt, N_GATE_PAD), lambda i, t: (i, t, 0)),
                  pl.BlockSpec((1, tt, ATT_WIDTH), lambda i, t: (i, t, C_AZ // ATT_WIDTH))],
        out_specs=pl.BlockSpec((1, tt, ATT_WIDTH), lambda i, t: (i, t, 0)),
        out_shape=jax.ShapeDtypeStruct((b, s, ATT_WIDTH), BF16),
        compiler_params=_params("parallel", "parallel"),
        name="attcomb",
    )(ocmp_t, osel_t, owin_t, gates3, proj3)


def _ssm_disc_kernel(lr_ref, li_ref, ls_ref, brt_ref, bit_ref, pr_ref, pi_ref, bbr_ref, bbi_ref):
    lr, li = lr_ref[...], li_ref[...]
    dt = jnp.exp(ls_ref[...])
    mag = jnp.exp(lr * dt)
    ab_re, ab_im = mag * jnp.cos(li * dt), mag * jnp.sin(li * dt)
    den = lr * lr + li * li
    co_re = ((ab_re - 1.0) * lr + ab_im * li) / den
    co_im = (ab_im * lr - (ab_re - 1.0) * li) / den
    brt, bit = brt_ref[...], bit_ref[...]
    bbr_ref[...] = co_re[:, None, :] * brt - co_im[:, None, :] * bit
    bbi_ref[...] = co_re[:, None, :] * bit + co_im[:, None, :] * brt
    pr, pi = ab_re, ab_im
    pr_ref[0], pi_ref[0] = pr, pi
    for r in range(1, SUBLANES):
        pr, pi = pr * ab_re - pi * ab_im, pr * ab_im + pi * ab_re
        pr_ref[r], pi_ref[r] = pr, pi


def _ssm_disc(lam_re, lam_im, log_step, b_re, b_im):
    g, n = lam_re.shape
    brt = jnp.swapaxes(b_re, 1, 2)
    bit = jnp.swapaxes(b_im, 1, 2)
    pw = jax.ShapeDtypeStruct((SUBLANES, g, n), F32)
    bb = jax.ShapeDtypeStruct(brt.shape, F32)
    return pl.pallas_call(_ssm_disc_kernel, out_shape=[pw, pw, bb, bb], name="ssm_disc")(
        lam_re, lam_im, log_step.reshape(g, 1), brt, bit)


def _ssm_kernel(su_ref, sz_ref, h0_ref, tab_ref, bmr_ref, bmi_ref, cm_ref, ds_ref, wg_ref,
                o_ref, st_ref, hr_ref, hi_ref, c_ref, *, tt, row_last):
    t = pl.program_id(1)

    @pl.when(t == 0)
    def _():
        c_ref[...] = h0_ref[0]

    u = su_ref[0]
    ub = u.astype(BF16)
    n_mm = SSM_WIDTH // 128
    for j in range(n_mm):
        uj = ub[:, j * 128:(j + 1) * 128]
        hr_ref[:, j * 512:(j + 1) * 512] = jnp.dot(uj, bmr_ref[j], preferred_element_type=F32)
        hi_ref[:, j * 512:(j + 1) * 512] = jnp.dot(uj, bmi_ref[j], preferred_element_type=F32)

    for cc in range(SSM_CH // SSM_LANE_CHUNK):
        lanes = slice(cc * SSM_LANE_CHUNK, (cc + 1) * SSM_LANE_CHUNK)
        tabs = [(tab_ref[2 * i, :, lanes], tab_ref[2 * i + 1, :, lanes]) for i in range(4)]

        def group(gi, carry):
            cr, ci = carry
            rows = pl.ds(pl.multiple_of(gi * SUBLANES, SUBLANES), SUBLANES)
            xr, xi = hr_ref[rows, lanes], hi_ref[rows, lanes]
            for lvl, shift in enumerate((1, 2, 4)):
                ar, ai = tabs[lvl]
                sr, si = pltpu.roll(xr, shift, axis=0), pltpu.roll(xi, shift, axis=0)
                xr, xi = xr + (ar * sr - ai * si), xi + (ar * si + ai * sr)
            pr, pi = tabs[3]
            xr, xi = xr + (pr * cr - pi * ci), xi + (pr * ci + pi * cr)
            hr_ref[rows, lanes], hi_ref[rows, lanes] = xr, xi
            return xr[SUBLANES - 1:SUBLANES, :], xi[SUBLANES - 1:SUBLANES, :]

        cr, ci = lax.fori_loop(0, tt // SUBLANES, group, (c_ref[0:1, lanes], c_ref[1:2, lanes]))
        c_ref[0:1, lanes], c_ref[1:2, lanes] = cr, ci

    @pl.when(t == pl.num_programs(1) - 1)
    def _():
        st_ref[0, 0:1, :] = hr_ref[row_last:row_last + 1, :]
        st_ref[0, 1:2, :] = hi_ref[row_last:row_last + 1, :]

    ys = []
    for j in range(n_mm):
        hcat = jnp.concatenate([hr_ref[:, j * 512:(j + 1) * 512], hi_ref[:, j * 512:(j + 1) * 512]], axis=-1)
        ys.append(jnp.dot(hcat.astype(BF16), cm_ref[j], preferred_element_type=F32))
    y = jnp.concatenate(ys, axis=-1) + ds_ref[...] * u
    z = jax.nn.gelu(y)
    out = z * jax.nn.sigmoid(jnp.dot(z.astype(BF16), wg_ref[...], preferred_element_type=F32))
    o_ref[0] = (out * _silu(sz_ref[0])).astype(BF16)


def _ssm(proj3, h0, tab, bmr, bmi, cm, d_skip, w_glu, *, s_valid):
    b, s, _ = proj3.shape
    tt = min(s, 256)
    row_last = (s_valid - 1) % tt
    n_mm = SSM_WIDTH // 128
    kern = functools.partial(_ssm_kernel, tt=tt, row_last=row_last)
    const3 = lambda i, t: (0, 0, 0)
    return pl.pallas_call(
        kern,
        grid=(b, s // tt),
        in_specs=[pl.BlockSpec((1, tt, SSM_WIDTH), lambda i, t: (i, t, C_SU // SSM_WIDTH)),
                  pl.BlockSpec((1, tt, SSM_WIDTH), lambda i, t: (i, t, C_SZ // SSM_WIDTH)),
                  pl.BlockSpec((1, 2, SSM_CH), lambda i, t: (i, 0, 0)),
                  pl.BlockSpec((8, SUBLANES, SSM_CH), const3),
                  pl.BlockSpec((n_mm, 128, 512), const3),
                  pl.BlockSpec((n_mm, 128, 512), const3),
                  pl.BlockSpec((n_mm, 1024, 128), const3),
                  pl.BlockSpec((1, SSM_WIDTH), lambda i, t: (0, 0)),
                  pl.BlockSpec((SSM_WIDTH, SSM_WIDTH), lambda i, t: (0, 0))],
        out_specs=[pl.BlockSpec((1, tt, SSM_WIDTH), lambda i, t: (i, t, 0)),
                   pl.BlockSpec((1, 2, SSM_CH), lambda i, t: (i, 0, 0))],
        out_shape=[jax.ShapeDtypeStruct((b, s, SSM_WIDTH), BF16),
                   jax.ShapeDtypeStruct((b, 2, SSM_CH), F32)],
        scratch_shapes=[pltpu.VMEM((tt, SSM_CH), F32), pltpu.VMEM((tt, SSM_CH), F32),
                        pltpu.VMEM((2, SSM_CH), F32)],
        compiler_params=_params("parallel", "arbitrary"),
        name="ssm",
    )(proj3, proj3, h0, tab, bmr, bmi, cm, d_skip, w_glu)


def _ssm_tables(lam_re, lam_im, log_step, b_re, b_im, c_re, c_im):
    pw_re, pw_im, bbr, bbi = _ssm_disc(lam_re, lam_im, log_step, b_re, b_im)
    pw_re = pw_re.reshape(SUBLANES, SSM_CH)
    pw_im = pw_im.reshape(SUBLANES, SSM_CH)
    row = jnp.arange(SUBLANES)[:, None]
    tabs = []
    for shift in (1, 2, 4):
        keep = row >= shift
        tabs += [jnp.where(keep, pw_re[shift - 1][None], 0.0), jnp.where(keep, pw_im[shift - 1][None], 0.0)]
    tab = jnp.stack(tabs + [pw_re, pw_im])
    eye = jnp.eye(8, dtype=F32)

    def b_blocks(bb):
        x = bb.reshape(SSM_GROUPS // 8, 8, SSM_GROUP_DIM, SSM_STATE)
        return jnp.einsum("jgcn,gh->jgchn", x, eye).reshape(SSM_GROUPS // 8, 128, 512).astype(BF16)

    def c_blocks(c):
        x = c.reshape(SSM_GROUPS // 8, 8, SSM_GROUP_DIM, SSM_STATE)
        return jnp.einsum("jgcn,gh->jgnhc", x, eye).reshape(SSM_GROUPS // 8, 512, 128)

    cm = jnp.concatenate([c_blocks(c_re), -c_blocks(c_im)], axis=1).astype(BF16)
    return tab, b_blocks(bbr), b_blocks(bbi), cm


def _merge_kernel(ap_ref, aa_ref, as_ref, wp_ref, wa_ref, ws_ref, g0_ref, g1_ref, g2_ref, o_ref):
    bp = jnp.dot(ap_ref[...], wp_ref[...], preferred_element_type=F32)
    ba = jnp.dot(aa_ref[...], wa_ref[...], preferred_element_type=F32)
    bs = jnp.dot(as_ref[...], ws_ref[...], preferred_element_type=F32)
    o_ref[...] = (jax.nn.sigmoid(g0_ref[...]) * bp + jax.nn.sigmoid(g1_ref[...]) * ba
                  + jax.nn.sigmoid(g2_ref[...]) * bs).astype(BF16)


def _merge(a_pool, a_att, a_ssm, w_p, w_a, w_s, proj2):
    m = a_pool.shape[0]
    tm = min(m, 512)
    tn = 1024
    act = pl.BlockSpec((tm, 1024), lambda i, n: (i, 0))
    wsp = pl.BlockSpec((1024, tn), lambda i, n: (0, n))
    gate = lambda r: pl.BlockSpec((tm, tn), lambda i, n: (i, (C_MG + r * D_MODEL) // tn + n))
    return pl.pallas_call(
        _merge_kernel,
        grid=(m // tm, D_MODEL // tn),
        in_specs=[act, act, act, wsp, wsp, wsp, gate(0), gate(1), gate(2)],
        out_specs=pl.BlockSpec((tm, tn), lambda i, n: (i, n)),
        out_shape=jax.ShapeDtypeStruct((m, D_MODEL), BF16),
        compiler_params=_params("parallel", "arbitrary"),
        name="merge",
    )(a_pool, a_att, a_ssm, w_p, w_a, w_s, proj2, proj2, proj2)


def _outproj_kernel(m_ref, w_ref, g_ref, x_ref, o_ref):
    y = jnp.dot(m_ref[...], w_ref[...], preferred_element_type=F32)
    r = lax.rsqrt(jnp.mean(y * y, axis=-1, keepdims=True) + EPS)
    o_ref[...] = x_ref[...] + y * r * g_ref[...]


def _outproj(merged, w_out, g_post, x2d):
    m = merged.shape[0]
    tm = min(m, 512)
    row = pl.BlockSpec((tm, D_MODEL), lambda i: (i, 0))
    return pl.pallas_call(
        _outproj_kernel,
        grid=(m // tm,),
        in_specs=[row, pl.BlockSpec((D_MODEL, D_MODEL), lambda i: (0, 0)),
                  pl.BlockSpec((1, D_MODEL), lambda i: (0, 0)), row],
        out_specs=row,
        out_shape=jax.ShapeDtypeStruct((m, D_MODEL), F32),
        compiler_params=_params("parallel"),
        name="outproj",
    )(merged, w_out, g_post, x2d)


def _prep_layer_weights(l, g_pre, g_post, w_in, w_pool, pool_scale, pe_cmp, w_phi, lam_re, lam_im, log_step,
                        b_re, b_im, c_re, c_im, d_skip, w_glu, w_br_pool, w_br_nsa, w_br_ssm, w_out):
    w = w_in[l]
    kv0 = 2 * POOL_WIDTH + ATT_WIDTH
    ag0 = kv0 + 6 * KV_WIDTH
    az0 = ag0 + N_HEADS * N_ATT_GATES
    w_main = jnp.concatenate([w[:, :kv0], w[:, az0:], w[:, kv0:ag0]], axis=1).astype(BF16)
    w_gate = jnp.pad(w[:, ag0:az0], ((0, 0), (0, N_GATE_PAD - N_HEADS * N_ATT_GATES))).astype(BF16)
    eye = jnp.eye(N_KV_HEADS, dtype=F32)
    w_cmp = [jnp.einsum("lde,kh->lkdhe", w_phi[l, c], eye).reshape(BLOCK, KV_WIDTH, KV_WIDTH).astype(BF16)
             for c in range(2)]
    pe = [jnp.broadcast_to(pe_cmp[l, c][:, None, :], (BLOCK, N_KV_HEADS, HEAD_DIM)).reshape(BLOCK, KV_WIDTH)
          for c in range(2)]
    tab, bmr, bmi, cm = _ssm_tables(lam_re[l], lam_im[l], log_step[l], b_re[l], b_im[l], c_re[l], c_im[l])
    return dict(
        g_pre=g_pre[l].reshape(1, D_MODEL), g_post=g_post[l].reshape(1, D_MODEL),
        w_main=w_main, w_gate=w_gate, w_pool=w_pool[l].astype(BF16), pool_scale=pool_scale[l].reshape(1, POOL_WIDTH),
        w_cmp=w_cmp, pe=pe, tab=tab, bmr=bmr, bmi=bmi, cm=cm, d_skip=d_skip[l].reshape(1, SSM_WIDTH),
        w_glu=w_glu[l].astype(BF16), w_br_pool=w_br_pool[l].astype(BF16), w_br_nsa=w_br_nsa[l].astype(BF16),
        w_br_ssm=w_br_ssm[l].astype(BF16), w_out=w_out[l].astype(BF16))


def _layer(x3, lw, *, pos0, s_valid, pool_prefix, ssm_h0, win_prefix, paged):
    b, s, _ = x3.shape
    x2 = x3.reshape(b * s, D_MODEL)
    proj2 = _inproj(x2, lw["g_pre"], lw["w_main"], 768)
    gates3 = _inproj(x2, lw["g_pre"], lw["w_gate"], N_GATE_PAD).reshape(b, s, N_GATE_PAD)
    proj3 = proj2.reshape(b, s, N_MAIN)

    prefix16 = jnp.pad(pool_prefix, ((0, 0), (1, 0), (0, 0)))
    a_pool, pool_tail = _pool(proj3, prefix16, lw["w_pool"], lw["pool_scale"], q0=pos0, s_valid=s_valid)
    pool_state = pool_tail[:, 1:]

    kvn = proj3[:, :, C_KV:]
    if paged is None:
        tq = min(s, 256)
        n_cmp = nb_total = nbp = s // BLOCK
        x3k = x3v = proj3.reshape(b * n_cmp, BLOCK, N_MAIN)
        kcol, vcol = KV_COL, KV_COL + 1
    else:
        tq = s
        cache5, layer, page_table = paged
        past_len = page_table.shape[1] * PAGE_SIZE
        n_cmp = past_len // BLOCK
        nb_total = -(-(past_len + s_valid) // BLOCK)
        nbp = -(-nb_total // SUBLANES) * SUBLANES
        x3k, x3v = _gather_cmp(cache5, layer, page_table)
        kcol = vcol = 0
    kc = _compress(x3k, kcol, lw["pe"][0], lw["w_cmp"][0]).reshape(b, n_cmp, KV_WIDTH)
    vc = _compress(x3v, vcol, lw["pe"][1], lw["w_cmp"][1]).reshape(b, n_cmp, KV_WIDTH)
    kc = jnp.pad(kc, ((0, 0), (0, nbp - n_cmp), (0, 0)))
    vct = jnp.swapaxes(jnp.pad(vc, ((0, 0), (0, nbp - n_cmp), (0, 0))), 1, 2)
    o_cmp, sel = _cmpsel(proj3, kc, vct, pos0=pos0, tq=tq, nb_total=nb_total)
    if paged is None:
        o_sel = _selattn(proj3, _tcols(proj3, KV_COL + 3), sel, pos0=pos0, tq=tq)
    else:
        o_sel = _selattn_paged(proj3, cache5, layer, page_table, sel)

    tq_win = min(s, 256)
    n_rows = ((s - tq_win) // KEY_TILE + -(-(WINDOW + tq_win) // KEY_TILE)) * KEY_TILE
    pad_rows = n_rows - WINDOW - s
    k_new, v_new = kvn[:, :, 4 * KV_WIDTH:5 * KV_WIDTH], kvn[:, :, 5 * KV_WIDTH:6 * KV_WIDTH]
    v_new_t = _tcols(proj3, KV_COL + 5)
    if win_prefix is None:
        kwin = jnp.pad(k_new, ((0, 0), (WINDOW, pad_rows), (0, 0)))
        vwin_t = jnp.pad(v_new_t, ((0, 0), (0, 0), (WINDOW, pad_rows)))
    else:
        k_pre = win_prefix[:, :, 0].reshape(b, WINDOW, KV_WIDTH)
        v_pre = win_prefix[:, :, 1].reshape(b, WINDOW, KV_WIDTH)
        kwin = jnp.pad(jnp.concatenate([k_pre, k_new], axis=1), ((0, 0), (0, pad_rows), (0, 0)))
        vwin_t = jnp.pad(jnp.concatenate([jnp.swapaxes(v_pre, 1, 2), v_new_t], axis=2),
                         ((0, 0), (0, 0), (0, pad_rows)))
    o_win = _winattn(proj3, kwin, vwin_t, pos0=pos0, row_pos0=pos0 - WINDOW, tq=tq_win)
    a_att = _attcomb(o_cmp, o_sel, o_win, gates3, proj3)

    a_ssm, ssm_state = _ssm(proj3, ssm_h0.reshape(b, 2, SSM_CH), lw["tab"], lw["bmr"], lw["bmi"], lw["cm"],
                            lw["d_skip"], lw["w_glu"], s_valid=s_valid)

    merged = _merge(a_pool.reshape(b * s, POOL_WIDTH), a_att.reshape(b * s, ATT_WIDTH),
                    a_ssm.reshape(b * s, SSM_WIDTH), lw["w_br_pool"], lw["w_br_nsa"], lw["w_br_ssm"], proj2)
    x_new = _outproj(merged, lw["w_out"], lw["g_post"], x2).reshape(b, s, D_MODEL)

    kv_rows = kvn[:, :s_valid, :4 * KV_WIDTH].reshape(b, s_valid, 4, N_KV_HEADS, HEAD_DIM)
    if win_prefix is None:
        assert s_valid >= WINDOW
        win_k, win_v = k_new[:, s_valid - WINDOW:s_valid], v_new[:, s_valid - WINDOW:s_valid]
    else:
        win_k = jnp.concatenate([k_pre, k_new], axis=1)[:, s_valid:s_valid + WINDOW]
        win_v = jnp.concatenate([v_pre, v_new], axis=1)[:, s_valid:s_valid + WINDOW]
    win_state = jnp.stack([win_k.reshape(b, WINDOW, N_KV_HEADS, HEAD_DIM),
                           win_v.reshape(b, WINDOW, N_KV_HEADS, HEAD_DIM)], axis=2)
    return x_new, kv_rows, win_state, pool_state, ssm_state.reshape(b, 2, SSM_GROUPS, SSM_STATE)


def kernel(x_prompt, x_sample, cache_kv, page_table, state_win_kv, state_pool, state_ssm, g_pre, g_post, w_in, w_pool, pool_scale, pe_cmp, w_phi, lam_re, lam_im, log_step, b_re, b_im, c_re, c_im, d_skip, w_glu, w_br_pool, w_br_nsa, w_br_ssm, w_out):
    depth = w_in.shape[0]
    bp, sp, _ = x_prompt.shape
    bd, sd, _ = x_sample.shape
    n_pool = cache_kv.shape[1]
    past_len = page_table.shape[1] * PAGE_SIZE
    assert state_win_kv.shape[2] == WINDOW and past_len >= WINDOW and sp >= WINDOW
    cache5 = jnp.transpose(cache_kv, (0, 1, 3, 4, 5, 2)).reshape(depth, n_pool, 4, KV_WIDTH, PAGE_SIZE)
    sd_pad = -(-sd // SUBLANES) * SUBLANES
    yp = x_prompt
    ys = jnp.pad(x_sample, ((0, 0), (0, sd_pad - sd), (0, 0)))
    zeros_pool = jnp.zeros((bp, POOL_STATE, POOL_WIDTH), F32)
    zeros_ssm = jnp.zeros((bp, 2, SSM_GROUPS, SSM_STATE), F32)
    outs_p, outs_s = [], []
    for l in range(depth):
        lw = _prep_layer_weights(l, g_pre, g_post, w_in, w_pool, pool_scale, pe_cmp, w_phi, lam_re, lam_im, log_step,
                                 b_re, b_im, c_re, c_im, d_skip, w_glu, w_br_pool, w_br_nsa, w_br_ssm, w_out)
        yp, *rp = _layer(yp, lw, pos0=0, s_valid=sp, pool_prefix=zeros_pool, ssm_h0=zeros_ssm,
                         win_prefix=None, paged=None)
        ys, *rs = _layer(ys, lw, pos0=past_len, s_valid=sd, pool_prefix=state_pool[l], ssm_h0=state_ssm[l],
                         win_prefix=state_win_kv[l], paged=(cache5, l, page_table))
        outs_p.append(rp)
        outs_s.append(rs)
    stack = lambda outs, i: jnp.stack([o[i] for o in outs])
    return (yp, ys[:, :sd], stack(outs_p, 0), stack(outs_s, 0), stack(outs_p, 1), stack(outs_s, 1),
            stack(outs_p, 2), stack(outs_s, 2), stack(outs_p, 3), stack(outs_s, 3))
```

```python
import functools

import jax
import jax.numpy as jnp
from jax import lax
from jax.experimental import pallas as pl
from jax.experimental.pallas import tpu as pltpu

F32 = jnp.float32
BF16 = jnp.bfloat16
I32 = jnp.int32

D_MODEL = 2048
PAGE_SIZE = 128
POOL_WIDTH = D_MODEL // 2
POOL_WINDOWS = (2, 4, 8, 16)
POOL_GROUP_DIM = POOL_WIDTH // len(POOL_WINDOWS)
POOL_STATE = max(POOL_WINDOWS) - 1
N_HEADS = 16
HEAD_DIM = 64
N_KV_HEADS = 4
GQA = N_HEADS // N_KV_HEADS
ATT_WIDTH = N_HEADS * HEAD_DIM
KV_WIDTH = N_KV_HEADS * HEAD_DIM
BLOCK = 64
TOP_N = 16
WINDOW = 512
N_ATT_GATES = 3
SSM_WIDTH = D_MODEL // 2
SSM_GROUP_DIM = 16
SSM_GROUPS = SSM_WIDTH // SSM_GROUP_DIM
SSM_STATE = 64
SSM_CH = SSM_GROUPS * SSM_STATE
N_BRANCH = 3
EPS = 1e-6
NEG = -1e30
FORCE = 1e4
ATT_SCALE = HEAD_DIM ** -0.5

C_PU, C_PZ, C_Q, C_AZ, C_SU, C_SZ = (i * 1024 for i in range(6))
C_MG = 6 * 1024
C_KV = C_MG + N_BRANCH * D_MODEL
N_MAIN = C_KV + 6 * KV_WIDTH
N_GATE_PAD = 128
KV_COL = C_KV // KV_WIDTH

VMEM_LIMIT_BYTES = 52 * 1024 * 1024
SUBLANES = 8
KEY_TILE = 256
BLOCKS_PER_TILE = KEY_TILE // BLOCK
BLOCKS_PER_PAGE = PAGE_SIZE // BLOCK
PAGES_PER_STEP = 8
SSM_LANE_CHUNK = 1024

NT_DIMS = (((1,), (1,)), ((), ()))


def _params(*sem):
    return pltpu.CompilerParams(dimension_semantics=sem, vmem_limit_bytes=VMEM_LIMIT_BYTES)


def _silu(x):
    return x * jax.nn.sigmoid(x)


def _inproj_kernel(x_ref, g_ref, w_ref, o_ref, h_ref):
    @pl.when(pl.program_id(1) == 0)
    def _():
        x = x_ref[...]
        r = lax.rsqrt(jnp.mean(x * x, axis=-1, keepdims=True) + EPS)
        h_ref[...] = (x * r * g_ref[...]).astype(BF16)

    o_ref[...] = jnp.dot(h_ref[...], w_ref[...], preferred_element_type=F32)


def _inproj(x2d, g, w, tn):
    m, d = x2d.shape
    n = w.shape[1]
    tm = min(m, 1024)
    return pl.pallas_call(
        _inproj_kernel,
        grid=(m // tm, n // tn),
        in_specs=[pl.BlockSpec((tm, d), lambda i, j: (i, 0)),
                  pl.BlockSpec((1, d), lambda i, j: (0, 0)),
                  pl.BlockSpec((d, tn), lambda i, j: (0, j))],
        out_specs=pl.BlockSpec((tm, tn), lambda i, j: (i, j)),
        out_shape=jax.ShapeDtypeStruct((m, n), F32),
        scratch_shapes=[pltpu.VMEM((tm, d), BF16)],
        compiler_params=_params("parallel", "arbitrary"),
        name="inproj",
    )(x2d, g, w)


def _pool_kernel(pu_ref, pz_ref, pre_ref, wp_ref, sc_ref, o_ref, st_ref, e_ref, *, tt, q0, rows_last):
    t = pl.program_id(1)
    hist = POOL_STATE + 1

    @pl.when(t == 0)
    def _():
        e_ref[0:hist, :] = pre_ref[0]

    @pl.when(t > 0)
    def _():
        e_ref[0:hist, :] = e_ref[tt:tt + hist, :]

    u = pu_ref[0]
    e_ref[hist:hist + tt, :] = u
    pos = q0 + t * tt + lax.broadcasted_iota(I32, (tt, 1), 0)
    ys = []
    for gi, w in enumerate(POOL_WINDOWS):
        lo = gi * POOL_GROUP_DIM
        tot = e_ref[hist:hist + tt, lo:lo + POOL_GROUP_DIM]
        for k in range(1, w):
            tot = tot + e_ref[hist - k:hist - k + tt, lo:lo + POOL_GROUP_DIM]
        cnt = jnp.minimum(pos + 1, w).astype(F32)
        diff = tot / cnt - u[:, lo:lo + POOL_GROUP_DIM]
        ys.append(jnp.dot(diff.astype(BF16), wp_ref[gi], preferred_element_type=F32))
    y = jnp.concatenate(ys, axis=-1) * sc_ref[...]
    o_ref[0] = (y * _silu(pz_ref[0])).astype(BF16)

    @pl.when(t == pl.num_programs(1) - 1)
    def _():
        st_ref[0] = e_ref[rows_last:rows_last + hist, :]


def _pool(proj3, prefix16, w_pool, pool_scale, *, q0, s_valid):
    b, s, _ = proj3.shape
    tt = min(s, 512)
    hist = POOL_STATE + 1
    rows_last = ((s_valid - 1) % tt) + 1
    kern = functools.partial(_pool_kernel, tt=tt, q0=q0, rows_last=rows_last)
    return pl.pallas_call(
        kern,
        grid=(b, s // tt),
        in_specs=[pl.BlockSpec((1, tt, POOL_WIDTH), lambda i, t: (i, t, C_PU // POOL_WIDTH)),
                  pl.BlockSpec((1, tt, POOL_WIDTH), lambda i, t: (i, t, C_PZ // POOL_WIDTH)),
                  pl.BlockSpec((1, hist, POOL_WIDTH), lambda i, t: (i, 0, 0)),
                  pl.BlockSpec((len(POOL_WINDOWS), POOL_GROUP_DIM, POOL_GROUP_DIM), lambda i, t: (0, 0, 0)),
                  pl.BlockSpec((1, POOL_WIDTH), lambda i, t: (0, 0))],
        out_specs=[pl.BlockSpec((1, tt, POOL_WIDTH), lambda i, t: (i, t, 0)),
                   pl.BlockSpec((1, hist, POOL_WIDTH), lambda i, t: (i, 0, 0))],
        out_shape=[jax.ShapeDtypeStruct((b, s, POOL_WIDTH), BF16),
                   jax.ShapeDtypeStruct((b, hist, POOL_WIDTH), F32)],
        scratch_shapes=[pltpu.VMEM((hist + tt, POOL_WIDTH), F32)],
        compiler_params=_params("parallel", "arbitrary"),
        name="pool",
    )(proj3, proj3, prefix16, w_pool, pool_scale)


def _compress_kernel(x_ref, pe_ref, w_ref, o_ref):
    def one_row(l, acc):
        x = x_ref[:, l, :] + pe_ref[pl.ds(l, 1), :]
        return acc + jnp.dot(x.astype(BF16), w_ref[l], preferred_element_type=F32)

    o_ref[...] = lax.fori_loop(0, BLOCK, one_row, jnp.zeros(o_ref.shape, F32))


def _compress(x3, col, pe, w):
    nb = x3.shape[0]
    tm = min(nb, 128)
    return pl.pallas_call(
        _compress_kernel,
        grid=(nb // tm,),
        in_specs=[pl.BlockSpec((tm, BLOCK, KV_WIDTH), lambda i: (i, 0, col)),
                  pl.BlockSpec((BLOCK, KV_WIDTH), lambda i: (0, 0)),
                  pl.BlockSpec((BLOCK, KV_WIDTH, KV_WIDTH), lambda i: (0, 0, 0))],
        out_specs=pl.BlockSpec((tm, KV_WIDTH), lambda i: (i, 0)),
        out_shape=jax.ShapeDtypeStruct((nb, KV_WIDTH), F32),
        compiler_params=_params("parallel"),
        name="compress",
    )(x3, pe, w)


def _compress_rows_kernel(x_ref, w_ref, o_ref):
    def one_row(l, acc):
        return acc + jnp.dot(x_ref[l], w_ref[l], preferred_element_type=F32)

    o_ref[...] = lax.fori_loop(0, BLOCK, one_row, jnp.zeros(o_ref.shape, F32))


def _compress_rows(xl, w):
    nb = xl.shape[1]
    tm = min(nb, 256)
    return pl.pallas_call(
        _compress_rows_kernel,
        grid=(nb // tm,),
        in_specs=[pl.BlockSpec((BLOCK, tm, KV_WIDTH), lambda i: (0, i, 0)),
                  pl.BlockSpec((BLOCK, KV_WIDTH, KV_WIDTH), lambda i: (0, 0, 0))],
        out_specs=pl.BlockSpec((tm, KV_WIDTH), lambda i: (i, 0)),
        out_shape=jax.ShapeDtypeStruct((nb, KV_WIDTH), F32),
        compiler_params=_params("parallel"),
        name="compress_rows",
    )(xl, w)


def _page_specs(layer, n_pages, comp_block):
    def spec(i):
        def index(bi, st, pt):
            return (layer, pt[bi, jnp.minimum(st * PAGES_PER_STEP + i, n_pages - 1)], comp_block, 0, 0)
        return pl.BlockSpec((1, 1, 2, KV_WIDTH, PAGE_SIZE), index)
    return [spec(i) for i in range(PAGES_PER_STEP)]


def _sublane_transpose8(a):
    sub = lax.broadcasted_iota(I32, a[0].shape, 0)
    for shift in (4, 2, 1):
        low = (sub & shift) == 0
        nxt = list(a)
        for j in range(SUBLANES):
            if j & shift == 0:
                nxt[j] = jnp.where(low, a[j], pltpu.roll(a[j + shift], shift, axis=0))
                nxt[j + shift] = jnp.where(low, pltpu.roll(a[j], SUBLANES - shift, axis=0), a[j + shift])
        a = nxt
    return a


def _gather_cmp_kernel(pt_ref, *refs):
    del pt_ref
    pages, (pe_ref, xk_ref, xv_ref) = refs[:PAGES_PER_STEP], refs[PAGES_PER_STEP:]
    n_groups = PAGES_PER_STEP * BLOCKS_PER_PAGE // SUBLANES
    for comp, dst in enumerate((xk_ref, xv_ref)):
        rows = [page[0, 0, comp].T + pe_ref[comp] for page in pages]
        for m in range(BLOCK // SUBLANES):
            per_group = []
            for grp in range(n_groups):
                pieces = []
                for jj in range(SUBLANES):
                    j = grp * SUBLANES + jj
                    lo = (j % BLOCKS_PER_PAGE) * BLOCK + m * SUBLANES
                    pieces.append(rows[j // BLOCKS_PER_PAGE][lo:lo + SUBLANES, :])
                per_group.append(_sublane_transpose8(pieces))
            for s in range(SUBLANES):
                dst[m * SUBLANES + s] = jnp.concatenate([g[s] for g in per_group], axis=0).astype(BF16)


def _gather_cmp(cache5, layer, page_table, pe2):
    b, n_pages = page_table.shape
    n_steps = n_pages // PAGES_PER_STEP
    blocks_per_step = PAGES_PER_STEP * BLOCKS_PER_PAGE
    assert blocks_per_step % (2 * SUBLANES) == 0
    out_spec = pl.BlockSpec((BLOCK, blocks_per_step, KV_WIDTH), lambda bi, st, pt: (0, bi * n_steps + st, 0))
    out = jax.ShapeDtypeStruct((BLOCK, b * n_pages * BLOCKS_PER_PAGE, KV_WIDTH), BF16)
    return pl.pallas_call(
        _gather_cmp_kernel,
        grid_spec=pltpu.PrefetchScalarGridSpec(
            num_scalar_prefetch=1, grid=(b, n_steps),
            in_specs=_page_specs(layer, n_pages, 0) + [
                pl.BlockSpec((2, PAGE_SIZE, KV_WIDTH), lambda bi, st, pt: (0, 0, 0))],
            out_specs=[out_spec, out_spec]),
        out_shape=[out, out],
        compiler_params=_params("parallel", "arbitrary"),
        name="gather_cmp",
    )(page_table, *([cache5] * PAGES_PER_STEP), pe2)


def _cmpsel_kernel(q_ref, kc_ref, vct_ref, ocmp_ref, sel_ref, s_ref, *, tq, nbp, pos0, n_sel):
    j = pl.program_id(1)
    qs = (q_ref[0] * ATT_SCALE).astype(BF16)
    kc = kc_ref[0].astype(BF16)
    vct = vct_ref[0].astype(BF16)
    nidx = lax.broadcasted_iota(I32, (nbp, tq), 0)
    pos = pos0 + j * tq + lax.broadcasted_iota(I32, (nbp, tq), 1)
    cvalid = nidx < lax.shift_right_logical(pos + 1, 6)
    cur = lax.shift_right_logical(pos, 6)
    forced = (nidx == 0) | (nidx == cur) | (nidx == cur - 1)
    outs = []
    for k in range(N_KV_HEADS):
        ksl = slice(k * HEAD_DIM, (k + 1) * HEAD_DIM)
        imp = jnp.zeros((nbp, tq), F32)
        for g in range(GQA):
            h = k * GQA + g
            sc = lax.dot_general(kc[:, ksl], qs[:, h * HEAD_DIM:(h + 1) * HEAD_DIM], NT_DIMS,
                                 preferred_element_type=F32)
            sc = jnp.where(cvalid, sc, NEG)
            e = jnp.exp(sc - jnp.max(sc, axis=0, keepdims=True))
            pc = jnp.where(cvalid, e / jnp.sum(e, axis=0, keepdims=True), 0.0)
            imp = imp + pc
            outs.append(jnp.dot(vct[ksl, :], pc.astype(BF16), preferred_element_type=F32))
        s_ref[:, k * tq:(k + 1) * tq] = jnp.where(forced, FORCE, jnp.where(nidx < cur, imp, NEG))
    ocmp_ref[0] = jnp.concatenate(outs, axis=0)

    score = s_ref[...]
    rows = lax.broadcasted_iota(I32, score.shape, 0)

    def pick_one(_, carry):
        work, sel = carry
        top = jnp.max(work, axis=0, keepdims=True)
        first = jnp.min(jnp.where(work == top, rows, nbp), axis=0, keepdims=True)
        hit = rows == first
        return jnp.where(hit, -jnp.inf, work), jnp.where(hit, 1.0, sel)

    _, sel = lax.fori_loop(0, n_sel, pick_one, (score, jnp.zeros_like(score)))
    sel_ref[0] = jnp.where(score > NEG / 2, sel, 0.0)


def _cmpsel(proj3, kc, vct, *, pos0, tq, nb_total):
    b, s, _ = proj3.shape
    nbp = kc.shape[1]
    kern = functools.partial(_cmpsel_kernel, tq=tq, nbp=nbp, pos0=pos0, n_sel=min(TOP_N, nb_total))
    lanes = N_KV_HEADS * tq
    return pl.pallas_call(
        kern,
        grid=(b, s // tq),
        in_specs=[pl.BlockSpec((1, tq, ATT_WIDTH), lambda i, j: (i, j, C_Q // ATT_WIDTH)),
                  pl.BlockSpec((1, nbp, KV_WIDTH), lambda i, j: (i, 0, 0)),
                  pl.BlockSpec((1, KV_WIDTH, nbp), lambda i, j: (i, 0, 0))],
        out_specs=[pl.BlockSpec((1, ATT_WIDTH, tq), lambda i, j: (i, 0, j)),
                   pl.BlockSpec((1, nbp, lanes), lambda i, j: (i, 0, j))],
        out_shape=[jax.ShapeDtypeStruct((b, ATT_WIDTH, s), F32),
                   jax.ShapeDtypeStruct((b, nbp, N_KV_HEADS * s), F32)],
        scratch_shapes=[pltpu.VMEM((nbp, lanes), F32)],
        compiler_params=_params("parallel", "parallel"),
        name="cmpsel",
    )(proj3, kc, vct)


def _attn_init(q_ref, qs_ref, m_ref, l_ref, acc_ref, tq):
    qs = (q_ref[0] * ATT_SCALE).astype(BF16)
    for k in range(N_KV_HEADS):
        for g in range(GQA):
            h = k * GQA + g
            qs_ref[k, g * tq:(g + 1) * tq, :] = qs[:, h * HEAD_DIM:(h + 1) * HEAD_DIM]
    m_ref[...] = jnp.full_like(m_ref, NEG)
    l_ref[...] = jnp.zeros_like(l_ref)
    acc_ref[...] = jnp.zeros_like(acc_ref)


def _attn_step(k, lanes, k_tile, vt_tile, block_rows, ok, qs_ref, m_ref, l_ref, acc_ref):
    ksl = slice(k * HEAD_DIM, (k + 1) * HEAD_DIM)
    s = lax.dot_general(k_tile[:, ksl], qs_ref[k, lanes, :], NT_DIMS, preferred_element_type=F32)
    if block_rows is not None:
        s = jnp.concatenate([jnp.where(row > 0.5, s[i * BLOCK:(i + 1) * BLOCK], NEG)
                             for i, row in enumerate(block_rows)], axis=0)
    if ok is not None:
        s = jnp.where(ok, s, NEG)
    m_old = m_ref[k, :, lanes]
    m_new = jnp.maximum(m_old, jnp.max(s, axis=0, keepdims=True))
    p = jnp.exp(s - m_new)
    alpha = jnp.exp(m_old - m_new)
    l_ref[k, :, lanes] = alpha * l_ref[k, :, lanes] + jnp.sum(p, axis=0, keepdims=True)
    m_ref[k, :, lanes] = m_new
    acc_ref[k, :, lanes] = (alpha * acc_ref[k, :, lanes]
                            + jnp.dot(vt_tile[ksl, :], p.astype(BF16), preferred_element_type=F32))


def _attn_finish(o_ref, l_ref, acc_ref, tq):
    outs = []
    for k in range(N_KV_HEADS):
        o = acc_ref[k] / l_ref[k]
        for g in range(GQA):
            outs.append(o[:, g * tq:(g + 1) * tq])
    o_ref[0] = jnp.concatenate(outs, axis=0)


def _attn_scratch(tq):
    r = GQA * tq
    return [pltpu.VMEM((N_KV_HEADS, r, HEAD_DIM), BF16),
            pltpu.VMEM((N_KV_HEADS, 1, r), F32),
            pltpu.VMEM((N_KV_HEADS, 1, r), F32),
            pltpu.VMEM((N_KV_HEADS, HEAD_DIM, r), F32)]


def _lane_query_pos(shape, tq, first):
    return first + (lax.broadcasted_iota(I32, shape, 1) & (tq - 1))


def _selattn_kernel(jt_ref, kt_ref, q_ref, k_ref, vt_ref, sel_ref, o_ref, qs_ref, m_ref, l_ref, acc_ref, *, tq):
    step = pl.program_id(1)
    j = jt_ref[step]
    kt = kt_ref[step]

    @pl.when(kt == 0)
    def _():
        _attn_init(q_ref, qs_ref, m_ref, l_ref, acc_ref, tq)

    def key_tile(diagonal):
        k_tile = k_ref[0].astype(BF16)
        vt_tile = vt_ref[0].astype(BF16)
        r = GQA * tq
        causal = None
        if diagonal:
            causal = lax.broadcasted_iota(I32, (KEY_TILE, r), 0) <= _lane_query_pos((KEY_TILE, r), tq, 0)
        for k in range(N_KV_HEADS):
            rows = [jnp.concatenate([sel_ref[0, pl.ds(kt * BLOCKS_PER_TILE + i, 1), k * tq:(k + 1) * tq]] * GQA,
                                    axis=1) for i in range(BLOCKS_PER_TILE)]
            _attn_step(k, slice(0, r), k_tile, vt_tile, rows, causal, qs_ref, m_ref, l_ref, acc_ref)

    @pl.when(kt < j)
    def _():
        key_tile(False)

    @pl.when(kt == j)
    def _():
        key_tile(True)
        _attn_finish(o_ref, l_ref, acc_ref, tq)


def _selattn(proj3, vt, sel, *, tq):
    b, s, _ = proj3.shape
    assert tq == KEY_TILE
    nbp = sel.shape[1]
    pairs = [(j, kt) for j in range(s // tq) for kt in range(j + 1)]
    jt = jnp.asarray([p[0] for p in pairs], I32)
    ktt = jnp.asarray([p[1] for p in pairs], I32)
    kern = functools.partial(_selattn_kernel, tq=tq)
    return pl.pallas_call(
        kern,
        grid_spec=pltpu.PrefetchScalarGridSpec(
            num_scalar_prefetch=2, grid=(b, len(pairs)),
            in_specs=[pl.BlockSpec((1, tq, ATT_WIDTH), lambda i, p, jt, kt: (i, jt[p], C_Q // ATT_WIDTH)),
                      pl.BlockSpec((1, KEY_TILE, KV_WIDTH), lambda i, p, jt, kt: (i, kt[p], KV_COL + 2)),
                      pl.BlockSpec((1, KV_WIDTH, KEY_TILE), lambda i, p, jt, kt: (i, 0, kt[p])),
                      pl.BlockSpec((1, nbp, N_KV_HEADS * tq), lambda i, p, jt, kt: (i, 0, jt[p]))],
            out_specs=pl.BlockSpec((1, ATT_WIDTH, tq), lambda i, p, jt, kt: (i, 0, jt[p])),
            scratch_shapes=_attn_scratch(tq)),
        out_shape=jax.ShapeDtypeStruct((b, ATT_WIDTH, s), F32),
        compiler_params=_params("parallel", "arbitrary"),
        name="selattn",
    )(jt, ktt, proj3, proj3, vt, sel)


def _winattn_kernel(q_ref, k_ref, vt_ref, o_ref, qs_ref, m_ref, l_ref, acc_ref, *, tq, pos0, row_pos0, n_sub):
    j = pl.program_id(1)
    t = pl.program_id(2)
    r = GQA * tq

    @pl.when(t == 0)
    def _():
        _attn_init(q_ref, qs_ref, m_ref, l_ref, acc_ref, tq)

    tile = (j * tq) // KEY_TILE + (n_sub - 1 - t)
    kp = row_pos0 + tile * KEY_TILE + lax.broadcasted_iota(I32, (KEY_TILE, r), 0)
    pq = _lane_query_pos((KEY_TILE, r), tq, pos0 + j * tq)
    ok = (kp >= 0) & (kp <= pq) & (kp > pq - WINDOW)
    k_tile = k_ref[0].astype(BF16)
    vt_tile = vt_ref[0].astype(BF16)
    for k in range(N_KV_HEADS):
        _attn_step(k, slice(0, r), k_tile, vt_tile, None, ok, qs_ref, m_ref, l_ref, acc_ref)

    @pl.when(t == n_sub - 1)
    def _():
        _attn_finish(o_ref, l_ref, acc_ref, tq)


def _winattn(proj3, kwin, vwin_t, *, pos0, row_pos0, tq):
    b, s, _ = proj3.shape
    assert tq & (tq - 1) == 0 and (tq % KEY_TILE == 0 or s == tq)
    n_sub = -(-(WINDOW + tq) // KEY_TILE)
    assert kwin.shape[1] >= ((s - tq) // KEY_TILE + n_sub) * KEY_TILE

    def tile(j, t):
        return (j * tq) // KEY_TILE + (n_sub - 1 - t)

    kern = functools.partial(_winattn_kernel, tq=tq, pos0=pos0, row_pos0=row_pos0, n_sub=n_sub)
    return pl.pallas_call(
        kern,
        grid=(b, s // tq, n_sub),
        in_specs=[pl.BlockSpec((1, tq, ATT_WIDTH), lambda i, j, t: (i, j, C_Q // ATT_WIDTH)),
                  pl.BlockSpec((1, KEY_TILE, KV_WIDTH), lambda i, j, t: (i, tile(j, t), 0)),
                  pl.BlockSpec((1, KV_WIDTH, KEY_TILE), lambda i, j, t: (i, 0, tile(j, t)))],
        out_specs=pl.BlockSpec((1, ATT_WIDTH, tq), lambda i, j, t: (i, 0, j)),
        out_shape=jax.ShapeDtypeStruct((b, ATT_WIDTH, s), F32),
        scratch_shapes=_attn_scratch(tq),
        compiler_params=_params("parallel", "parallel", "arbitrary"),
        name="winattn",
    )(proj3, kwin, vwin_t)


def _selattn_paged_kernel(pt_ref, *refs, tq, n_steps):
    del pt_ref
    pages = refs[:PAGES_PER_STEP]
    q_ref, new_ref, sel_ref, o_ref, qbd_ref, selr_ref, m_ref, l_ref, acc_ref = refs[PAGES_PER_STEP:]
    st = pl.program_id(1)
    rows = N_HEADS * tq

    def flash(s, ok, v, v_is_transposed):
        s = jnp.where(ok, s, NEG)
        m_old = m_ref[...]
        m_new = jnp.maximum(m_old, jnp.max(s, axis=-1, keepdims=True))
        p = jnp.exp(s - m_new)
        alpha = jnp.exp(m_old - m_new)
        l_ref[...] = alpha * l_ref[...] + jnp.sum(p, axis=-1, keepdims=True)
        m_ref[...] = m_new
        if v_is_transposed:
            pv = lax.dot_general(p.astype(BF16), v, NT_DIMS, preferred_element_type=F32)
        else:
            pv = jnp.dot(p.astype(BF16), v, preferred_element_type=F32)
        acc_ref[...] = alpha * acc_ref[...] + pv

    @pl.when(st == 0)
    def _():
        qs = (q_ref[0] * ATT_SCALE).astype(BF16)
        qbd_ref[...] = jnp.zeros_like(qbd_ref)
        for k in range(N_KV_HEADS):
            for g in range(GQA):
                h = k * GQA + g
                qbd_ref[h * tq:(h + 1) * tq, k * HEAD_DIM:(k + 1) * HEAD_DIM] = qs[:, h * HEAD_DIM:(h + 1) * HEAD_DIM]
                selr_ref[:, h * tq:(h + 1) * tq] = sel_ref[0, :, k * tq:(k + 1) * tq]
        m_ref[...] = jnp.full_like(m_ref, NEG)
        l_ref[...] = jnp.zeros_like(l_ref)
        acc_ref[...] = jnp.zeros_like(acc_ref)

    @pl.when(st < n_steps)
    def _():
        blk = st * (PAGES_PER_STEP * BLOCKS_PER_PAGE)
        chosen = jnp.concatenate(
            [jnp.broadcast_to(selr_ref[pl.ds(blk + c, 1), :], (BLOCK, rows))
             for c in range(PAGES_PER_STEP * BLOCKS_PER_PAGE)], axis=0)
        k_t = jnp.concatenate([page[0, 0, 0] for page in pages], axis=1).astype(BF16)
        v_t = jnp.concatenate([page[0, 0, 1] for page in pages], axis=1).astype(BF16)
        s = jnp.dot(qbd_ref[...], k_t, preferred_element_type=F32)
        flash(s, chosen.T > 0.5, v_t, True)

    @pl.when(st == n_steps)
    def _():
        new = new_ref[0]
        k_new = new[:, 2 * KV_WIDTH:3 * KV_WIDTH].astype(BF16)
        v_new = new[:, 3 * KV_WIDTH:4 * KV_WIDTH].astype(BF16)
        s = lax.dot_general(qbd_ref[...], k_new, NT_DIMS, preferred_element_type=F32)
        blk = n_steps * PAGES_PER_STEP * BLOCKS_PER_PAGE
        chosen = jnp.broadcast_to(selr_ref[pl.ds(blk, 1), :], (tq, rows)).T
        causal = (lax.broadcasted_iota(I32, (rows, tq), 1)
                  <= (lax.broadcasted_iota(I32, (rows, tq), 0) & (tq - 1)))
        flash(s, (chosen > 0.5) & causal, v_new, False)
        o = acc_ref[...] / l_ref[...]
        outs = []
        for k in range(N_KV_HEADS):
            for g in range(GQA):
                h = k * GQA + g
                outs.append(o[h * tq:(h + 1) * tq, k * HEAD_DIM:(k + 1) * HEAD_DIM].T)
        o_ref[0] = jnp.concatenate(outs, axis=0)


def _selattn_paged(proj3, cache5, layer, page_table, sel):
    b, tq, _ = proj3.shape
    _, n_pages = page_table.shape
    assert tq & (tq - 1) == 0 and tq <= BLOCK and n_pages % PAGES_PER_STEP == 0
    n_steps = n_pages // PAGES_PER_STEP
    nbp = sel.shape[1]
    rows = N_HEADS * tq
    kern = functools.partial(_selattn_paged_kernel, tq=tq, n_steps=n_steps)
    return pl.pallas_call(
        kern,
        grid_spec=pltpu.PrefetchScalarGridSpec(
            num_scalar_prefetch=1, grid=(b, n_steps + 1),
            in_specs=_page_specs(layer, n_pages, 1) + [
                pl.BlockSpec((1, tq, ATT_WIDTH), lambda bi, st, pt: (bi, 0, C_Q // ATT_WIDTH)),
                pl.BlockSpec((1, tq, 4 * KV_WIDTH), lambda bi, st, pt: (bi, 0, C_KV // (4 * KV_WIDTH))),
                pl.BlockSpec((1, nbp, N_KV_HEADS * tq), lambda bi, st, pt: (bi, 0, 0))],
            out_specs=pl.BlockSpec((1, ATT_WIDTH, tq), lambda bi, st, pt: (bi, 0, 0)),
            scratch_shapes=[pltpu.VMEM((rows, KV_WIDTH), BF16), pltpu.VMEM((nbp, rows), F32),
                            pltpu.VMEM((rows, 1), F32), pltpu.VMEM((rows, 1), F32),
                            pltpu.VMEM((rows, KV_WIDTH), F32)]),
        out_shape=jax.ShapeDtypeStruct((b, ATT_WIDTH, tq), F32),
        compiler_params=_params("parallel", "arbitrary"),
        name="selattn_paged",
    )(page_table, *([cache5] * PAGES_PER_STEP), proj3, proj3, sel)


def _tcols_kernel(x_ref, o_ref):
    o_ref[0] = x_ref[0].T


def _tcols(proj3, col):
    b, s, _ = proj3.shape
    ts = min(s, 512)
    return pl.pallas_call(
        _tcols_kernel,
        grid=(b, s // ts),
        in_specs=[pl.BlockSpec((1, ts, KV_WIDTH), lambda i, t: (i, t, col))],
        out_specs=pl.BlockSpec((1, KV_WIDTH, ts), lambda i, t: (i, 0, t)),
        out_shape=jax.ShapeDtypeStruct((b, KV_WIDTH, s), F32),
        compiler_params=_params("parallel", "parallel"),
        name="tcols",
    )(proj3)


def _attcomb_kernel(oc_ref, os_ref, ow_ref, ag_ref, az_ref, o_ref):
    gate_t = jax.nn.sigmoid(ag_ref[0]).T
    oc, osel, ow = oc_ref[0], os_ref[0], ow_ref[0]
    outs = []
    for h in range(N_HEADS):
        sl = slice(h * HEAD_DIM, (h + 1) * HEAD_DIM)
        c = h * N_ATT_GATES
        outs.append(gate_t[c:c + 1, :] * oc[sl, :] + gate_t[c + 1:c + 2, :] * osel[sl, :]
                    + gate_t[c + 2:c + 3, :] * ow[sl, :])
    o_ref[0] = (jnp.concatenate(outs, axis=0).T * _silu(az_ref[0])).astype(BF16)


def _attcomb(ocmp_t, osel_t, owin_t, gates3, proj3):
    b, s, _ = proj3.shape
    tt = min(s, 256)
    att_t = pl.BlockSpec((1, ATT_WIDTH, tt), lambda i, t: (i, 0, t))
    return pl.pallas_call(
        _attcomb_kernel,
        grid=(b, s // tt),
        in_specs=[att_t, att_t, att_t,
                  pl.BlockSpec((1, tt, N_GATE_PAD), lambda i, t: (i, t, 0)),
                  pl.BlockSpec((1, tt, ATT_WIDTH), lambda i, t: (i, t, C_AZ // ATT_WIDTH))],
        out_specs=pl.BlockSpec((1, tt, ATT_WIDTH), lambda i, t: (i, t, 0)),
        out_shape=jax.ShapeDtypeStruct((b, s, ATT_WIDTH), BF16),
        compiler_params=_params("parallel", "parallel"),
        name="attcomb",
    )(ocmp_t, osel_t, owin_t, gates3, proj3)


def _ssm_disc_kernel(lr_ref, li_ref, ls_ref, brt_ref, bit_ref, pr_ref, pi_ref, bbr_ref, bbi_ref):
    lr, li = lr_ref[...], li_ref[...]
    dt = jnp.exp(ls_ref[...])
    mag = jnp.exp(lr * dt)
    ab_re, ab_im = mag * jnp.cos(li * dt), mag * jnp.sin(li * dt)
    den = lr * lr + li * li
    co_re = ((ab_re - 1.0) * lr + ab_im * li) / den
    co_im = (ab_im * lr - (ab_re - 1.0) * li) / den
    brt, bit = brt_ref[...], bit_ref[...]
    bbr_ref[...] = co_re[:, None, :] * brt - co_im[:, None, :] * bit
    bbi_ref[...] = co_re[:, None, :] * bit + co_im[:, None, :] * brt
    pr, pi = ab_re, ab_im
    pr_ref[0], pi_ref[0] = pr, pi
    for r in range(1, SUBLANES):
        pr, pi = pr * ab_re - pi * ab_im, pr * ab_im + pi * ab_re
        pr_ref[r], pi_ref[r] = pr, pi


def _ssm_disc(lam_re, lam_im, log_step, b_re, b_im):
    g, n = lam_re.shape
    brt = jnp.swapaxes(b_re, 1, 2)
    bit = jnp.swapaxes(b_im, 1, 2)
    pw = jax.ShapeDtypeStruct((SUBLANES, g, n), F32)
    bb = jax.ShapeDtypeStruct(brt.shape, F32)
    return pl.pallas_call(_ssm_disc_kernel, out_shape=[pw, pw, bb, bb], name="ssm_disc")(
        lam_re, lam_im, log_step.reshape(g, 1), brt, bit)


def _ssm_kernel(su_ref, sz_ref, h0_ref, tab_ref, bmr_ref, bmi_ref, cm_ref, ds_ref, wg_ref,
                o_ref, st_ref, hr_ref, hi_ref, c_ref, *, tt, row_last):
    t = pl.program_id(1)

    @pl.when(t == 0)
    def _():
        c_ref[...] = h0_ref[0]

    u = su_ref[0]
    ub = u.astype(BF16)
    n_mm = SSM_WIDTH // 128
    for j in range(n_mm):
        uj = ub[:, j * 128:(j + 1) * 128]
        hr_ref[:, j * 512:(j + 1) * 512] = jnp.dot(uj, bmr_ref[j], preferred_element_type=F32)
        hi_ref[:, j * 512:(j + 1) * 512] = jnp.dot(uj, bmi_ref[j], preferred_element_type=F32)

    for cc in range(SSM_CH // SSM_LANE_CHUNK):
        lanes = slice(cc * SSM_LANE_CHUNK, (cc + 1) * SSM_LANE_CHUNK)
        tabs = [(tab_ref[2 * i, :, lanes], tab_ref[2 * i + 1, :, lanes]) for i in range(4)]

        def group(gi, carry):
            cr, ci = carry
            rows = pl.ds(pl.multiple_of(gi * SUBLANES, SUBLANES), SUBLANES)
            xr, xi = hr_ref[rows, lanes], hi_ref[rows, lanes]
            for lvl, shift in enumerate((1, 2, 4)):
                ar, ai = tabs[lvl]
                sr, si = pltpu.roll(xr, shift, axis=0), pltpu.roll(xi, shift, axis=0)
                xr, xi = xr + (ar * sr - ai * si), xi + (ar * si + ai * sr)
            pr, pi = tabs[3]
            xr, xi = xr + (pr * cr - pi * ci), xi + (pr * ci + pi * cr)
            hr_ref[rows, lanes], hi_ref[rows, lanes] = xr, xi
            return xr[SUBLANES - 1:SUBLANES, :], xi[SUBLANES - 1:SUBLANES, :]

        cr, ci = lax.fori_loop(0, tt // SUBLANES, group, (c_ref[0:1, lanes], c_ref[1:2, lanes]))
        c_ref[0:1, lanes], c_ref[1:2, lanes] = cr, ci

    @pl.when(t == pl.num_programs(1) - 1)
    def _():
        st_ref[0, 0:1, :] = hr_ref[row_last:row_last + 1, :]
        st_ref[0, 1:2, :] = hi_ref[row_last:row_last + 1, :]

    ys = []
    for j in range(n_mm):
        hcat = jnp.concatenate([hr_ref[:, j * 512:(j + 1) * 512], hi_ref[:, j * 512:(j + 1) * 512]], axis=-1)
        ys.append(jnp.dot(hcat.astype(BF16), cm_ref[j], preferred_element_type=F32))
    y = jnp.concatenate(ys, axis=-1) + ds_ref[...] * u
    z = jax.nn.gelu(y)
    out = z * jax.nn.sigmoid(jnp.dot(z.astype(BF16), wg_ref[...], preferred_element_type=F32))
    o_ref[0] = (out * _silu(sz_ref[0])).astype(BF16)


def _ssm(proj3, h0, tab, bmr, bmi, cm, d_skip, w_glu, *, s_valid):
    b, s, _ = proj3.shape
    tt = min(s, 256)
    row_last = (s_valid - 1) % tt
    n_mm = SSM_WIDTH // 128
    kern = functools.partial(_ssm_kernel, tt=tt, row_last=row_last)
    const3 = lambda i, t: (0, 0, 0)
    return pl.pallas_call(
        kern,
        grid=(b, s // tt),
        in_specs=[pl.BlockSpec((1, tt, SSM_WIDTH), lambda i, t: (i, t, C_SU // SSM_WIDTH)),
                  pl.BlockSpec((1, tt, SSM_WIDTH), lambda i, t: (i, t, C_SZ // SSM_WIDTH)),
                  pl.BlockSpec((1, 2, SSM_CH), lambda i, t: (i, 0, 0)),
                  pl.BlockSpec((8, SUBLANES, SSM_CH), const3),
                  pl.BlockSpec((n_mm, 128, 512), const3),
                  pl.BlockSpec((n_mm, 128, 512), const3),
                  pl.BlockSpec((n_mm, 1024, 128), const3),
                  pl.BlockSpec((1, SSM_WIDTH), lambda i, t: (0, 0)),
                  pl.BlockSpec((SSM_WIDTH, SSM_WIDTH), lambda i, t: (0, 0))],
        out_specs=[pl.BlockSpec((1, tt, SSM_WIDTH), lambda i, t: (i, t, 0)),
                   pl.BlockSpec((1, 2, SSM_CH), lambda i, t: (i, 0, 0))],
        out_shape=[jax.ShapeDtypeStruct((b, s, SSM_WIDTH), BF16),
                   jax.ShapeDtypeStruct((b, 2, SSM_CH), F32)],
        scratch_shapes=[pltpu.VMEM((tt, SSM_CH), F32), pltpu.VMEM((tt, SSM_CH), F32),
                        pltpu.VMEM((2, SSM_CH), F32)],
        compiler_params=_params("parallel", "arbitrary"),
        name="ssm",
    )(proj3, proj3, h0, tab, bmr, bmi, cm, d_skip, w_glu)


def _ssm_tables(lam_re, lam_im, log_step, b_re, b_im, c_re, c_im):
    pw_re, pw_im, bbr, bbi = _ssm_disc(lam_re, lam_im, log_step, b_re, b_im)
    pw_re = pw_re.reshape(SUBLANES, SSM_CH)
    pw_im = pw_im.reshape(SUBLANES, SSM_CH)
    row = jnp.arange(SUBLANES)[:, None]
    tabs = []
    for shift in (1, 2, 4):
        keep = row >= shift
        tabs += [jnp.where(keep, pw_re[shift - 1][None], 0.0), jnp.where(keep, pw_im[shift - 1][None], 0.0)]
    tab = jnp.stack(tabs + [pw_re, pw_im])
    eye = jnp.eye(8, dtype=F32)

    def b_blocks(bb):
        x = bb.reshape(SSM_GROUPS // 8, 8, SSM_GROUP_DIM, SSM_STATE)
        return jnp.einsum("jgcn,gh->jgchn", x, eye).reshape(SSM_GROUPS // 8, 128, 512).astype(BF16)

    def c_blocks(c):
        x = c.reshape(SSM_GROUPS // 8, 8, SSM_GROUP_DIM, SSM_STATE)
        return jnp.einsum("jgcn,gh->jgnhc", x, eye).reshape(SSM_GROUPS // 8, 512, 128)

    cm = jnp.concatenate([c_blocks(c_re), -c_blocks(c_im)], axis=1).astype(BF16)
    return tab, b_blocks(bbr), b_blocks(bbi), cm


def _merge_kernel(ap_ref, aa_ref, as_ref, wp_ref, wa_ref, ws_ref, g0_ref, g1_ref, g2_ref, o_ref):
    bp = jnp.dot(ap_ref[...], wp_ref[...], preferred_element_type=F32)
    ba = jnp.dot(aa_ref[...], wa_ref[...], preferred_element_type=F32)
    bs = jnp.dot(as_ref[...], ws_ref[...], preferred_element_type=F32)
    o_ref[...] = (jax.nn.sigmoid(g0_ref[...]) * bp + jax.nn.sigmoid(g1_ref[...]) * ba
                  + jax.nn.sigmoid(g2_ref[...]) * bs).astype(BF16)


def _merge(a_pool, a_att, a_ssm, w_p, w_a, w_s, proj2):
    m = a_pool.shape[0]
    tm = min(m, 512)
    tn = 1024
    act = pl.BlockSpec((tm, 1024), lambda i, n: (i, 0))
    wsp = pl.BlockSpec((1024, tn), lambda i, n: (0, n))
    gate = lambda r: pl.BlockSpec((tm, tn), lambda i, n: (i, (C_MG + r * D_MODEL) // tn + n))
    return pl.pallas_call(
        _merge_kernel,
        grid=(m // tm, D_MODEL // tn),
        in_specs=[act, act, act, wsp, wsp, wsp, gate(0), gate(1), gate(2)],
        out_specs=pl.BlockSpec((tm, tn), lambda i, n: (i, n)),
        out_shape=jax.ShapeDtypeStruct((m, D_MODEL), BF16),
        compiler_params=_params("parallel", "arbitrary"),
        name="merge",
    )(a_pool, a_att, a_ssm, w_p, w_a, w_s, proj2, proj2, proj2)


def _outproj_kernel(m_ref, w_ref, g_ref, x_ref, o_ref):
    y = jnp.dot(m_ref[...], w_ref[...], preferred_element_type=F32)
    r = lax.rsqrt(jnp.mean(y * y, axis=-1, keepdims=True) + EPS)
    o_ref[...] = x_ref[...] + y * r * g_ref[...]


def _outproj(merged, w_out, g_post, x2d):
    m = merged.shape[0]
    tm = min(m, 512)
    row = pl.BlockSpec((tm, D_MODEL), lambda i: (i, 0))
    return pl.pallas_call(
        _outproj_kernel,
        grid=(m // tm,),
        in_specs=[row, pl.BlockSpec((D_MODEL, D_MODEL), lambda i: (0, 0)),
                  pl.BlockSpec((1, D_MODEL), lambda i: (0, 0)), row],
        out_specs=row,
        out_shape=jax.ShapeDtypeStruct((m, D_MODEL), F32),
        compiler_params=_params("parallel"),
        name="outproj",
    )(merged, w_out, g_post, x2d)


def _prep_layer_weights(l, g_pre, g_post, w_in, w_pool, pool_scale, pe_cmp, w_phi, lam_re, lam_im, log_step,
                        b_re, b_im, c_re, c_im, d_skip, w_glu, w_br_pool, w_br_nsa, w_br_ssm, w_out):
    w = w_in[l]
    kv0 = 2 * POOL_WIDTH + ATT_WIDTH
    ag0 = kv0 + 6 * KV_WIDTH
    az0 = ag0 + N_HEADS * N_ATT_GATES
    w_main = jnp.concatenate([w[:, :kv0], w[:, az0:], w[:, kv0:ag0]], axis=1).astype(BF16)
    w_gate = jnp.pad(w[:, ag0:az0], ((0, 0), (0, N_GATE_PAD - N_HEADS * N_ATT_GATES))).astype(BF16)
    eye = jnp.eye(N_KV_HEADS, dtype=F32)
    w_cmp = [jnp.einsum("lde,kh->lkdhe", w_phi[l, c], eye).reshape(BLOCK, KV_WIDTH, KV_WIDTH).astype(BF16)
             for c in range(2)]
    pe = [jnp.broadcast_to(pe_cmp[l, c][:, None, :], (BLOCK, N_KV_HEADS, HEAD_DIM)).reshape(BLOCK, KV_WIDTH)
          for c in range(2)]
    tab, bmr, bmi, cm = _ssm_tables(lam_re[l], lam_im[l], log_step[l], b_re[l], b_im[l], c_re[l], c_im[l])
    return dict(
        g_pre=g_pre[l].reshape(1, D_MODEL), g_post=g_post[l].reshape(1, D_MODEL),
        w_main=w_main, w_gate=w_gate, w_pool=w_pool[l].astype(BF16), pool_scale=pool_scale[l].reshape(1, POOL_WIDTH),
        w_cmp=w_cmp, pe=pe, tab=tab, bmr=bmr, bmi=bmi, cm=cm, d_skip=d_skip[l].reshape(1, SSM_WIDTH),
        w_glu=w_glu[l].astype(BF16), w_br_pool=w_br_pool[l].astype(BF16), w_br_nsa=w_br_nsa[l].astype(BF16),
        w_br_ssm=w_br_ssm[l].astype(BF16), w_out=w_out[l].astype(BF16))


def _layer(x3, lw, *, pos0, s_valid, pool_prefix, ssm_h0, win_prefix, paged):
    b, s, _ = x3.shape
    x2 = x3.reshape(b * s, D_MODEL)
    proj2 = _inproj(x2, lw["g_pre"], lw["w_main"], 768)
    gates3 = _inproj(x2, lw["g_pre"], lw["w_gate"], N_GATE_PAD).reshape(b, s, N_GATE_PAD)
    proj3 = proj2.reshape(b, s, N_MAIN)

    prefix16 = jnp.pad(pool_prefix, ((0, 0), (1, 0), (0, 0)))
    a_pool, pool_tail = _pool(proj3, prefix16, lw["w_pool"], lw["pool_scale"], q0=pos0, s_valid=s_valid)
    pool_state = pool_tail[:, 1:]

    kvn = proj3[:, :, C_KV:]
    if paged is None:
        tq = min(s, 256)
        n_cmp = nb_total = nbp = s // BLOCK
        x3k = x3v = proj3.reshape(b * n_cmp, BLOCK, N_MAIN)
        kcol, vcol = KV_COL, KV_COL + 1
    else:
        tq = s
        cache5, layer, page_table = paged
        past_len = page_table.shape[1] * PAGE_SIZE
        n_cmp = past_len // BLOCK
        nb_total = -(-(past_len + s_valid) // BLOCK)
        nbp = -(-nb_total // SUBLANES) * SUBLANES
    if paged is None:
        kc = _compress(x3k, kcol, lw["pe"][0], lw["w_cmp"][0])
        vc = _compress(x3v, vcol, lw["pe"][1], lw["w_cmp"][1])
    else:
        pe2 = jnp.stack([jnp.tile(pe, (BLOCKS_PER_PAGE, 1)) for pe in lw["pe"]])
        xk_rows, xv_rows = _gather_cmp(cache5, layer, page_table, pe2)
        kc = _compress_rows(xk_rows, lw["w_cmp"][0])
        vc = _compress_rows(xv_rows, lw["w_cmp"][1])
    kc = kc.reshape(b, n_cmp, KV_WIDTH)
    vc = vc.reshape(b, n_cmp, KV_WIDTH)
    kc = jnp.pad(kc, ((0, 0), (0, nbp - n_cmp), (0, 0)))
    vct = jnp.swapaxes(jnp.pad(vc, ((0, 0), (0, nbp - n_cmp), (0, 0))), 1, 2)
    o_cmp, sel = _cmpsel(proj3, kc, vct, pos0=pos0, tq=tq, nb_total=nb_total)
    if paged is None:
        assert pos0 == 0
        o_sel = _selattn(proj3, _tcols(proj3, KV_COL + 3), sel, tq=tq)
    else:
        o_sel = _selattn_paged(proj3, cache5, layer, page_table, sel)

    tq_win = min(s, 256)
    n_rows = ((s - tq_win) // KEY_TILE + -(-(WINDOW + tq_win) // KEY_TILE)) * KEY_TILE
    pad_rows = n_rows - WINDOW - s
    k_new, v_new = kvn[:, :, 4 * KV_WIDTH:5 * KV_WIDTH], kvn[:, :, 5 * KV_WIDTH:6 * KV_WIDTH]
    v_new_t = _tcols(proj3, KV_COL + 5)
    if win_prefix is None:
        kwin = jnp.pad(k_new, ((0, 0), (WINDOW, pad_rows), (0, 0)))
        vwin_t = jnp.pad(v_new_t, ((0, 0), (0, 0), (WINDOW, pad_rows)))
    else:
        k_pre = win_prefix[:, :, 0].reshape(b, WINDOW, KV_WIDTH)
        v_pre = win_prefix[:, :, 1].reshape(b, WINDOW, KV_WIDTH)
        kwin = jnp.pad(jnp.concatenate([k_pre, k_new], axis=1), ((0, 0), (0, pad_rows), (0, 0)))
        vwin_t = jnp.pad(jnp.concatenate([jnp.swapaxes(v_pre, 1, 2), v_new_t], axis=2),
                         ((0, 0), (0, 0), (0, pad_rows)))
    o_win = _winattn(proj3, kwin, vwin_t, pos0=pos0, row_pos0=pos0 - WINDOW, tq=tq_win)
    a_att = _attcomb(o_cmp, o_sel, o_win, gates3, proj3)

    a_ssm, ssm_state = _ssm(proj3, ssm_h0.reshape(b, 2, SSM_CH), lw["tab"], lw["bmr"], lw["bmi"], lw["cm"],
                            lw["d_skip"], lw["w_glu"], s_valid=s_valid)

    merged = _merge(a_pool.reshape(b * s, POOL_WIDTH), a_att.reshape(b * s, ATT_WIDTH),
                    a_ssm.reshape(b * s, SSM_WIDTH), lw["w_br_pool"], lw["w_br_nsa"], lw["w_br_ssm"], proj2)
    x_new = _outproj(merged, lw["w_out"], lw["g_post"], x2).reshape(b, s, D_MODEL)

    kv_rows = kvn[:, :s_valid, :4 * KV_WIDTH].reshape(b, s_valid, 4, N_KV_HEADS, HEAD_DIM)
    if win_prefix is None:
        assert s_valid >= WINDOW
        win_k, win_v = k_new[:, s_valid - WINDOW:s_valid], v_new[:, s_valid - WINDOW:s_valid]
    else:
        win_k = jnp.concatenate([k_pre, k_new], axis=1)[:, s_valid:s_valid + WINDOW]
        win_v = jnp.concatenate([v_pre, v_new], axis=1)[:, s_valid:s_valid + WINDOW]
    win_state = jnp.stack([win_k.reshape(b, WINDOW, N_KV_HEADS, HEAD_DIM),
                           win_v.reshape(b, WINDOW, N_KV_HEADS, HEAD_DIM)], axis=2)
    return x_new, kv_rows, win_state, pool_state, ssm_state.reshape(b, 2, SSM_GROUPS, SSM_STATE)


def kernel(x_prompt, x_sample, cache_kv, page_table, state_win_kv, state_pool, state_ssm, g_pre, g_post, w_in, w_pool, pool_scale, pe_cmp, w_phi, lam_re, lam_im, log_step, b_re, b_im, c_re, c_im, d_skip, w_glu, w_br_pool, w_br_nsa, w_br_ssm, w_out):
    depth = w_in.shape[0]
    bp, sp, _ = x_prompt.shape
    bd, sd, _ = x_sample.shape
    n_pool = cache_kv.shape[1]
    past_len = page_table.shape[1] * PAGE_SIZE
    assert state_win_kv.shape[2] == WINDOW and past_len >= WINDOW and sp >= WINDOW
    cache5 = jnp.transpose(cache_kv, (0, 1, 3, 4, 5, 2)).reshape(depth, n_pool, 4, KV_WIDTH, PAGE_SIZE)
    sd_pad = -(-sd // SUBLANES) * SUBLANES
    yp = x_prompt
    ys = jnp.pad(x_sample, ((0, 0), (0, sd_pad - sd), (0, 0)))
    zeros_pool = jnp.zeros((bp, POOL_STATE, POOL_WIDTH), F32)
    zeros_ssm = jnp.zeros((bp, 2, SSM_GROUPS, SSM_STATE), F32)
    outs_p, outs_s = [], []
    for l in range(depth):
        lw = _prep_layer_weights(l, g_pre, g_post, w_in, w_pool, pool_scale, pe_cmp, w_phi, lam_re, lam_im, log_step,
                                 b_re, b_im, c_re, c_im, d_skip, w_glu, w_br_pool, w_br_nsa, w_br_ssm, w_out)
        yp, *rp = _layer(yp, lw, pos0=0, s_valid=sp, pool_prefix=zeros_pool, ssm_h0=zeros_ssm,
                         win_prefix=None, paged=None)
        ys, *rs = _layer(ys, lw, pos0=past_len, s_valid=sd, pool_prefix=state_pool[l], ssm_h0=state_ssm[l],
                         win_prefix=state_win_kv[l], paged=(cache5, l, page_table))
        outs_p.append(rp)
        outs_s.append(rs)
    stack = lambda outs, i: jnp.stack([o[i] for o in outs])
    return (yp, ys[:, :sd], stack(outs_p, 0), stack(outs_s, 0), stack(outs_p, 1), stack(outs_s, 1),
            stack(outs_p, 2), stack(outs_s, 2), stack(outs_p, 3), stack(outs_s, 3))
```

```python
import functools

import jax
import jax.numpy as jnp
from jax import lax
from jax.experimental import pallas as pl
from jax.experimental.pallas import tpu as pltpu

F32 = jnp.float32
BF16 = jnp.bfloat16
I32 = jnp.int32

D_MODEL = 2048
PAGE_SIZE = 128
POOL_WIDTH = D_MODEL // 2
POOL_WINDOWS = (2, 4, 8, 16)
POOL_GROUP_DIM = POOL_WIDTH // len(POOL_WINDOWS)
POOL_STATE = max(POOL_WINDOWS) - 1
N_HEADS = 16
HEAD_DIM = 64
N_KV_HEADS = 4
GQA = N_HEADS // N_KV_HEADS
ATT_WIDTH = N_HEADS * HEAD_DIM
KV_WIDTH = N_KV_HEADS * HEAD_DIM
BLOCK = 64
TOP_N = 16
WINDOW = 512
N_ATT_GATES = 3
SSM_WIDTH = D_MODEL // 2
SSM_GROUP_DIM = 16
SSM_GROUPS = SSM_WIDTH // SSM_GROUP_DIM
SSM_STATE = 64
SSM_CH = SSM_GROUPS * SSM_STATE
N_BRANCH = 3
EPS = 1e-6
NEG = -1e30
FORCE = 1e4
ATT_SCALE = HEAD_DIM ** -0.5
LOG2_E = 1.4426950408889634

C_PU, C_PZ, C_Q, C_AZ, C_SU, C_SZ = (i * 1024 for i in range(6))
C_MG = 6 * 1024
C_KV = C_MG + N_BRANCH * D_MODEL
N_MAIN = C_KV + 6 * KV_WIDTH
N_GATE_PAD = 128
KV_COL = C_KV // KV_WIDTH

VMEM_LIMIT_BYTES = 52 * 1024 * 1024
SUBLANES = 8
KEY_TILE = 256
BLOCKS_PER_TILE = KEY_TILE // BLOCK
BLOCKS_PER_PAGE = PAGE_SIZE // BLOCK
PAGES_PER_STEP = 16
SSM_LANE_CHUNK = 1024

NT_DIMS = (((1,), (1,)), ((), ()))


def _params(*sem):
    return pltpu.CompilerParams(dimension_semantics=sem, vmem_limit_bytes=VMEM_LIMIT_BYTES)


def _silu(x):
    return x * jax.nn.sigmoid(x)


def _inproj_kernel(x_ref, g_ref, w_ref, o_ref, h_ref):
    @pl.when(pl.program_id(1) == 0)
    def _():
        x = x_ref[...]
        r = lax.rsqrt(jnp.mean(x * x, axis=-1, keepdims=True) + EPS)
        h_ref[...] = (x * r * g_ref[...]).astype(BF16)

    o_ref[...] = lax.dot_general(h_ref[...], w_ref[...], NT_DIMS, preferred_element_type=F32)


def _inproj(x2d, g, wt, tn):
    m, d = x2d.shape
    n = wt.shape[0]
    tm = min(m, 1024)
    return pl.pallas_call(
        _inproj_kernel,
        grid=(m // tm, n // tn),
        in_specs=[pl.BlockSpec((tm, d), lambda i, j: (i, 0)),
                  pl.BlockSpec((1, d), lambda i, j: (0, 0)),
                  pl.BlockSpec((tn, d), lambda i, j: (j, 0))],
        out_specs=pl.BlockSpec((tm, tn), lambda i, j: (i, j)),
        out_shape=jax.ShapeDtypeStruct((m, n), F32),
        scratch_shapes=[pltpu.VMEM((tm, d), BF16)],
        compiler_params=_params("parallel", "arbitrary"),
        name="inproj",
    )(x2d, g, wt)


def _pool_kernel(pu_ref, pz_ref, pre_ref, wp_ref, sc_ref, o_ref, st_ref, e_ref, *, tt, q0, rows_last):
    t = pl.program_id(1)
    hist = POOL_STATE + 1

    @pl.when(t == 0)
    def _():
        e_ref[0:hist, :] = pre_ref[0]

    @pl.when(t > 0)
    def _():
        e_ref[0:hist, :] = e_ref[tt:tt + hist, :]

    u = pu_ref[0]
    e_ref[hist:hist + tt, :] = u
    pos = q0 + t * tt + lax.broadcasted_iota(I32, (tt, 1), 0)
    ys = []
    for gi, w in enumerate(POOL_WINDOWS):
        lo = gi * POOL_GROUP_DIM
        tot = e_ref[hist:hist + tt, lo:lo + POOL_GROUP_DIM]
        for k in range(1, w):
            tot = tot + e_ref[hist - k:hist - k + tt, lo:lo + POOL_GROUP_DIM]
        cnt = jnp.minimum(pos + 1, w).astype(F32)
        diff = tot / cnt - u[:, lo:lo + POOL_GROUP_DIM]
        ys.append(jnp.dot(diff.astype(BF16), wp_ref[gi], preferred_element_type=F32))
    y = jnp.concatenate(ys, axis=-1) * sc_ref[...]
    o_ref[0] = (y * _silu(pz_ref[0])).astype(BF16)

    @pl.when(t == pl.num_programs(1) - 1)
    def _():
        st_ref[0] = e_ref[rows_last:rows_last + hist, :]


def _pool(proj3, prefix16, w_pool, pool_scale, *, q0, s_valid):
    b, s, _ = proj3.shape
    tt = min(s, 512)
    hist = POOL_STATE + 1
    rows_last = ((s_valid - 1) % tt) + 1
    kern = functools.partial(_pool_kernel, tt=tt, q0=q0, rows_last=rows_last)
    return pl.pallas_call(
        kern,
        grid=(b, s // tt),
        in_specs=[pl.BlockSpec((1, tt, POOL_WIDTH), lambda i, t: (i, t, C_PU // POOL_WIDTH)),
                  pl.BlockSpec((1, tt, POOL_WIDTH), lambda i, t: (i, t, C_PZ // POOL_WIDTH)),
                  pl.BlockSpec((1, hist, POOL_WIDTH), lambda i, t: (i, 0, 0)),
                  pl.BlockSpec((len(POOL_WINDOWS), POOL_GROUP_DIM, POOL_GROUP_DIM), lambda i, t: (0, 0, 0)),
                  pl.BlockSpec((1, POOL_WIDTH), lambda i, t: (0, 0))],
        out_specs=[pl.BlockSpec((1, tt, POOL_WIDTH), lambda i, t: (i, t, 0)),
                   pl.BlockSpec((1, hist, POOL_WIDTH), lambda i, t: (i, 0, 0))],
        out_shape=[jax.ShapeDtypeStruct((b, s, POOL_WIDTH), BF16),
                   jax.ShapeDtypeStruct((b, hist, POOL_WIDTH), F32)],
        scratch_shapes=[pltpu.VMEM((hist + tt, POOL_WIDTH), F32)],
        compiler_params=_params("parallel", "arbitrary"),
        name="pool",
    )(proj3, proj3, prefix16, w_pool, pool_scale)


def _compress_kernel(x_ref, pe_ref, w_ref, o_ref):
    def one_row(l, acc):
        x = x_ref[:, l, :] + pe_ref[pl.ds(l, 1), :]
        return acc + jnp.dot(x.astype(BF16), w_ref[l], preferred_element_type=F32)

    o_ref[...] = lax.fori_loop(0, BLOCK, one_row, jnp.zeros(o_ref.shape, F32))


def _compress(x3, col, pe, w):
    nb = x3.shape[0]
    tm = min(nb, 128)
    return pl.pallas_call(
        _compress_kernel,
        grid=(nb // tm,),
        in_specs=[pl.BlockSpec((tm, BLOCK, KV_WIDTH), lambda i: (i, 0, col)),
                  pl.BlockSpec((BLOCK, KV_WIDTH), lambda i: (0, 0)),
                  pl.BlockSpec((BLOCK, KV_WIDTH, KV_WIDTH), lambda i: (0, 0, 0))],
        out_specs=pl.BlockSpec((tm, KV_WIDTH), lambda i: (i, 0)),
        out_shape=jax.ShapeDtypeStruct((nb, KV_WIDTH), F32),
        compiler_params=_params("parallel"),
        name="compress",
    )(x3, pe, w)


def _compress_rows_kernel(x_ref, w_ref, o_ref):
    x = jnp.concatenate([x_ref[l] for l in range(BLOCK)], axis=1)
    o_ref[...] = jnp.dot(x, w_ref[...].reshape(BLOCK * KV_WIDTH, KV_WIDTH), preferred_element_type=F32)


def _compress_rows(xl, w):
    nb = xl.shape[1]
    tm = min(nb, 256)
    return pl.pallas_call(
        _compress_rows_kernel,
        grid=(nb // tm,),
        in_specs=[pl.BlockSpec((BLOCK, tm, KV_WIDTH), lambda i: (0, i, 0)),
                  pl.BlockSpec((BLOCK, KV_WIDTH, KV_WIDTH), lambda i: (0, 0, 0))],
        out_specs=pl.BlockSpec((tm, KV_WIDTH), lambda i: (i, 0)),
        out_shape=jax.ShapeDtypeStruct((nb, KV_WIDTH), F32),
        compiler_params=_params("parallel"),
        name="compress_rows",
    )(xl, w)


def _page_specs(layer, n_pages, comp_block):
    def spec(i):
        def index(bi, st, pt):
            return (layer, pt[bi, jnp.minimum(st * PAGES_PER_STEP + i, n_pages - 1)], comp_block, 0, 0)
        return pl.BlockSpec((1, 1, 2, KV_WIDTH, PAGE_SIZE), index)
    return [spec(i) for i in range(PAGES_PER_STEP)]


def _sublane_transpose8(a):
    sub = lax.broadcasted_iota(I32, a[0].shape, 0)
    for shift in (4, 2, 1):
        low = (sub & shift) == 0
        nxt = list(a)
        for j in range(SUBLANES):
            if j & shift == 0:
                nxt[j] = jnp.where(low, a[j], pltpu.roll(a[j + shift], shift, axis=0))
                nxt[j + shift] = jnp.where(low, pltpu.roll(a[j], SUBLANES - shift, axis=0), a[j + shift])
        a = nxt
    return a


def _gather_cmp_kernel(pt_ref, *refs):
    del pt_ref
    pages, (pe_ref, xk_ref, xv_ref) = refs[:PAGES_PER_STEP], refs[PAGES_PER_STEP:]
    n_groups = PAGES_PER_STEP * BLOCKS_PER_PAGE // SUBLANES
    for comp, dst in enumerate((xk_ref, xv_ref)):
        rows = [page[0, 0, comp].T + pe_ref[comp] for page in pages]
        for m in range(BLOCK // SUBLANES):
            per_group = []
            for grp in range(n_groups):
                pieces = []
                for jj in range(SUBLANES):
                    j = grp * SUBLANES + jj
                    lo = (j % BLOCKS_PER_PAGE) * BLOCK + m * SUBLANES
                    pieces.append(rows[j // BLOCKS_PER_PAGE][lo:lo + SUBLANES, :])
                per_group.append(_sublane_transpose8(pieces))
            for s in range(SUBLANES):
                dst[m * SUBLANES + s] = jnp.concatenate([g[s] for g in per_group], axis=0).astype(BF16)


def _gather_cmp(cache5, layer, page_table, pe2):
    b, n_pages = page_table.shape
    n_steps = n_pages // PAGES_PER_STEP
    blocks_per_step = PAGES_PER_STEP * BLOCKS_PER_PAGE
    assert blocks_per_step % (2 * SUBLANES) == 0
    out_spec = pl.BlockSpec((BLOCK, blocks_per_step, KV_WIDTH), lambda bi, st, pt: (0, bi * n_steps + st, 0))
    out = jax.ShapeDtypeStruct((BLOCK, b * n_pages * BLOCKS_PER_PAGE, KV_WIDTH), BF16)
    return pl.pallas_call(
        _gather_cmp_kernel,
        grid_spec=pltpu.PrefetchScalarGridSpec(
            num_scalar_prefetch=1, grid=(b, n_steps),
            in_specs=_page_specs(layer, n_pages, 0) + [
                pl.BlockSpec((2, PAGE_SIZE, KV_WIDTH), lambda bi, st, pt: (0, 0, 0))],
            out_specs=[out_spec, out_spec]),
        out_shape=[out, out],
        compiler_params=_params("parallel", "arbitrary"),
        name="gather_cmp",
    )(page_table, *([cache5] * PAGES_PER_STEP), pe2)


def _cmpsel_kernel(q_ref, kc_ref, vct_ref, ocmp_ref, sel_ref, s_ref, *, tq, nbp, pos0, n_sel):
    j = pl.program_id(1)
    qs = (q_ref[0] * ATT_SCALE).astype(BF16)
    kc = kc_ref[0].astype(BF16)
    vct = vct_ref[0].astype(BF16)
    nidx = lax.broadcasted_iota(I32, (nbp, tq), 0)
    pos = pos0 + j * tq + lax.broadcasted_iota(I32, (nbp, tq), 1)
    cvalid = nidx < lax.shift_right_logical(pos + 1, 6)
    cur = lax.shift_right_logical(pos, 6)
    forced = (nidx == 0) | (nidx == cur) | (nidx == cur - 1)
    outs = []
    for k in range(N_KV_HEADS):
        ksl = slice(k * HEAD_DIM, (k + 1) * HEAD_DIM)
        imp = jnp.zeros((nbp, tq), F32)
        for g in range(GQA):
            h = k * GQA + g
            sc = lax.dot_general(kc[:, ksl], qs[:, h * HEAD_DIM:(h + 1) * HEAD_DIM], NT_DIMS,
                                 preferred_element_type=F32)
            sc = jnp.where(cvalid, sc, NEG)
            e = jnp.exp(sc - jnp.max(sc, axis=0, keepdims=True))
            pc = jnp.where(cvalid, e / jnp.sum(e, axis=0, keepdims=True), 0.0)
            imp = imp + pc
            outs.append(jnp.dot(vct[ksl, :], pc.astype(BF16), preferred_element_type=F32))
        s_ref[:, k * tq:(k + 1) * tq] = jnp.where(forced, FORCE, jnp.where(nidx < cur, imp, NEG))
    ocmp_ref[0] = jnp.concatenate(outs, axis=0)

    score = s_ref[...]
    rows = lax.broadcasted_iota(I32, score.shape, 0)

    def pick_one(_, carry):
        work, sel = carry
        top = jnp.max(work, axis=0, keepdims=True)
        first = jnp.min(jnp.where(work == top, rows, nbp), axis=0, keepdims=True)
        hit = rows == first
        return jnp.where(hit, -jnp.inf, work), jnp.where(hit, 1.0, sel)

    _, sel = lax.fori_loop(0, n_sel, pick_one, (score, jnp.zeros_like(score)))
    sel_ref[0] = jnp.where(score > NEG / 2, sel, 0.0)


def _cmpsel(proj3, kc, vct, *, pos0, tq, nb_total):
    b, s, _ = proj3.shape
    nbp = kc.shape[1]
    kern = functools.partial(_cmpsel_kernel, tq=tq, nbp=nbp, pos0=pos0, n_sel=min(TOP_N, nb_total))
    lanes = N_KV_HEADS * tq
    return pl.pallas_call(
        kern,
        grid=(b, s // tq),
        in_specs=[pl.BlockSpec((1, tq, ATT_WIDTH), lambda i, j: (i, j, C_Q // ATT_WIDTH)),
                  pl.BlockSpec((1, nbp, KV_WIDTH), lambda i, j: (i, 0, 0)),
                  pl.BlockSpec((1, KV_WIDTH, nbp), lambda i, j: (i, 0, 0))],
        out_specs=[pl.BlockSpec((1, ATT_WIDTH, tq), lambda i, j: (i, 0, j)),
                   pl.BlockSpec((1, nbp, lanes), lambda i, j: (i, 0, j))],
        out_shape=[jax.ShapeDtypeStruct((b, ATT_WIDTH, s), F32),
                   jax.ShapeDtypeStruct((b, nbp, N_KV_HEADS * s), F32)],
        scratch_shapes=[pltpu.VMEM((nbp, lanes), F32)],
        compiler_params=_params("parallel", "parallel"),
        name="cmpsel",
    )(proj3, kc, vct)


def _attn_init(q_ref, qs_ref, m_ref, l_ref, acc_ref, tq):
    qs = (q_ref[0] * (ATT_SCALE * LOG2_E)).astype(BF16)
    for k in range(N_KV_HEADS):
        for g in range(GQA):
            h = k * GQA + g
            qs_ref[k, g * tq:(g + 1) * tq, :] = qs[:, h * HEAD_DIM:(h + 1) * HEAD_DIM]
    m_ref[...] = jnp.full_like(m_ref, NEG)
    l_ref[...] = jnp.zeros_like(l_ref)
    acc_ref[...] = jnp.zeros_like(acc_ref)


def _attn_step(k, lanes, k_tile, vt_tile, block_rows, ok, qs_ref, m_ref, l_ref, acc_ref):
    ksl = slice(k * HEAD_DIM, (k + 1) * HEAD_DIM)
    s = lax.dot_general(k_tile[:, ksl], qs_ref[k, lanes, :], NT_DIMS, preferred_element_type=F32)
    if block_rows is not None:
        s = jnp.concatenate([jnp.where(row > 0.5, s[i * BLOCK:(i + 1) * BLOCK], NEG)
                             for i, row in enumerate(block_rows)], axis=0)
    if ok is not None:
        s = jnp.where(ok, s, NEG)
    m_old = m_ref[k, :, lanes]
    m_new = jnp.maximum(m_old, jnp.max(s, axis=0, keepdims=True))
    p = jnp.exp2(s - m_new)
    alpha = jnp.exp2(m_old - m_new)
    l_ref[k, :, lanes] = alpha * l_ref[k, :, lanes] + jnp.sum(p, axis=0, keepdims=True)
    m_ref[k, :, lanes] = m_new
    acc_ref[k, :, lanes] = (alpha * acc_ref[k, :, lanes]
                            + jnp.dot(vt_tile[ksl, :], p.astype(BF16), preferred_element_type=F32))


def _attn_finish(o_ref, l_ref, acc_ref, tq):
    outs = []
    for k in range(N_KV_HEADS):
        o = acc_ref[k] / l_ref[k]
        for g in range(GQA):
            outs.append(o[:, g * tq:(g + 1) * tq])
    o_ref[0] = jnp.concatenate(outs, axis=0)


def _attn_scratch(tq):
    r = GQA * tq
    return [pltpu.VMEM((N_KV_HEADS, r, HEAD_DIM), BF16),
            pltpu.VMEM((N_KV_HEADS, 1, r), F32),
            pltpu.VMEM((N_KV_HEADS, 1, r), F32),
            pltpu.VMEM((N_KV_HEADS, HEAD_DIM, r), F32)]


def _lane_query_pos(shape, tq, first):
    return first + (lax.broadcasted_iota(I32, shape, 1) & (tq - 1))


def _selattn_kernel(jt_ref, kt_ref, q_ref, k_ref, vt_ref, sel_ref, o_ref, qs_ref, m_ref, l_ref, acc_ref, *, tq):
    step = pl.program_id(1)
    j = jt_ref[step]
    kt = kt_ref[step]

    @pl.when(kt == 0)
    def _():
        _attn_init(q_ref, qs_ref, m_ref, l_ref, acc_ref, tq)

    def key_tile(diagonal):
        k_tile = k_ref[0].astype(BF16)
        vt_tile = vt_ref[0].astype(BF16)
        r = GQA * tq
        causal = None
        if diagonal:
            causal = lax.broadcasted_iota(I32, (KEY_TILE, r), 0) <= _lane_query_pos((KEY_TILE, r), tq, 0)
        for k in range(N_KV_HEADS):
            rows = [jnp.concatenate([sel_ref[0, pl.ds(kt * BLOCKS_PER_TILE + i, 1), k * tq:(k + 1) * tq]] * GQA,
                                    axis=1) for i in range(BLOCKS_PER_TILE)]
            _attn_step(k, slice(0, r), k_tile, vt_tile, rows, causal, qs_ref, m_ref, l_ref, acc_ref)

    @pl.when(kt < j)
    def _():
        key_tile(False)

    @pl.when(kt == j)
    def _():
        key_tile(True)
        _attn_finish(o_ref, l_ref, acc_ref, tq)


def _selattn(proj3, vt, sel, *, tq):
    b, s, _ = proj3.shape
    assert tq == KEY_TILE
    nbp = sel.shape[1]
    pairs = [(j, kt) for j in range(s // tq) for kt in range(j + 1)]
    jt = jnp.asarray([p[0] for p in pairs], I32)
    ktt = jnp.asarray([p[1] for p in pairs], I32)
    kern = functools.partial(_selattn_kernel, tq=tq)
    return pl.pallas_call(
        kern,
        grid_spec=pltpu.PrefetchScalarGridSpec(
            num_scalar_prefetch=2, grid=(b, len(pairs)),
            in_specs=[pl.BlockSpec((1, tq, ATT_WIDTH), lambda i, p, jt, kt: (i, jt[p], C_Q // ATT_WIDTH)),
                      pl.BlockSpec((1, KEY_TILE, KV_WIDTH), lambda i, p, jt, kt: (i, kt[p], KV_COL + 2)),
                      pl.BlockSpec((1, KV_WIDTH, KEY_TILE), lambda i, p, jt, kt: (i, 0, kt[p])),
                      pl.BlockSpec((1, nbp, N_KV_HEADS * tq), lambda i, p, jt, kt: (i, 0, jt[p]))],
            out_specs=pl.BlockSpec((1, ATT_WIDTH, tq), lambda i, p, jt, kt: (i, 0, jt[p])),
            scratch_shapes=_attn_scratch(tq)),
        out_shape=jax.ShapeDtypeStruct((b, ATT_WIDTH, s), F32),
        compiler_params=_params("parallel", "arbitrary"),
        name="selattn",
    )(jt, ktt, proj3, proj3, vt, sel)


def _winattn_kernel(q_ref, k_ref, vt_ref, o_ref, qs_ref, m_ref, l_ref, acc_ref, *, tq, pos0, row_pos0, n_sub):
    j = pl.program_id(1)
    t = pl.program_id(2)
    r = GQA * tq

    @pl.when(t == 0)
    def _():
        _attn_init(q_ref, qs_ref, m_ref, l_ref, acc_ref, tq)

    tile = (j * tq) // KEY_TILE + (n_sub - 1 - t)
    kp = row_pos0 + tile * KEY_TILE + lax.broadcasted_iota(I32, (KEY_TILE, r), 0)
    pq = _lane_query_pos((KEY_TILE, r), tq, pos0 + j * tq)
    ok = (kp >= 0) & (kp <= pq) & (kp > pq - WINDOW)
    k_tile = k_ref[0].astype(BF16)
    vt_tile = vt_ref[0].astype(BF16)
    for k in range(N_KV_HEADS):
        _attn_step(k, slice(0, r), k_tile, vt_tile, None, ok, qs_ref, m_ref, l_ref, acc_ref)

    @pl.when(t == n_sub - 1)
    def _():
        _attn_finish(o_ref, l_ref, acc_ref, tq)


def _winattn(proj3, kwin, vwin_t, *, pos0, row_pos0, tq):
    b, s, _ = proj3.shape
    assert tq & (tq - 1) == 0 and (tq % KEY_TILE == 0 or s == tq)
    n_sub = -(-(WINDOW + tq) // KEY_TILE)
    assert kwin.shape[1] >= ((s - tq) // KEY_TILE + n_sub) * KEY_TILE

    def tile(j, t):
        return (j * tq) // KEY_TILE + (n_sub - 1 - t)

    kern = functools.partial(_winattn_kernel, tq=tq, pos0=pos0, row_pos0=row_pos0, n_sub=n_sub)
    return pl.pallas_call(
        kern,
        grid=(b, s // tq, n_sub),
        in_specs=[pl.BlockSpec((1, tq, ATT_WIDTH), lambda i, j, t: (i, j, C_Q // ATT_WIDTH)),
                  pl.BlockSpec((1, KEY_TILE, KV_WIDTH), lambda i, j, t: (i, tile(j, t), 0)),
                  pl.BlockSpec((1, KV_WIDTH, KEY_TILE), lambda i, j, t: (i, 0, tile(j, t)))],
        out_specs=pl.BlockSpec((1, ATT_WIDTH, tq), lambda i, j, t: (i, 0, j)),
        out_shape=jax.ShapeDtypeStruct((b, ATT_WIDTH, s), F32),
        scratch_shapes=_attn_scratch(tq),
        compiler_params=_params("parallel", "parallel", "arbitrary"),
        name="winattn",
    )(proj3, kwin, vwin_t)


def _selattn_paged_kernel(pt_ref, *refs, tq, n_steps):
    del pt_ref
    pages = refs[:PAGES_PER_STEP]
    q_ref, new_ref, sel_ref, o_ref, qbd_ref, selr_ref, m_ref, l_ref, acc_ref = refs[PAGES_PER_STEP:]
    st = pl.program_id(1)
    rows = N_HEADS * tq

    def flash(s, ok, v, v_is_transposed):
        s = jnp.where(ok, s, NEG)
        m_old = m_ref[...]
        m_new = jnp.maximum(m_old, jnp.max(s, axis=-1, keepdims=True))
        p = jnp.exp(s - m_new)
        alpha = jnp.exp(m_old - m_new)
        l_ref[...] = alpha * l_ref[...] + jnp.sum(p, axis=-1, keepdims=True)
        m_ref[...] = m_new
        if v_is_transposed:
            pv = lax.dot_general(p.astype(BF16), v, NT_DIMS, preferred_element_type=F32)
        else:
            pv = jnp.dot(p.astype(BF16), v, preferred_element_type=F32)
        acc_ref[...] = alpha * acc_ref[...] + pv

    @pl.when(st == 0)
    def _():
        qs = (q_ref[0] * ATT_SCALE).astype(BF16)
        qbd_ref[...] = jnp.zeros_like(qbd_ref)
        for k in range(N_KV_HEADS):
            for g in range(GQA):
                h = k * GQA + g
                qbd_ref[h * tq:(h + 1) * tq, k * HEAD_DIM:(k + 1) * HEAD_DIM] = qs[:, h * HEAD_DIM:(h + 1) * HEAD_DIM]
                selr_ref[:, h * tq:(h + 1) * tq] = sel_ref[0, :, k * tq:(k + 1) * tq]
        m_ref[...] = jnp.full_like(m_ref, NEG)
        l_ref[...] = jnp.zeros_like(l_ref)
        acc_ref[...] = jnp.zeros_like(acc_ref)

    @pl.when(st < n_steps)
    def _():
        blk = st * (PAGES_PER_STEP * BLOCKS_PER_PAGE)
        chosen = jnp.concatenate(
            [jnp.broadcast_to(selr_ref[pl.ds(blk + c, 1), :], (BLOCK, rows))
             for c in range(PAGES_PER_STEP * BLOCKS_PER_PAGE)], axis=0)
        k_t = jnp.concatenate([page[0, 0, 0] for page in pages], axis=1).astype(BF16)
        v_t = jnp.concatenate([page[0, 0, 1] for page in pages], axis=1).astype(BF16)
        s = jnp.dot(qbd_ref[...], k_t, preferred_element_type=F32)
        flash(s, chosen.T > 0.5, v_t, True)

    @pl.when(st == n_steps)
    def _():
        new = new_ref[0]
        k_new = new[:, 2 * KV_WIDTH:3 * KV_WIDTH].astype(BF16)
        v_new = new[:, 3 * KV_WIDTH:4 * KV_WIDTH].astype(BF16)
        s = lax.dot_general(qbd_ref[...], k_new, NT_DIMS, preferred_element_type=F32)
        blk = n_steps * PAGES_PER_STEP * BLOCKS_PER_PAGE
        chosen = jnp.broadcast_to(selr_ref[pl.ds(blk, 1), :], (tq, rows)).T
        causal = (lax.broadcasted_iota(I32, (rows, tq), 1)
                  <= (lax.broadcasted_iota(I32, (rows, tq), 0) & (tq - 1)))
        flash(s, (chosen > 0.5) & causal, v_new, False)
        o = acc_ref[...] / l_ref[...]
        outs = []
        for k in range(N_KV_HEADS):
            for g in range(GQA):
                h = k * GQA + g
                outs.append(o[h * tq:(h + 1) * tq, k * HEAD_DIM:(k + 1) * HEAD_DIM].T)
        o_ref[0] = jnp.concatenate(outs, axis=0)


def _selattn_paged(proj3, cache5, layer, page_table, sel):
    b, tq, _ = proj3.shape
    _, n_pages = page_table.shape
    assert tq & (tq - 1) == 0 and tq <= BLOCK and n_pages % PAGES_PER_STEP == 0
    n_steps = n_pages // PAGES_PER_STEP
    nbp = sel.shape[1]
    rows = N_HEADS * tq
    kern = functools.partial(_selattn_paged_kernel, tq=tq, n_steps=n_steps)
    return pl.pallas_call(
        kern,
        grid_spec=pltpu.PrefetchScalarGridSpec(
            num_scalar_prefetch=1, grid=(b, n_steps + 1),
            in_specs=_page_specs(layer, n_pages, 1) + [
                pl.BlockSpec((1, tq, ATT_WIDTH), lambda bi, st, pt: (bi, 0, C_Q // ATT_WIDTH)),
                pl.BlockSpec((1, tq, 4 * KV_WIDTH), lambda bi, st, pt: (bi, 0, C_KV // (4 * KV_WIDTH))),
                pl.BlockSpec((1, nbp, N_KV_HEADS * tq), lambda bi, st, pt: (bi, 0, 0))],
            out_specs=pl.BlockSpec((1, ATT_WIDTH, tq), lambda bi, st, pt: (bi, 0, 0)),
            scratch_shapes=[pltpu.VMEM((rows, KV_WIDTH), BF16), pltpu.VMEM((nbp, rows), F32),
                            pltpu.VMEM((rows, 1), F32), pltpu.VMEM((rows, 1), F32),
                            pltpu.VMEM((rows, KV_WIDTH), F32)]),
        out_shape=jax.ShapeDtypeStruct((b, ATT_WIDTH, tq), F32),
        compiler_params=_params("parallel", "arbitrary"),
        name="selattn_paged",
    )(page_table, *([cache5] * PAGES_PER_STEP), proj3, proj3, sel)


def _tcols_kernel(x_ref, o_ref):
    o_ref[0] = x_ref[0].T


def _tcols(proj3, col):
    b, s, _ = proj3.shape
    ts = min(s, 512)
    return pl.pallas_call(
        _tcols_kernel,
        grid=(b, s // ts),
        in_specs=[pl.BlockSpec((1, ts, KV_WIDTH), lambda i, t: (i, t, col))],
        out_specs=pl.BlockSpec((1, KV_WIDTH, ts), lambda i, t: (i, 0, t)),
        out_shape=jax.ShapeDtypeStruct((b, KV_WIDTH, s), F32),
        compiler_params=_params("parallel", "parallel"),
        name="tcols",
    )(proj3)


def _attcomb_kernel(oc_ref, os_ref, ow_ref, ag_ref, az_ref, o_ref):
    gate_t = jax.nn.sigmoid(ag_ref[0]).T
    oc, osel, ow = oc_ref[0], os_ref[0], ow_ref[0]
    outs = []
    for h in range(N_HEADS):
        sl = slice(h * HEAD_DIM, (h + 1) * HEAD_DIM)
        c = h * N_ATT_GATES
        outs.append(gate_t[c:c + 1, :] * oc[sl, :] + gate_t[c + 1:c + 2, :] * osel[sl, :]
                    + gate_t[c + 2:c + 3, :] * ow[sl, :])
    o_ref[0] = (jnp.concatenate(outs, axis=0).T * _silu(az_ref[0])).astype(BF16)


def _attcomb(ocmp_t, osel_t, owin_t, gates3, proj3):
    b, s, _ = proj3.shape
    tt = min(s, 256)
    att_t = pl.BlockSpec((1, ATT_WIDTH, tt), lambda i, t: (i, 0, t))
    return pl.pallas_call(
        _attcomb_kernel,
        grid=(b, s // tt),
        in_specs=[att_t, att_t, att_t,
                  pl.BlockSpec((1, tt, N_GATE_PAD), lambda i, t: (i, t, 0)),
                  pl.BlockSpec((1, tt, ATT_WIDTH), lambda i, t: (i, t, C_AZ // ATT_WIDTH))],
        out_specs=pl.BlockSpec((1, tt, ATT_WIDTH), lambda i, t: (i, t, 0)),
        out_shape=jax.ShapeDtypeStruct((b, s, ATT_WIDTH), BF16),
        compiler_params=_params("parallel", "parallel"),
        name="attcomb",
    )(ocmp_t, osel_t, owin_t, gates3, proj3)


def _ssm_disc_kernel(lr_ref, li_ref, ls_ref, brt_ref, bit_ref, pr_ref, pi_ref, bbr_ref, bbi_ref):
    lr, li = lr_ref[...], li_ref[...]
    dt = jnp.exp(ls_ref[...])
    mag = jnp.exp(lr * dt)
    ab_re, ab_im = mag * jnp.cos(li * dt), mag * jnp.sin(li * dt)
    den = lr * lr + li * li
    co_re = ((ab_re - 1.0) * lr + ab_im * li) / den
    co_im = (ab_im * lr - (ab_re - 1.0) * li) / den
    brt, bit = brt_ref[...], bit_ref[...]
    bbr_ref[...] = co_re[:, None, :] * brt - co_im[:, None, :] * bit
    bbi_ref[...] = co_re[:, None, :] * bit + co_im[:, None, :] * brt
    pr, pi = ab_re, ab_im
    pr_ref[0], pi_ref[0] = pr, pi
    for r in range(1, SUBLANES):
        pr, pi = pr * ab_re - pi * ab_im, pr * ab_im + pi * ab_re
        pr_ref[r], pi_ref[r] = pr, pi


def _ssm_disc(lam_re, lam_im, log_step, b_re, b_im):
    g, n = lam_re.shape
    brt = jnp.swapaxes(b_re, 1, 2)
    bit = jnp.swapaxes(b_im, 1, 2)
    pw = jax.ShapeDtypeStruct((SUBLANES, g, n), F32)
    bb = jax.ShapeDtypeStruct(brt.shape, F32)
    return pl.pallas_call(_ssm_disc_kernel, out_shape=[pw, pw, bb, bb], name="ssm_disc")(
        lam_re, lam_im, log_step.reshape(g, 1), brt, bit)


def _ssm_kernel(su_ref, sz_ref, h0_ref, tab_ref, bmr_ref, bmi_ref, cm_ref, ds_ref, wg_ref,
                o_ref, st_ref, hr_ref, hi_ref, c_ref, *, tt, row_last):
    t = pl.program_id(1)

    @pl.when(t == 0)
    def _():
        c_ref[...] = h0_ref[0]

    u = su_ref[0]
    ub = u.astype(BF16)
    n_mm = SSM_WIDTH // 128
    for j in range(n_mm):
        uj = ub[:, j * 128:(j + 1) * 128]
        hr_ref[:, j * 512:(j + 1) * 512] = jnp.dot(uj, bmr_ref[j], preferred_element_type=F32)
        hi_ref[:, j * 512:(j + 1) * 512] = jnp.dot(uj, bmi_ref[j], preferred_element_type=F32)

    for cc in range(SSM_CH // SSM_LANE_CHUNK):
        lanes = slice(cc * SSM_LANE_CHUNK, (cc + 1) * SSM_LANE_CHUNK)
        tabs = [(tab_ref[2 * i, :, lanes], tab_ref[2 * i + 1, :, lanes]) for i in range(4)]

        def group(gi, carry):
            cr, ci = carry
            rows = pl.ds(pl.multiple_of(gi * SUBLANES, SUBLANES), SUBLANES)
            xr, xi = hr_ref[rows, lanes], hi_ref[rows, lanes]
            for lvl, shift in enumerate((1, 2, 4)):
                ar, ai = tabs[lvl]
                sr, si = pltpu.roll(xr, shift, axis=0), pltpu.roll(xi, shift, axis=0)
                xr, xi = xr + (ar * sr - ai * si), xi + (ar * si + ai * sr)
            pr, pi = tabs[3]
            xr, xi = xr + (pr * cr - pi * ci), xi + (pr * ci + pi * cr)
            hr_ref[rows, lanes], hi_ref[rows, lanes] = xr, xi
            return xr[SUBLANES - 1:SUBLANES, :], xi[SUBLANES - 1:SUBLANES, :]

        cr, ci = lax.fori_loop(0, tt // SUBLANES, group, (c_ref[0:1, lanes], c_ref[1:2, lanes]))
        c_ref[0:1, lanes], c_ref[1:2, lanes] = cr, ci

    @pl.when(t == pl.num_programs(1) - 1)
    def _():
        st_ref[0, 0:1, :] = hr_ref[row_last:row_last + 1, :]
        st_ref[0, 1:2, :] = hi_ref[row_last:row_last + 1, :]

    ys = []
    for j in range(n_mm):
        hcat = jnp.concatenate([hr_ref[:, j * 512:(j + 1) * 512], hi_ref[:, j * 512:(j + 1) * 512]], axis=-1)
        ys.append(jnp.dot(hcat.astype(BF16), cm_ref[j], preferred_element_type=F32))
    y = jnp.concatenate(ys, axis=-1) + ds_ref[...] * u
    z = jax.nn.gelu(y)
    out = z * jax.nn.sigmoid(jnp.dot(z.astype(BF16), wg_ref[...], preferred_element_type=F32))
    o_ref[0] = (out * _silu(sz_ref[0])).astype(BF16)


def _ssm(proj3, h0, tab, bmr, bmi, cm, d_skip, w_glu, *, s_valid):
    b, s, _ = proj3.shape
    tt = min(s, 256)
    row_last = (s_valid - 1) % tt
    n_mm = SSM_WIDTH // 128
    kern = functools.partial(_ssm_kernel, tt=tt, row_last=row_last)
    const3 = lambda i, t: (0, 0, 0)
    return pl.pallas_call(
        kern,
        grid=(b, s // tt),
        in_specs=[pl.BlockSpec((1, tt, SSM_WIDTH), lambda i, t: (i, t, C_SU // SSM_WIDTH)),
                  pl.BlockSpec((1, tt, SSM_WIDTH), lambda i, t: (i, t, C_SZ // SSM_WIDTH)),
                  pl.BlockSpec((1, 2, SSM_CH), lambda i, t: (i, 0, 0)),
                  pl.BlockSpec((8, SUBLANES, SSM_CH), const3),
                  pl.BlockSpec((n_mm, 128, 512), const3),
                  pl.BlockSpec((n_mm, 128, 512), const3),
                  pl.BlockSpec((n_mm, 1024, 128), const3),
                  pl.BlockSpec((1, SSM_WIDTH), lambda i, t: (0, 0)),
                  pl.BlockSpec((SSM_WIDTH, SSM_WIDTH), lambda i, t: (0, 0))],
        out_specs=[pl.BlockSpec((1, tt, SSM_WIDTH), lambda i, t: (i, t, 0)),
                   pl.BlockSpec((1, 2, SSM_CH), lambda i, t: (i, 0, 0))],
        out_shape=[jax.ShapeDtypeStruct((b, s, SSM_WIDTH), BF16),
                   jax.ShapeDtypeStruct((b, 2, SSM_CH), F32)],
        scratch_shapes=[pltpu.VMEM((tt, SSM_CH), F32), pltpu.VMEM((tt, SSM_CH), F32),
                        pltpu.VMEM((2, SSM_CH), F32)],
        compiler_params=_params("parallel", "arbitrary"),
        name="ssm",
    )(proj3, proj3, h0, tab, bmr, bmi, cm, d_skip, w_glu)


def _ssm_tables(lam_re, lam_im, log_step, b_re, b_im, c_re, c_im):
    pw_re, pw_im, bbr, bbi = _ssm_disc(lam_re, lam_im, log_step, b_re, b_im)
    pw_re = pw_re.reshape(SUBLANES, SSM_CH)
    pw_im = pw_im.reshape(SUBLANES, SSM_CH)
    row = jnp.arange(SUBLANES)[:, None]
    tabs = []
    for shift in (1, 2, 4):
        keep = row >= shift
        tabs += [jnp.where(keep, pw_re[shift - 1][None], 0.0), jnp.where(keep, pw_im[shift - 1][None], 0.0)]
    tab = jnp.stack(tabs + [pw_re, pw_im])
    eye = jnp.eye(8, dtype=F32)

    def b_blocks(bb):
        x = bb.reshape(SSM_GROUPS // 8, 8, SSM_GROUP_DIM, SSM_STATE)
        return jnp.einsum("jgcn,gh->jgchn", x, eye).reshape(SSM_GROUPS // 8, 128, 512).astype(BF16)

    def c_blocks(c):
        x = c.reshape(SSM_GROUPS // 8, 8, SSM_GROUP_DIM, SSM_STATE)
        return jnp.einsum("jgcn,gh->jgnhc", x, eye).reshape(SSM_GROUPS // 8, 512, 128)

    cm = jnp.concatenate([c_blocks(c_re), -c_blocks(c_im)], axis=1).astype(BF16)
    return tab, b_blocks(bbr), b_blocks(bbi), cm


def _merge_kernel(ap_ref, aa_ref, as_ref, wp_ref, wa_ref, ws_ref, g0_ref, g1_ref, g2_ref, o_ref):
    bp = jnp.dot(ap_ref[...], wp_ref[...], preferred_element_type=F32)
    ba = jnp.dot(aa_ref[...], wa_ref[...], preferred_element_type=F32)
    bs = jnp.dot(as_ref[...], ws_ref[...], preferred_element_type=F32)
    o_ref[...] = (jax.nn.sigmoid(g0_ref[...]) * bp + jax.nn.sigmoid(g1_ref[...]) * ba
                  + jax.nn.sigmoid(g2_ref[...]) * bs).astype(BF16)


def _merge(a_pool, a_att, a_ssm, w_p, w_a, w_s, proj2):
    m = a_pool.shape[0]
    tm = min(m, 512)
    tn = 1024
    act = pl.BlockSpec((tm, 1024), lambda i, n: (i, 0))
    wsp = pl.BlockSpec((1024, tn), lambda i, n: (0, n))
    gate = lambda r: pl.BlockSpec((tm, tn), lambda i, n: (i, (C_MG + r * D_MODEL) // tn + n))
    return pl.pallas_call(
        _merge_kernel,
        grid=(m // tm, D_MODEL // tn),
        in_specs=[act, act, act, wsp, wsp, wsp, gate(0), gate(1), gate(2)],
        out_specs=pl.BlockSpec((tm, tn), lambda i, n: (i, n)),
        out_shape=jax.ShapeDtypeStruct((m, D_MODEL), BF16),
        compiler_params=_params("parallel", "arbitrary"),
        name="merge",
    )(a_pool, a_att, a_ssm, w_p, w_a, w_s, proj2, proj2, proj2)


def _outproj_kernel(m_ref, w_ref, g_ref, x_ref, o_ref):
    y = jnp.dot(m_ref[...], w_ref[...], preferred_element_type=F32)
    r = lax.rsqrt(jnp.mean(y * y, axis=-1, keepdims=True) + EPS)
    o_ref[...] = x_ref[...] + y * r * g_ref[...]


def _outproj(merged, w_out, g_post, x2d):
    m = merged.shape[0]
    tm = min(m, 512)
    row = pl.BlockSpec((tm, D_MODEL), lambda i: (i, 0))
    return pl.pallas_call(
        _outproj_kernel,
        grid=(m // tm,),
        in_specs=[row, pl.BlockSpec((D_MODEL, D_MODEL), lambda i: (0, 0)),
                  pl.BlockSpec((1, D_MODEL), lambda i: (0, 0)), row],
        out_specs=row,
        out_shape=jax.ShapeDtypeStruct((m, D_MODEL), F32),
        compiler_params=_params("parallel"),
        name="outproj",
    )(merged, w_out, g_post, x2d)


def _prep_layer_weights(l, g_pre, g_post, w_in, w_pool, pool_scale, pe_cmp, w_phi, lam_re, lam_im, log_step,
                        b_re, b_im, c_re, c_im, d_skip, w_glu, w_br_pool, w_br_nsa, w_br_ssm, w_out):
    w = w_in[l]
    kv0 = 2 * POOL_WIDTH + ATT_WIDTH
    ag0 = kv0 + 6 * KV_WIDTH
    az0 = ag0 + N_HEADS * N_ATT_GATES
    wt = jnp.swapaxes(w, 0, 1)
    w_main = jnp.concatenate([wt[:kv0], wt[az0:], wt[kv0:ag0]], axis=0).astype(BF16)
    w_gate = jnp.pad(wt[ag0:az0], ((0, N_GATE_PAD - N_HEADS * N_ATT_GATES), (0, 0))).astype(BF16)
    eye = jnp.eye(N_KV_HEADS, dtype=F32)
    w_cmp = [jnp.einsum("lde,kh->lkdhe", w_phi[l, c], eye).reshape(BLOCK, KV_WIDTH, KV_WIDTH).astype(BF16)
             for c in range(2)]
    pe = [jnp.broadcast_to(pe_cmp[l, c][:, None, :], (BLOCK, N_KV_HEADS, HEAD_DIM)).reshape(BLOCK, KV_WIDTH)
          for c in range(2)]
    tab, bmr, bmi, cm = _ssm_tables(lam_re[l], lam_im[l], log_step[l], b_re[l], b_im[l], c_re[l], c_im[l])
    return dict(
        g_pre=g_pre[l].reshape(1, D_MODEL), g_post=g_post[l].reshape(1, D_MODEL),
        w_main=w_main, w_gate=w_gate, w_pool=w_pool[l].astype(BF16), pool_scale=pool_scale[l].reshape(1, POOL_WIDTH),
        w_cmp=w_cmp, pe=pe, tab=tab, bmr=bmr, bmi=bmi, cm=cm, d_skip=d_skip[l].reshape(1, SSM_WIDTH),
        w_glu=w_glu[l].astype(BF16), w_br_pool=w_br_pool[l].astype(BF16), w_br_nsa=w_br_nsa[l].astype(BF16),
        w_br_ssm=w_br_ssm[l].astype(BF16), w_out=w_out[l].astype(BF16))


def _layer(x3, lw, *, pos0, s_valid, pool_prefix, ssm_h0, win_prefix, paged):
    b, s, _ = x3.shape
    x2 = x3.reshape(b * s, D_MODEL)
    proj2 = _inproj(x2, lw["g_pre"], lw["w_main"], 768)
    gates3 = _inproj(x2, lw["g_pre"], lw["w_gate"], N_GATE_PAD).reshape(b, s, N_GATE_PAD)
    proj3 = proj2.reshape(b, s, N_MAIN)

    prefix16 = jnp.pad(pool_prefix, ((0, 0), (1, 0), (0, 0)))
    a_pool, pool_tail = _pool(proj3, prefix16, lw["w_pool"], lw["pool_scale"], q0=pos0, s_valid=s_valid)
    pool_state = pool_tail[:, 1:]

    kvn = proj3[:, :, C_KV:]
    if paged is None:
        tq = min(s, 256)
        n_cmp = nb_total = nbp = s // BLOCK
        x3k = x3v = proj3.reshape(b * n_cmp, BLOCK, N_MAIN)
        kcol, vcol = KV_COL, KV_COL + 1
    else:
        tq = s
        cache5, layer, page_table = paged
        past_len = page_table.shape[1] * PAGE_SIZE
        n_cmp = past_len // BLOCK
        nb_total = -(-(past_len + s_valid) // BLOCK)
        nbp = -(-nb_total // SUBLANES) * SUBLANES
    if paged is None:
        kc = _compress(x3k, kcol, lw["pe"][0], lw["w_cmp"][0])
        vc = _compress(x3v, vcol, lw["pe"][1], lw["w_cmp"][1])
    else:
        pe2 = jnp.stack([jnp.tile(pe, (BLOCKS_PER_PAGE, 1)) for pe in lw["pe"]])
        xk_rows, xv_rows = _gather_cmp(cache5, layer, page_table, pe2)
        kc = _compress_rows(xk_rows, lw["w_cmp"][0])
        vc = _compress_rows(xv_rows, lw["w_cmp"][1])
    kc = kc.reshape(b, n_cmp, KV_WIDTH)
    vc = vc.reshape(b, n_cmp, KV_WIDTH)
    kc = jnp.pad(kc, ((0, 0), (0, nbp - n_cmp), (0, 0)))
    vct = jnp.swapaxes(jnp.pad(vc, ((0, 0), (0, nbp - n_cmp), (0, 0))), 1, 2)
    o_cmp, sel = _cmpsel(proj3, kc, vct, pos0=pos0, tq=tq, nb_total=nb_total)
    if paged is None:
        assert pos0 == 0
        o_sel = _selattn(proj3, _tcols(proj3, KV_COL + 3), sel, tq=tq)
    else:
        o_sel = _selattn_paged(proj3, cache5, layer, page_table, sel)

    tq_win = min(s, 256)
    n_rows = ((s - tq_win) // KEY_TILE + -(-(WINDOW + tq_win) // KEY_TILE)) * KEY_TILE
    pad_rows = n_rows - WINDOW - s
    k_new, v_new = kvn[:, :, 4 * KV_WIDTH:5 * KV_WIDTH], kvn[:, :, 5 * KV_WIDTH:6 * KV_WIDTH]
    v_new_t = _tcols(proj3, KV_COL + 5)
    if win_prefix is None:
        kwin = jnp.pad(k_new, ((0, 0), (WINDOW, pad_rows), (0, 0)))
        vwin_t = jnp.pad(v_new_t, ((0, 0), (0, 0), (WINDOW, pad_rows)))
    else:
        k_pre = win_prefix[:, :, 0].reshape(b, WINDOW, KV_WIDTH)
        v_pre = win_prefix[:, :, 1].reshape(b, WINDOW, KV_WIDTH)
        kwin = jnp.pad(jnp.concatenate([k_pre, k_new], axis=1), ((0, 0), (0, pad_rows), (0, 0)))
        vwin_t = jnp.pad(jnp.concatenate([jnp.swapaxes(v_pre, 1, 2), v_new_t], axis=2),
                         ((0, 0), (0, 0), (0, pad_rows)))
    o_win = _winattn(proj3, kwin, vwin_t, pos0=pos0, row_pos0=pos0 - WINDOW, tq=tq_win)
    a_att = _attcomb(o_cmp, o_sel, o_win, gates3, proj3)

    a_ssm, ssm_state = _ssm(proj3, ssm_h0.reshape(b, 2, SSM_CH), lw["tab"], lw["bmr"], lw["bmi"], lw["cm"],
                            lw["d_skip"], lw["w_glu"], s_valid=s_valid)

    merged = _merge(a_pool.reshape(b * s, POOL_WIDTH), a_att.reshape(b * s, ATT_WIDTH),
                    a_ssm.reshape(b * s, SSM_WIDTH), lw["w_br_pool"], lw["w_br_nsa"], lw["w_br_ssm"], proj2)
    x_new = _outproj(merged, lw["w_out"], lw["g_post"], x2).reshape(b, s, D_MODEL)

    kv_rows = kvn[:, :s_valid, :4 * KV_WIDTH].reshape(b, s_valid, 4, N_KV_HEADS, HEAD_DIM)
    if win_prefix is None:
        assert s_valid >= WINDOW
        win_k, win_v = k_new[:, s_valid - WINDOW:s_valid], v_new[:, s_valid - WINDOW:s_valid]
    else:
        win_k = jnp.concatenate([k_pre, k_new], axis=1)[:, s_valid:s_valid + WINDOW]
        win_v = jnp.concatenate([v_pre, v_new], axis=1)[:, s_valid:s_valid + WINDOW]
    win_state = jnp.stack([win_k.reshape(b, WINDOW, N_KV_HEADS, HEAD_DIM),
                           win_v.reshape(b, WINDOW, N_KV_HEADS, HEAD_DIM)], axis=2)
    return x_new, kv_rows, win_state, pool_state, ssm_state.reshape(b, 2, SSM_GROUPS, SSM_STATE)


def kernel(x_prompt, x_sample, cache_kv, page_table, state_win_kv, state_pool, state_ssm, g_pre, g_post, w_in, w_pool, pool_scale, pe_cmp, w_phi, lam_re, lam_im, log_step, b_re, b_im, c_re, c_im, d_skip, w_glu, w_br_pool, w_br_nsa, w_br_ssm, w_out):
    depth = w_in.shape[0]
    bp, sp, _ = x_prompt.shape
    bd, sd, _ = x_sample.shape
    n_pool = cache_kv.shape[1]
    past_len = page_table.shape[1] * PAGE_SIZE
    assert state_win_kv.shape[2] == WINDOW and past_len >= WINDOW and sp >= WINDOW
    cache5 = jnp.transpose(cache_kv, (0, 1, 3, 4, 5, 2)).reshape(depth, n_pool, 4, KV_WIDTH, PAGE_SIZE)
    sd_pad = -(-sd // SUBLANES) * SUBLANES
    yp = x_prompt
    ys = jnp.pad(x_sample, ((0, 0), (0, sd_pad - sd), (0, 0)))
    zeros_pool = jnp.zeros((bp, POOL_STATE, POOL_WIDTH), F32)
    zeros_ssm = jnp.zeros((bp, 2, SSM_GROUPS, SSM_STATE), F32)
    outs_p, outs_s = [], []
    for l in range(depth):
        lw = _prep_layer_weights(l, g_pre, g_post, w_in, w_pool, pool_scale, pe_cmp, w_phi, lam_re, lam_im, log_step,
                                 b_re, b_im, c_re, c_im, d_skip, w_glu, w_br_pool, w_br_nsa, w_br_ssm, w_out)
        yp, *rp = _layer(yp, lw, pos0=0, s_valid=sp, pool_prefix=zeros_pool, ssm_h0=zeros_ssm,
                         win_prefix=None, paged=None)
        ys, *rs = _layer(ys, lw, pos0=past_len, s_valid=sd, pool_prefix=state_pool[l], ssm_h0=state_ssm[l],
                         win_prefix=state_win_kv[l], paged=(cache5, l, page_table))
        outs_p.append(rp)
        outs_s.append(rs)
    stack = lambda outs, i: jnp.stack([o[i] for o in outs])
    return (yp, ys[:, :sd], stack(outs_p, 0), stack(outs_s, 0), stack(outs_p, 1), stack(outs_s, 1),
            stack(outs_p, 2), stack(outs_s, 2), stack(outs_p, 3), stack(outs_s, 3))
```

```python
import functools

import jax
import jax.numpy as jnp
from jax import lax
from jax.experimental import pallas as pl
from jax.experimental.pallas import tpu as pltpu

F32 = jnp.float32
BF16 = jnp.bfloat16
I32 = jnp.int32

D_MODEL = 2048
PAGE_SIZE = 128
POOL_WIDTH = D_MODEL // 2
POOL_WINDOWS = (2, 4, 8, 16)
POOL_GROUP_DIM = POOL_WIDTH // len(POOL_WINDOWS)
POOL_STATE = max(POOL_WINDOWS) - 1
N_HEADS = 16
HEAD_DIM = 64
N_KV_HEADS = 4
GQA = N_HEADS // N_KV_HEADS
ATT_WIDTH = N_HEADS * HEAD_DIM
KV_WIDTH = N_KV_HEADS * HEAD_DIM
BLOCK = 64
TOP_N = 16
WINDOW = 512
N_ATT_GATES = 3
SSM_WIDTH = D_MODEL // 2
SSM_GROUP_DIM = 16
SSM_GROUPS = SSM_WIDTH // SSM_GROUP_DIM
SSM_STATE = 64
SSM_CH = SSM_GROUPS * SSM_STATE
N_BRANCH = 3
EPS = 1e-6
NEG = -1e30
FORCE = 1e4
ATT_SCALE = HEAD_DIM ** -0.5
LOG2_E = 1.4426950408889634

C_PU, C_PZ, C_Q, C_AZ, C_SU, C_SZ = (i * 1024 for i in range(6))
C_MG = 6 * 1024
C_KV = C_MG + N_BRANCH * D_MODEL
N_MAIN = C_KV + 6 * KV_WIDTH
N_GATE_PAD = 128
KV_COL = C_KV // KV_WIDTH

VMEM_LIMIT_BYTES = 52 * 1024 * 1024
SUBLANES = 8
KEY_TILE = 256
BLOCKS_PER_TILE = KEY_TILE // BLOCK
BLOCKS_PER_PAGE = PAGE_SIZE // BLOCK
PAGES_PER_STEP = 16
SSM_LANE_CHUNK = 1024

NT_DIMS = (((1,), (1,)), ((), ()))


def _params(*sem):
    return pltpu.CompilerParams(dimension_semantics=sem, vmem_limit_bytes=VMEM_LIMIT_BYTES)


def _silu(x):
    return x * jax.nn.sigmoid(x)


def _inproj_kernel(x_ref, g_ref, w_ref, o_ref, h_ref):
    @pl.when(pl.program_id(1) == 0)
    def _():
        x = x_ref[...]
        r = lax.rsqrt(jnp.mean(x * x, axis=-1, keepdims=True) + EPS)
        h_ref[...] = (x * r * g_ref[...]).astype(BF16)

    o_ref[...] = lax.dot_general(h_ref[...], w_ref[0].astype(BF16), NT_DIMS, preferred_element_type=F32)


def _inproj(x2d, g, wt_all, layer, n_out, tn, src_row):
    m, d = x2d.shape
    tm = min(m, 1024)
    w_spec = pl.BlockSpec((pl.Element(1), pl.Element(tn), pl.Element(d)),
                          lambda i, j: (layer, pl.multiple_of(src_row(j), SUBLANES), 0))
    return pl.pallas_call(
        _inproj_kernel,
        grid=(m // tm, n_out // tn),
        in_specs=[pl.BlockSpec((tm, d), lambda i, j: (i, 0)),
                  pl.BlockSpec((1, d), lambda i, j: (0, 0)),
                  w_spec],
        out_specs=pl.BlockSpec((tm, tn), lambda i, j: (i, j)),
        out_shape=jax.ShapeDtypeStruct((m, n_out), F32),
        scratch_shapes=[pltpu.VMEM((tm, d), BF16)],
        compiler_params=_params("parallel", "arbitrary"),
        name="inproj",
    )(x2d, g, wt_all)


_ROW_KV = 2 * POOL_WIDTH + ATT_WIDTH
_ROW_AG = _ROW_KV + 6 * KV_WIDTH
_ROW_AZ = _ROW_AG + N_HEADS * N_ATT_GATES
INPROJ_TILE = 768


def _main_src_row(j):
    n_head = _ROW_KV // INPROJ_TILE
    n_mid = (C_KV - _ROW_KV) // INPROJ_TILE
    return jnp.where(j < n_head, j * INPROJ_TILE,
                     jnp.where(j < n_head + n_mid, _ROW_AZ + (j - n_head) * INPROJ_TILE,
                               _ROW_KV + (j - n_head - n_mid) * INPROJ_TILE))


def _pool_kernel(pu_ref, pz_ref, pre_ref, wp_ref, sc_ref, o_ref, st_ref, e_ref, *, tt, q0, rows_last):
    t = pl.program_id(1)
    hist = POOL_STATE + 1

    @pl.when(t == 0)
    def _():
        e_ref[0:hist, :] = pre_ref[0]

    @pl.when(t > 0)
    def _():
        e_ref[0:hist, :] = e_ref[tt:tt + hist, :]

    u = pu_ref[0]
    e_ref[hist:hist + tt, :] = u
    pos = q0 + t * tt + lax.broadcasted_iota(I32, (tt, 1), 0)
    ys = []
    for gi, w in enumerate(POOL_WINDOWS):
        lo = gi * POOL_GROUP_DIM
        tot = e_ref[hist:hist + tt, lo:lo + POOL_GROUP_DIM]
        for k in range(1, w):
            tot = tot + e_ref[hist - k:hist - k + tt, lo:lo + POOL_GROUP_DIM]
        cnt = jnp.minimum(pos + 1, w).astype(F32)
        diff = tot / cnt - u[:, lo:lo + POOL_GROUP_DIM]
        ys.append(jnp.dot(diff.astype(BF16), wp_ref[gi], preferred_element_type=F32))
    y = jnp.concatenate(ys, axis=-1) * sc_ref[...]
    o_ref[0] = (y * _silu(pz_ref[0])).astype(BF16)

    @pl.when(t == pl.num_programs(1) - 1)
    def _():
        st_ref[0] = e_ref[rows_last:rows_last + hist, :]


def _pool(proj3, prefix16, w_pool, pool_scale, *, q0, s_valid):
    b, s, _ = proj3.shape
    tt = min(s, 512)
    hist = POOL_STATE + 1
    rows_last = ((s_valid - 1) % tt) + 1
    kern = functools.partial(_pool_kernel, tt=tt, q0=q0, rows_last=rows_last)
    return pl.pallas_call(
        kern,
        grid=(b, s // tt),
        in_specs=[pl.BlockSpec((1, tt, POOL_WIDTH), lambda i, t: (i, t, C_PU // POOL_WIDTH)),
                  pl.BlockSpec((1, tt, POOL_WIDTH), lambda i, t: (i, t, C_PZ // POOL_WIDTH)),
                  pl.BlockSpec((1, hist, POOL_WIDTH), lambda i, t: (i, 0, 0)),
                  pl.BlockSpec((len(POOL_WINDOWS), POOL_GROUP_DIM, POOL_GROUP_DIM), lambda i, t: (0, 0, 0)),
                  pl.BlockSpec((1, POOL_WIDTH), lambda i, t: (0, 0))],
        out_specs=[pl.BlockSpec((1, tt, POOL_WIDTH), lambda i, t: (i, t, 0)),
                   pl.BlockSpec((1, hist, POOL_WIDTH), lambda i, t: (i, 0, 0))],
        out_shape=[jax.ShapeDtypeStruct((b, s, POOL_WIDTH), BF16),
                   jax.ShapeDtypeStruct((b, hist, POOL_WIDTH), F32)],
        scratch_shapes=[pltpu.VMEM((hist + tt, POOL_WIDTH), F32)],
        compiler_params=_params("parallel", "arbitrary"),
        name="pool",
    )(proj3, proj3, prefix16, w_pool, pool_scale)


def _compress_kernel(x_ref, pe_ref, w_ref, o_ref):
    def one_row(l, acc):
        x = x_ref[:, l, :] + pe_ref[pl.ds(l, 1), :]
        return acc + jnp.dot(x.astype(BF16), w_ref[l], preferred_element_type=F32)

    o_ref[...] = lax.fori_loop(0, BLOCK, one_row, jnp.zeros(o_ref.shape, F32))


def _compress(x3, col, pe, w):
    nb = x3.shape[0]
    tm = min(nb, 128)
    return pl.pallas_call(
        _compress_kernel,
        grid=(nb // tm,),
        in_specs=[pl.BlockSpec((tm, BLOCK, KV_WIDTH), lambda i: (i, 0, col)),
                  pl.BlockSpec((BLOCK, KV_WIDTH), lambda i: (0, 0)),
                  pl.BlockSpec((BLOCK, KV_WIDTH, KV_WIDTH), lambda i: (0, 0, 0))],
        out_specs=pl.BlockSpec((tm, KV_WIDTH), lambda i: (i, 0)),
        out_shape=jax.ShapeDtypeStruct((nb, KV_WIDTH), F32),
        compiler_params=_params("parallel"),
        name="compress",
    )(x3, pe, w)


def _compress_rows_kernel(x_ref, w_ref, o_ref):
    x = jnp.concatenate([x_ref[l] for l in range(BLOCK)], axis=1)
    o_ref[...] = jnp.dot(x, w_ref[...].reshape(BLOCK * KV_WIDTH, KV_WIDTH), preferred_element_type=F32)


def _compress_rows(xl, w):
    nb = xl.shape[1]
    tm = min(nb, 256)
    return pl.pallas_call(
        _compress_rows_kernel,
        grid=(nb // tm,),
        in_specs=[pl.BlockSpec((BLOCK, tm, KV_WIDTH), lambda i: (0, i, 0)),
                  pl.BlockSpec((BLOCK, KV_WIDTH, KV_WIDTH), lambda i: (0, 0, 0))],
        out_specs=pl.BlockSpec((tm, KV_WIDTH), lambda i: (i, 0)),
        out_shape=jax.ShapeDtypeStruct((nb, KV_WIDTH), F32),
        compiler_params=_params("parallel"),
        name="compress_rows",
    )(xl, w)


def _page_specs(layer, n_pages, comp_block):
    def spec(i):
        def index(bi, st, pt):
            return (layer, pt[bi, jnp.minimum(st * PAGES_PER_STEP + i, n_pages - 1)], comp_block, 0, 0)
        return pl.BlockSpec((1, 1, 2, KV_WIDTH, PAGE_SIZE), index)
    return [spec(i) for i in range(PAGES_PER_STEP)]


def _sublane_transpose8(a):
    sub = lax.broadcasted_iota(I32, a[0].shape, 0)
    for shift in (4, 2, 1):
        low = (sub & shift) == 0
        nxt = list(a)
        for j in range(SUBLANES):
            if j & shift == 0:
                nxt[j] = jnp.where(low, a[j], pltpu.roll(a[j + shift], shift, axis=0))
                nxt[j + shift] = jnp.where(low, pltpu.roll(a[j], SUBLANES - shift, axis=0), a[j + shift])
        a = nxt
    return a


def _gather_cmp_kernel(pt_ref, *refs):
    del pt_ref
    pages, (pe_ref, xk_ref, xv_ref) = refs[:PAGES_PER_STEP], refs[PAGES_PER_STEP:]
    n_groups = PAGES_PER_STEP * BLOCKS_PER_PAGE // SUBLANES
    for comp, dst in enumerate((xk_ref, xv_ref)):
        rows = [page[0, 0, comp].T + pe_ref[comp] for page in pages]
        for m in range(BLOCK // SUBLANES):
            per_group = []
            for grp in range(n_groups):
                pieces = []
                for jj in range(SUBLANES):
                    j = grp * SUBLANES + jj
                    lo = (j % BLOCKS_PER_PAGE) * BLOCK + m * SUBLANES
                    pieces.append(rows[j // BLOCKS_PER_PAGE][lo:lo + SUBLANES, :])
                per_group.append(_sublane_transpose8(pieces))
            for s in range(SUBLANES):
                dst[m * SUBLANES + s] = jnp.concatenate([g[s] for g in per_group], axis=0).astype(BF16)


def _gather_cmp(cache5, layer, page_table, pe2):
    b, n_pages = page_table.shape
    n_steps = n_pages // PAGES_PER_STEP
    blocks_per_step = PAGES_PER_STEP * BLOCKS_PER_PAGE
    assert blocks_per_step % (2 * SUBLANES) == 0
    out_spec = pl.BlockSpec((BLOCK, blocks_per_step, KV_WIDTH), lambda bi, st, pt: (0, bi * n_steps + st, 0))
    out = jax.ShapeDtypeStruct((BLOCK, b * n_pages * BLOCKS_PER_PAGE, KV_WIDTH), BF16)
    return pl.pallas_call(
        _gather_cmp_kernel,
        grid_spec=pltpu.PrefetchScalarGridSpec(
            num_scalar_prefetch=1, grid=(b, n_steps),
            in_specs=_page_specs(layer, n_pages, 0) + [
                pl.BlockSpec((2, PAGE_SIZE, KV_WIDTH), lambda bi, st, pt: (0, 0, 0))],
            out_specs=[out_spec, out_spec]),
        out_shape=[out, out],
        compiler_params=_params("parallel", "arbitrary"),
        name="gather_cmp",
    )(page_table, *([cache5] * PAGES_PER_STEP), pe2)


def _cmpsel_kernel(q_ref, kc_ref, vct_ref, ocmp_ref, sel_ref, s_ref, *, tq, nbp, pos0, n_sel):
    j = pl.program_id(1)
    qs = (q_ref[0] * ATT_SCALE).astype(BF16)
    kc = kc_ref[0].astype(BF16)
    vct = vct_ref[0].astype(BF16)
    nidx = lax.broadcasted_iota(I32, (nbp, tq), 0)
    pos = pos0 + j * tq + lax.broadcasted_iota(I32, (nbp, tq), 1)
    cvalid = nidx < lax.shift_right_logical(pos + 1, 6)
    cur = lax.shift_right_logical(pos, 6)
    forced = (nidx == 0) | (nidx == cur) | (nidx == cur - 1)
    outs = []
    for k in range(N_KV_HEADS):
        ksl = slice(k * HEAD_DIM, (k + 1) * HEAD_DIM)
        imp = jnp.zeros((nbp, tq), F32)
        for g in range(GQA):
            h = k * GQA + g
            sc = lax.dot_general(kc[:, ksl], qs[:, h * HEAD_DIM:(h + 1) * HEAD_DIM], NT_DIMS,
                                 preferred_element_type=F32)
            sc = jnp.where(cvalid, sc, NEG)
            e = jnp.exp(sc - jnp.max(sc, axis=0, keepdims=True))
            pc = jnp.where(cvalid, e / jnp.sum(e, axis=0, keepdims=True), 0.0)
            imp = imp + pc
            outs.append(jnp.dot(vct[ksl, :], pc.astype(BF16), preferred_element_type=F32))
        s_ref[:, k * tq:(k + 1) * tq] = jnp.where(forced, FORCE, jnp.where(nidx < cur, imp, NEG))
    ocmp_ref[0] = jnp.concatenate(outs, axis=0)

    lanes = N_KV_HEADS * tq
    chunk = min(lanes, 128)
    rows = lax.broadcasted_iota(I32, (nbp, chunk), 0)
    for c in range(lanes // chunk):
        score = s_ref[:, c * chunk:(c + 1) * chunk]
        work, sel = score, jnp.zeros_like(score)
        for _ in range(n_sel):
            top = jnp.max(work, axis=0, keepdims=True)
            first = jnp.min(jnp.where(work == top, rows, nbp), axis=0, keepdims=True)
            hit = rows == first
            work, sel = jnp.where(hit, -jnp.inf, work), jnp.where(hit, 1.0, sel)
        sel_ref[0, :, c * chunk:(c + 1) * chunk] = jnp.where(score > NEG / 2, sel, 0.0)


def _cmpsel(proj3, kc, vct, *, pos0, tq, nb_total):
    b, s, _ = proj3.shape
    nbp = kc.shape[1]
    kern = functools.partial(_cmpsel_kernel, tq=tq, nbp=nbp, pos0=pos0, n_sel=min(TOP_N, nb_total))
    lanes = N_KV_HEADS * tq
    return pl.pallas_call(
        kern,
        grid=(b, s // tq),
        in_specs=[pl.BlockSpec((1, tq, ATT_WIDTH), lambda i, j: (i, j, C_Q // ATT_WIDTH)),
                  pl.BlockSpec((1, nbp, KV_WIDTH), lambda i, j: (i, 0, 0)),
                  pl.BlockSpec((1, KV_WIDTH, nbp), lambda i, j: (i, 0, 0))],
        out_specs=[pl.BlockSpec((1, ATT_WIDTH, tq), lambda i, j: (i, 0, j)),
                   pl.BlockSpec((1, nbp, lanes), lambda i, j: (i, 0, j))],
        out_shape=[jax.ShapeDtypeStruct((b, ATT_WIDTH, s), F32),
                   jax.ShapeDtypeStruct((b, nbp, N_KV_HEADS * s), F32)],
        scratch_shapes=[pltpu.VMEM((nbp, lanes), F32)],
        compiler_params=_params("parallel", "parallel"),
        name="cmpsel",
    )(proj3, kc, vct)


def _attn_init(q_ref, qs_ref, m_ref, acc_ref, tq):
    qs = (q_ref[0] * (ATT_SCALE * LOG2_E)).astype(BF16)
    for k in range(N_KV_HEADS):
        for g in range(GQA):
            h = k * GQA + g
            qs_ref[k, g * tq:(g + 1) * tq, :] = qs[:, h * HEAD_DIM:(h + 1) * HEAD_DIM]
    m_ref[...] = jnp.full_like(m_ref, NEG)
    acc_ref[...] = jnp.zeros_like(acc_ref)


def _attn_step(k, lanes, k_tile, vt_tile, block_rows, ok, qs_ref, m_ref, acc_ref):
    ksl = slice(k * HEAD_DIM, (k + 1) * HEAD_DIM)
    s = lax.dot_general(k_tile[:, ksl], qs_ref[k, lanes, :], NT_DIMS, preferred_element_type=F32)
    if block_rows is not None:
        s = jnp.concatenate([jnp.where(row > 0.5, s[i * BLOCK:(i + 1) * BLOCK], NEG)
                             for i, row in enumerate(block_rows)], axis=0)
    if ok is not None:
        s = jnp.where(ok, s, NEG)
    m_old = m_ref[k, :, lanes]
    m_new = jnp.maximum(m_old, jnp.max(s, axis=0, keepdims=True))
    p = jnp.exp2(s - m_new)
    alpha = jnp.exp2(m_old - m_new)
    m_ref[k, :, lanes] = m_new
    vt_ones = jnp.concatenate([vt_tile[ksl, :], jnp.ones((ONES_ROWS, vt_tile.shape[1]), BF16)], axis=0)
    acc_ref[k, :, lanes] = (alpha * acc_ref[k, :, lanes]
                            + jnp.dot(vt_ones, p.astype(BF16), preferred_element_type=F32))


def _attn_finish(o_ref, acc_ref, tq):
    outs = []
    for k in range(N_KV_HEADS):
        o = acc_ref[k, 0:HEAD_DIM, :] / acc_ref[k, HEAD_DIM:HEAD_DIM + 1, :]
        for g in range(GQA):
            outs.append(o[:, g * tq:(g + 1) * tq])
    o_ref[0] = jnp.concatenate(outs, axis=0)


ONES_ROWS = 16


def _attn_scratch(tq):
    r = GQA * tq
    return [pltpu.VMEM((N_KV_HEADS, r, HEAD_DIM), BF16),
            pltpu.VMEM((N_KV_HEADS, 1, r), F32),
            pltpu.VMEM((N_KV_HEADS, HEAD_DIM + ONES_ROWS, r), F32)]


def _lane_query_pos(shape, tq, first):
    return first + (lax.broadcasted_iota(I32, shape, 1) & (tq - 1))


def _selattn_kernel(jt_ref, kt_ref, q_ref, k_ref, vt_ref, sel_ref, o_ref, qs_ref, m_ref, acc_ref, *, tq):
    step = pl.program_id(1)
    j = jt_ref[step]
    kt = kt_ref[step]

    @pl.when(kt == 0)
    def _():
        _attn_init(q_ref, qs_ref, m_ref, acc_ref, tq)

    def key_tile(diagonal):
        k_tile = k_ref[0].astype(BF16)
        vt_tile = vt_ref[0].astype(BF16)
        r = GQA * tq
        causal = None
        if diagonal:
            causal = lax.broadcasted_iota(I32, (KEY_TILE, r), 0) <= _lane_query_pos((KEY_TILE, r), tq, 0)
        for k in range(N_KV_HEADS):
            rows = [jnp.concatenate([sel_ref[0, pl.ds(kt * BLOCKS_PER_TILE + i, 1), k * tq:(k + 1) * tq]] * GQA,
                                    axis=1) for i in range(BLOCKS_PER_TILE)]
            _attn_step(k, slice(0, r), k_tile, vt_tile, rows, causal, qs_ref, m_ref, acc_ref)

    @pl.when(kt < j)
    def _():
        key_tile(False)

    @pl.when(kt == j)
    def _():
        key_tile(True)
        _attn_finish(o_ref, acc_ref, tq)


def _selattn(proj3, vt, sel, *, tq):
    b, s, _ = proj3.shape
    assert tq == KEY_TILE
    nbp = sel.shape[1]
    pairs = [(j, kt) for j in range(s // tq) for kt in range(j + 1)]
    jt = jnp.asarray([p[0] for p in pairs], I32)
    ktt = jnp.asarray([p[1] for p in pairs], I32)
    kern = functools.partial(_selattn_kernel, tq=tq)
    return pl.pallas_call(
        kern,
        grid_spec=pltpu.PrefetchScalarGridSpec(
            num_scalar_prefetch=2, grid=(b, len(pairs)),
            in_specs=[pl.BlockSpec((1, tq, ATT_WIDTH), lambda i, p, jt, kt: (i, jt[p], C_Q // ATT_WIDTH)),
                      pl.BlockSpec((1, KEY_TILE, KV_WIDTH), lambda i, p, jt, kt: (i, kt[p], KV_COL + 2)),
                      pl.BlockSpec((1, KV_WIDTH, KEY_TILE), lambda i, p, jt, kt: (i, 0, kt[p])),
                      pl.BlockSpec((1, nbp, N_KV_HEADS * tq), lambda i, p, jt, kt: (i, 0, jt[p]))],
            out_specs=pl.BlockSpec((1, ATT_WIDTH, tq), lambda i, p, jt, kt: (i, 0, jt[p])),
            scratch_shapes=_attn_scratch(tq)),
        out_shape=jax.ShapeDtypeStruct((b, ATT_WIDTH, s), F32),
        compiler_params=_params("parallel", "arbitrary"),
        name="selattn",
    )(jt, ktt, proj3, proj3, vt, sel)


def _winattn_kernel(q_ref, k_ref, vt_ref, o_ref, qs_ref, m_ref, acc_ref, *, tq, pos0, row_pos0, n_sub):
    j = pl.program_id(1)
    t = pl.program_id(2)
    r = GQA * tq

    @pl.when(t == 0)
    def _():
        _attn_init(q_ref, qs_ref, m_ref, acc_ref, tq)

    tile = (j * tq) // KEY_TILE + (n_sub - 1 - t)
    kp = row_pos0 + tile * KEY_TILE + lax.broadcasted_iota(I32, (KEY_TILE, r), 0)
    pq = _lane_query_pos((KEY_TILE, r), tq, pos0 + j * tq)
    ok = (kp >= 0) & (kp <= pq) & (kp > pq - WINDOW)
    k_tile = k_ref[0].astype(BF16)
    vt_tile = vt_ref[0].astype(BF16)
    for k in range(N_KV_HEADS):
        _attn_step(k, slice(0, r), k_tile, vt_tile, None, ok, qs_ref, m_ref, acc_ref)

    @pl.when(t == n_sub - 1)
    def _():
        _attn_finish(o_ref, acc_ref, tq)


def _winattn(proj3, kwin, vwin_t, *, pos0, row_pos0, tq):
    b, s, _ = proj3.shape
    assert tq & (tq - 1) == 0 and (tq % KEY_TILE == 0 or s == tq)
    n_sub = -(-(WINDOW + tq) // KEY_TILE)
    assert kwin.shape[1] >= ((s - tq) // KEY_TILE + n_sub) * KEY_TILE

    def tile(j, t):
        return (j * tq) // KEY_TILE + (n_sub - 1 - t)

    kern = functools.partial(_winattn_kernel, tq=tq, pos0=pos0, row_pos0=row_pos0, n_sub=n_sub)
    return pl.pallas_call(
        kern,
        grid=(b, s // tq, n_sub),
        in_specs=[pl.BlockSpec((1, tq, ATT_WIDTH), lambda i, j, t: (i, j, C_Q // ATT_WIDTH)),
                  pl.BlockSpec((1, KEY_TILE, KV_WIDTH), lambda i, j, t: (i, tile(j, t), 0)),
                  pl.BlockSpec((1, KV_WIDTH, KEY_TILE), lambda i, j, t: (i, 0, tile(j, t)))],
        out_specs=pl.BlockSpec((1, ATT_WIDTH, tq), lambda i, j, t: (i, 0, j)),
        out_shape=jax.ShapeDtypeStruct((b, ATT_WIDTH, s), F32),
        scratch_shapes=_attn_scratch(tq),
        compiler_params=_params("parallel", "parallel", "arbitrary"),
        name="winattn",
    )(proj3, kwin, vwin_t)


def _selattn_paged_kernel(pt_ref, *refs, tq, n_steps):
    del pt_ref
    pages = refs[:PAGES_PER_STEP]
    q_ref, new_ref, sel_ref, o_ref, qbd_ref, selr_ref, m_ref, l_ref, acc_ref = refs[PAGES_PER_STEP:]
    st = pl.program_id(1)
    rows = N_HEADS * tq

    def flash(s, ok, v, v_is_transposed):
        s = jnp.where(ok, s, NEG)
        m_old = m_ref[...]
        m_new = jnp.maximum(m_old, jnp.max(s, axis=-1, keepdims=True))
        p = jnp.exp(s - m_new)
        alpha = jnp.exp(m_old - m_new)
        l_ref[...] = alpha * l_ref[...] + jnp.sum(p, axis=-1, keepdims=True)
        m_ref[...] = m_new
        if v_is_transposed:
            pv = lax.dot_general(p.astype(BF16), v, NT_DIMS, preferred_element_type=F32)
        else:
            pv = jnp.dot(p.astype(BF16), v, preferred_element_type=F32)
        acc_ref[...] = alpha * acc_ref[...] + pv

    @pl.when(st == 0)
    def _():
        qs = (q_ref[0] * ATT_SCALE).astype(BF16)
        qbd_ref[...] = jnp.zeros_like(qbd_ref)
        for k in range(N_KV_HEADS):
            for g in range(GQA):
                h = k * GQA + g
                qbd_ref[h * tq:(h + 1) * tq, k * HEAD_DIM:(k + 1) * HEAD_DIM] = qs[:, h * HEAD_DIM:(h + 1) * HEAD_DIM]
                selr_ref[:, h * tq:(h + 1) * tq] = sel_ref[0, :, k * tq:(k + 1) * tq]
        m_ref[...] = jnp.full_like(m_ref, NEG)
        l_ref[...] = jnp.zeros_like(l_ref)
        acc_ref[...] = jnp.zeros_like(acc_ref)

    @pl.when(st < n_steps)
    def _():
        blk = st * (PAGES_PER_STEP * BLOCKS_PER_PAGE)
        chosen = jnp.concatenate(
            [jnp.broadcast_to(selr_ref[pl.ds(blk + c, 1), :], (BLOCK, rows))
             for c in range(PAGES_PER_STEP * BLOCKS_PER_PAGE)], axis=0)
        k_t = jnp.concatenate([page[0, 0, 0] for page in pages], axis=1).astype(BF16)
        v_t = jnp.concatenate([page[0, 0, 1] for page in pages], axis=1).astype(BF16)
        s = jnp.dot(qbd_ref[...], k_t, preferred_element_type=F32)
        flash(s, chosen.T > 0.5, v_t, True)

    @pl.when(st == n_steps)
    def _():
        new = new_ref[0]
        k_new = new[:, 2 * KV_WIDTH:3 * KV_WIDTH].astype(BF16)
        v_new = new[:, 3 * KV_WIDTH:4 * KV_WIDTH].astype(BF16)
        s = lax.dot_general(qbd_ref[...], k_new, NT_DIMS, preferred_element_type=F32)
        blk = n_steps * PAGES_PER_STEP * BLOCKS_PER_PAGE
        chosen = jnp.broadcast_to(selr_ref[pl.ds(blk, 1), :], (tq, rows)).T
        causal = (lax.broadcasted_iota(I32, (rows, tq), 1)
                  <= (lax.broadcasted_iota(I32, (rows, tq), 0) & (tq - 1)))
        flash(s, (chosen > 0.5) & causal, v_new, False)
        o = acc_ref[...] / l_ref[...]
        outs = []
        for k in range(N_KV_HEADS):
            for g in range(GQA):
                h = k * GQA + g
                outs.append(o[h * tq:(h + 1) * tq, k * HEAD_DIM:(k + 1) * HEAD_DIM].T)
        o_ref[0] = jnp.concatenate(outs, axis=0)


def _selattn_paged(proj3, cache5, layer, page_table, sel):
    b, tq, _ = proj3.shape
    _, n_pages = page_table.shape
    assert tq & (tq - 1) == 0 and tq <= BLOCK and n_pages % PAGES_PER_STEP == 0
    n_steps = n_pages // PAGES_PER_STEP
    nbp = sel.shape[1]
    rows = N_HEADS * tq
    kern = functools.partial(_selattn_paged_kernel, tq=tq, n_steps=n_steps)
    return pl.pallas_call(
        kern,
        grid_spec=pltpu.PrefetchScalarGridSpec(
            num_scalar_prefetch=1, grid=(b, n_steps + 1),
            in_specs=_page_specs(layer, n_pages, 1) + [
                pl.BlockSpec((1, tq, ATT_WIDTH), lambda bi, st, pt: (bi, 0, C_Q // ATT_WIDTH)),
                pl.BlockSpec((1, tq, 4 * KV_WIDTH), lambda bi, st, pt: (bi, 0, C_KV // (4 * KV_WIDTH))),
                pl.BlockSpec((1, nbp, N_KV_HEADS * tq), lambda bi, st, pt: (bi, 0, 0))],
            out_specs=pl.BlockSpec((1, ATT_WIDTH, tq), lambda bi, st, pt: (bi, 0, 0)),
            scratch_shapes=[pltpu.VMEM((rows, KV_WIDTH), BF16), pltpu.VMEM((nbp, rows), F32),
                            pltpu.VMEM((rows, 1), F32), pltpu.VMEM((rows, 1), F32),
                            pltpu.VMEM((rows, KV_WIDTH), F32)]),
        out_shape=jax.ShapeDtypeStruct((b, ATT_WIDTH, tq), F32),
        compiler_params=_params("parallel", "arbitrary"),
        name="selattn_paged",
    )(page_table, *([cache5] * PAGES_PER_STEP), proj3, proj3, sel)


def _tcols_kernel(x_ref, o_ref):
    o_ref[0] = x_ref[0].T


def _tcols(proj3, col):
    b, s, _ = proj3.shape
    ts = min(s, 512)
    return pl.pallas_call(
        _tcols_kernel,
        grid=(b, s // ts),
        in_specs=[pl.BlockSpec((1, ts, KV_WIDTH), lambda i, t: (i, t, col))],
        out_specs=pl.BlockSpec((1, KV_WIDTH, ts), lambda i, t: (i, 0, t)),
        out_shape=jax.ShapeDtypeStruct((b, KV_WIDTH, s), F32),
        compiler_params=_params("parallel", "parallel"),
        name="tcols",
    )(proj3)


def _attcomb_kernel(oc_ref, os_ref, ow_ref, ag_ref, az_ref, o_ref):
    gate_t = jax.nn.sigmoid(ag_ref[0]).T
    oc, osel, ow = oc_ref[0], os_ref[0], ow_ref[0]
    outs = []
    for h in range(N_HEADS):
        sl = slice(h * HEAD_DIM, (h + 1) * HEAD_DIM)
        c = h * N_ATT_GATES
        outs.append(gate_t[c:c + 1, :] * oc[sl, :] + gate_t[c + 1:c + 2, :] * osel[sl, :]
                    + gate_t[c + 2:c + 3, :] * ow[sl, :])
    o_ref[0] = (jnp.concatenate(outs, axis=0).T * _silu(az_ref[0])).astype(BF16)


def _attcomb(ocmp_t, osel_t, owin_t, gates3, proj3):
    b, s, _ = proj3.shape
    tt = min(s, 256)
    att_t = pl.BlockSpec((1, ATT_WIDTH, tt), lambda i, t: (i, 0, t))
    return pl.pallas_call(
        _attcomb_kernel,
        grid=(b, s // tt),
        in_specs=[att_t, att_t, att_t,
                  pl.BlockSpec((1, tt, N_GATE_PAD), lambda i, t: (i, t, 0)),
                  pl.BlockSpec((1, tt, ATT_WIDTH), lambda i, t: (i, t, C_AZ // ATT_WIDTH))],
        out_specs=pl.BlockSpec((1, tt, ATT_WIDTH), lambda i, t: (i, t, 0)),
        out_shape=jax.ShapeDtypeStruct((b, s, ATT_WIDTH), BF16),
        compiler_params=_params("parallel", "parallel"),
        name="attcomb",
    )(ocmp_t, osel_t, owin_t, gates3, proj3)


def _ssm_disc_kernel(lr_ref, li_ref, ls_ref, brt_ref, bit_ref, pr_ref, pi_ref, bbr_ref, bbi_ref):
    lr, li = lr_ref[...], li_ref[...]
    dt = jnp.exp(ls_ref[...])
    mag = jnp.exp(lr * dt)
    ab_re, ab_im = mag * jnp.cos(li * dt), mag * jnp.sin(li * dt)
    den = lr * lr + li * li
    co_re = ((ab_re - 1.0) * lr + ab_im * li) / den
    co_im = (ab_im * lr - (ab_re - 1.0) * li) / den
    brt, bit = brt_ref[...], bit_ref[...]
    bbr_ref[...] = co_re[:, None, :] * brt - co_im[:, None, :] * bit
    bbi_ref[...] = co_re[:, None, :] * bit + co_im[:, None, :] * brt
    pr, pi = ab_re, ab_im
    pr_ref[0], pi_ref[0] = pr, pi
    for r in range(1, SUBLANES):
        pr, pi = pr * ab_re - pi * ab_im, pr * ab_im + pi * ab_re
        pr_ref[r], pi_ref[r] = pr, pi


def _ssm_disc(lam_re, lam_im, log_step, b_re, b_im):
    g, n = lam_re.shape
    brt = jnp.swapaxes(b_re, 1, 2)
    bit = jnp.swapaxes(b_im, 1, 2)
    pw = jax.ShapeDtypeStruct((SUBLANES, g, n), F32)
    bb = jax.ShapeDtypeStruct(brt.shape, F32)
    return pl.pallas_call(_ssm_disc_kernel, out_shape=[pw, pw, bb, bb], name="ssm_disc")(
        lam_re, lam_im, log_step.reshape(g, 1), brt, bit)


def _ssm_kernel(su_ref, sz_ref, h0_ref, tab_ref, bmr_ref, bmi_ref, cm_ref, ds_ref, wg_ref,
                o_ref, st_ref, hr_ref, hi_ref, c_ref, *, tt, row_last):
    t = pl.program_id(1)

    @pl.when(t == 0)
    def _():
        c_ref[...] = h0_ref[0]

    u = su_ref[0]
    ub = u.astype(BF16)
    n_mm = SSM_WIDTH // 128
    for j in range(n_mm):
        uj = ub[:, j * 128:(j + 1) * 128]
        hr_ref[:, j * 512:(j + 1) * 512] = jnp.dot(uj, bmr_ref[j], preferred_element_type=F32)
        hi_ref[:, j * 512:(j + 1) * 512] = jnp.dot(uj, bmi_ref[j], preferred_element_type=F32)

    for cc in range(SSM_CH // SSM_LANE_CHUNK):
        lanes = slice(cc * SSM_LANE_CHUNK, (cc + 1) * SSM_LANE_CHUNK)
        tabs = [(tab_ref[2 * i, :, lanes], tab_ref[2 * i + 1, :, lanes]) for i in range(4)]

        def group(gi, carry):
            cr, ci = carry
            rows = pl.ds(pl.multiple_of(gi * SUBLANES, SUBLANES), SUBLANES)
            xr, xi = hr_ref[rows, lanes], hi_ref[rows, lanes]
            for lvl, shift in enumerate((1, 2, 4)):
                ar, ai = tabs[lvl]
                sr, si = pltpu.roll(xr, shift, axis=0), pltpu.roll(xi, shift, axis=0)
                xr, xi = xr + (ar * sr - ai * si), xi + (ar * si + ai * sr)
            pr, pi = tabs[3]
            xr, xi = xr + (pr * cr - pi * ci), xi + (pr * ci + pi * cr)
            hr_ref[rows, lanes], hi_ref[rows, lanes] = xr, xi
            return xr[SUBLANES - 1:SUBLANES, :], xi[SUBLANES - 1:SUBLANES, :]

        cr, ci = lax.fori_loop(0, tt // SUBLANES, group, (c_ref[0:1, lanes], c_ref[1:2, lanes]))
        c_ref[0:1, lanes], c_ref[1:2, lanes] = cr, ci

    @pl.when(t == pl.num_programs(1) - 1)
    def _():
        st_ref[0, 0:1, :] = hr_ref[row_last:row_last + 1, :]
        st_ref[0, 1:2, :] = hi_ref[row_last:row_last + 1, :]

    ys = []
    for j in range(n_mm):
        hcat = jnp.concatenate([hr_ref[:, j * 512:(j + 1) * 512], hi_ref[:, j * 512:(j + 1) * 512]], axis=-1)
        ys.append(jnp.dot(hcat.astype(BF16), cm_ref[j], preferred_element_type=F32))
    y = jnp.concatenate(ys, axis=-1) + ds_ref[...] * u
    z = jax.nn.gelu(y)
    out = z * jax.nn.sigmoid(jnp.dot(z.astype(BF16), wg_ref[...], preferred_element_type=F32))
    o_ref[0] = (out * _silu(sz_ref[0])).astype(BF16)


def _ssm(proj3, h0, tab, bmr, bmi, cm, d_skip, w_glu, *, s_valid):
    b, s, _ = proj3.shape
    tt = min(s, 256)
    row_last = (s_valid - 1) % tt
    n_mm = SSM_WIDTH // 128
    kern = functools.partial(_ssm_kernel, tt=tt, row_last=row_last)
    const3 = lambda i, t: (0, 0, 0)
    return pl.pallas_call(
        kern,
        grid=(b, s // tt),
        in_specs=[pl.BlockSpec((1, tt, SSM_WIDTH), lambda i, t: (i, t, C_SU // SSM_WIDTH)),
                  pl.BlockSpec((1, tt, SSM_WIDTH), lambda i, t: (i, t, C_SZ // SSM_WIDTH)),
                  pl.BlockSpec((1, 2, SSM_CH), lambda i, t: (i, 0, 0)),
                  pl.BlockSpec((8, SUBLANES, SSM_CH), const3),
                  pl.BlockSpec((n_mm, 128, 512), const3),
                  pl.BlockSpec((n_mm, 128, 512), const3),
                  pl.BlockSpec((n_mm, 1024, 128), const3),
                  pl.BlockSpec((1, SSM_WIDTH), lambda i, t: (0, 0)),
                  pl.BlockSpec((SSM_WIDTH, SSM_WIDTH), lambda i, t: (0, 0))],
        out_specs=[pl.BlockSpec((1, tt, SSM_WIDTH), lambda i, t: (i, t, 0)),
                   pl.BlockSpec((1, 2, SSM_CH), lambda i, t: (i, 0, 0))],
        out_shape=[jax.ShapeDtypeStruct((b, s, SSM_WIDTH), BF16),
                   jax.ShapeDtypeStruct((b, 2, SSM_CH), F32)],
        scratch_shapes=[pltpu.VMEM((tt, SSM_CH), F32), pltpu.VMEM((tt, SSM_CH), F32),
                        pltpu.VMEM((2, SSM_CH), F32)],
        compiler_params=_params("parallel", "arbitrary"),
        name="ssm",
    )(proj3, proj3, h0, tab, bmr, bmi, cm, d_skip, w_glu)


def _ssm_tables(lam_re, lam_im, log_step, b_re, b_im, c_re, c_im):
    pw_re, pw_im, bbr, bbi = _ssm_disc(lam_re, lam_im, log_step, b_re, b_im)
    pw_re = pw_re.reshape(SUBLANES, SSM_CH)
    pw_im = pw_im.reshape(SUBLANES, SSM_CH)
    row = jnp.arange(SUBLANES)[:, None]
    tabs = []
    for shift in (1, 2, 4):
        keep = row >= shift
        tabs += [jnp.where(keep, pw_re[shift - 1][None], 0.0), jnp.where(keep, pw_im[shift - 1][None], 0.0)]
    tab = jnp.stack(tabs + [pw_re, pw_im])
    eye = jnp.eye(8, dtype=F32)

    def b_blocks(bb):
        x = bb.reshape(SSM_GROUPS // 8, 8, SSM_GROUP_DIM, SSM_STATE)
        return jnp.einsum("jgcn,gh->jgchn", x, eye).reshape(SSM_GROUPS // 8, 128, 512).astype(BF16)

    def c_blocks(c):
        x = c.reshape(SSM_GROUPS // 8, 8, SSM_GROUP_DIM, SSM_STATE)
        return jnp.einsum("jgcn,gh->jgnhc", x, eye).reshape(SSM_GROUPS // 8, 512, 128)

    cm = jnp.concatenate([c_blocks(c_re), -c_blocks(c_im)], axis=1).astype(BF16)
    return tab, b_blocks(bbr), b_blocks(bbi), cm


def _merge_kernel(ap_ref, aa_ref, as_ref, wp_ref, wa_ref, ws_ref, g0_ref, g1_ref, g2_ref, o_ref):
    bp = jnp.dot(ap_ref[...], wp_ref[...], preferred_element_type=F32)
    ba = jnp.dot(aa_ref[...], wa_ref[...], preferred_element_type=F32)
    bs = jnp.dot(as_ref[...], ws_ref[...], preferred_element_type=F32)
    o_ref[...] = (jax.nn.sigmoid(g0_ref[...]) * bp + jax.nn.sigmoid(g1_ref[...]) * ba
                  + jax.nn.sigmoid(g2_ref[...]) * bs).astype(BF16)


def _merge(a_pool, a_att, a_ssm, w_p, w_a, w_s, proj2):
    m = a_pool.shape[0]
    tm = min(m, 512)
    tn = 1024
    act = pl.BlockSpec((tm, 1024), lambda i, n: (i, 0))
    wsp = pl.BlockSpec((1024, tn), lambda i, n: (0, n))
    gate = lambda r: pl.BlockSpec((tm, tn), lambda i, n: (i, (C_MG + r * D_MODEL) // tn + n))
    return pl.pallas_call(
        _merge_kernel,
        grid=(m // tm, D_MODEL // tn),
        in_specs=[act, act, act, wsp, wsp, wsp, gate(0), gate(1), gate(2)],
        out_specs=pl.BlockSpec((tm, tn), lambda i, n: (i, n)),
        out_shape=jax.ShapeDtypeStruct((m, D_MODEL), BF16),
        compiler_params=_params("parallel", "arbitrary"),
        name="merge",
    )(a_pool, a_att, a_ssm, w_p, w_a, w_s, proj2, proj2, proj2)


def _outproj_kernel(m_ref, w_ref, g_ref, x_ref, o_ref):
    y = jnp.dot(m_ref[...], w_ref[...], preferred_element_type=F32)
    r = lax.rsqrt(jnp.mean(y * y, axis=-1, keepdims=True) + EPS)
    o_ref[...] = x_ref[...] + y * r * g_ref[...]


def _outproj(merged, w_out, g_post, x2d):
    m = merged.shape[0]
    tm = min(m, 512)
    row = pl.BlockSpec((tm, D_MODEL), lambda i: (i, 0))
    return pl.pallas_call(
        _outproj_kernel,
        grid=(m // tm,),
        in_specs=[row, pl.BlockSpec((D_MODEL, D_MODEL), lambda i: (0, 0)),
                  pl.BlockSpec((1, D_MODEL), lambda i: (0, 0)), row],
        out_specs=row,
        out_shape=jax.ShapeDtypeStruct((m, D_MODEL), F32),
        compiler_params=_params("parallel"),
        name="outproj",
    )(merged, w_out, g_post, x2d)


def _prep_layer_weights(l, g_pre, g_post, w_in, w_pool, pool_scale, pe_cmp, w_phi, lam_re, lam_im, log_step,
                        b_re, b_im, c_re, c_im, d_skip, w_glu, w_br_pool, w_br_nsa, w_br_ssm, w_out):
    w_cmp = [sum(jnp.pad(w_phi[l, c], ((0, 0), (k * HEAD_DIM, KV_WIDTH - (k + 1) * HEAD_DIM),
                                       (k * HEAD_DIM, KV_WIDTH - (k + 1) * HEAD_DIM)))
                 for k in range(N_KV_HEADS)).astype(BF16) for c in range(2)]
    pe = [jnp.broadcast_to(pe_cmp[l, c][:, None, :], (BLOCK, N_KV_HEADS, HEAD_DIM)).reshape(BLOCK, KV_WIDTH)
          for c in range(2)]
    tab, bmr, bmi, cm = _ssm_tables(lam_re[l], lam_im[l], log_step[l], b_re[l], b_im[l], c_re[l], c_im[l])
    return dict(
        g_pre=g_pre[l].reshape(1, D_MODEL), g_post=g_post[l].reshape(1, D_MODEL),
        layer=l, wt_in=jnp.swapaxes(w_in, 1, 2),
        w_pool=w_pool[l].astype(BF16), pool_scale=pool_scale[l].reshape(1, POOL_WIDTH),
        w_cmp=w_cmp, pe=pe, tab=tab, bmr=bmr, bmi=bmi, cm=cm, d_skip=d_skip[l].reshape(1, SSM_WIDTH),
        w_glu=w_glu[l].astype(BF16), w_br_pool=w_br_pool[l].astype(BF16), w_br_nsa=w_br_nsa[l].astype(BF16),
        w_br_ssm=w_br_ssm[l].astype(BF16), w_out=w_out[l].astype(BF16))


def _layer(x3, lw, *, pos0, s_valid, pool_prefix, ssm_h0, win_prefix, paged):
    b, s, _ = x3.shape
    x2 = x3.reshape(b * s, D_MODEL)
    proj2 = _inproj(x2, lw["g_pre"], lw["wt_in"], lw["layer"], N_MAIN, INPROJ_TILE, _main_src_row)
    gates3 = _inproj(x2, lw["g_pre"], lw["wt_in"], lw["layer"], N_GATE_PAD, N_GATE_PAD,
                     lambda j: _ROW_AG + j * N_GATE_PAD).reshape(b, s, N_GATE_PAD)
    proj3 = proj2.reshape(b, s, N_MAIN)

    prefix16 = jnp.pad(pool_prefix, ((0, 0), (1, 0), (0, 0)))
    a_pool, pool_tail = _pool(proj3, prefix16, lw["w_pool"], lw["pool_scale"], q0=pos0, s_valid=s_valid)
    pool_state = pool_tail[:, 1:]

    kvn = proj3[:, :, C_KV:]
    if paged is None:
        tq = min(s, 256)
        n_cmp = nb_total = nbp = s // BLOCK
        x3k = x3v = proj3.reshape(b * n_cmp, BLOCK, N_MAIN)
        kcol, vcol = KV_COL, KV_COL + 1
    else:
        tq = s
        cache5, layer, page_table = paged
        past_len = page_table.shape[1] * PAGE_SIZE
        n_cmp = past_len // BLOCK
        nb_total = -(-(past_len + s_valid) // BLOCK)
        nbp = -(-nb_total // SUBLANES) * SUBLANES
    if paged is None:
        kc = _compress(x3k, kcol, lw["pe"][0], lw["w_cmp"][0])
        vc = _compress(x3v, vcol, lw["pe"][1], lw["w_cmp"][1])
    else:
        pe2 = jnp.stack([jnp.tile(pe, (BLOCKS_PER_PAGE, 1)) for pe in lw["pe"]])
        xk_rows, xv_rows = _gather_cmp(cache5, layer, page_table, pe2)
        kc = _compress_rows(xk_rows, lw["w_cmp"][0])
        vc = _compress_rows(xv_rows, lw["w_cmp"][1])
    kc = kc.reshape(b, n_cmp, KV_WIDTH)
    vc = vc.reshape(b, n_cmp, KV_WIDTH)
    kc = jnp.pad(kc, ((0, 0), (0, nbp - n_cmp), (0, 0)))
    vct = jnp.swapaxes(jnp.pad(vc, ((0, 0), (0, nbp - n_cmp), (0, 0))), 1, 2)
    o_cmp, sel = _cmpsel(proj3, kc, vct, pos0=pos0, tq=tq, nb_total=nb_total)
    if paged is None:
        assert pos0 == 0
        o_sel = _selattn(proj3, _tcols(proj3, KV_COL + 3), sel, tq=tq)
    else:
        o_sel = _selattn_paged(proj3, cache5, layer, page_table, sel)

    tq_win = min(s, 256)
    n_rows = ((s - tq_win) // KEY_TILE + -(-(WINDOW + tq_win) // KEY_TILE)) * KEY_TILE
    pad_rows = n_rows - WINDOW - s
    k_new, v_new = kvn[:, :, 4 * KV_WIDTH:5 * KV_WIDTH], kvn[:, :, 5 * KV_WIDTH:6 * KV_WIDTH]
    v_new_t = _tcols(proj3, KV_COL + 5)
    if win_prefix is None:
        kwin = jnp.pad(k_new, ((0, 0), (WINDOW, pad_rows), (0, 0)))
        vwin_t = jnp.pad(v_new_t, ((0, 0), (0, 0), (WINDOW, pad_rows)))
    else:
        k_pre = win_prefix[:, :, 0].reshape(b, WINDOW, KV_WIDTH)
        v_pre = win_prefix[:, :, 1].reshape(b, WINDOW, KV_WIDTH)
        kwin = jnp.pad(jnp.concatenate([k_pre, k_new], axis=1), ((0, 0), (0, pad_rows), (0, 0)))
        vwin_t = jnp.pad(jnp.concatenate([jnp.swapaxes(v_pre, 1, 2), v_new_t], axis=2),
                         ((0, 0), (0, 0), (0, pad_rows)))
    o_win = _winattn(proj3, kwin, vwin_t, pos0=pos0, row_pos0=pos0 - WINDOW, tq=tq_win)
    a_att = _attcomb(o_cmp, o_sel, o_win, gates3, proj3)

    a_ssm, ssm_state = _ssm(proj3, ssm_h0.reshape(b, 2, SSM_CH), lw["tab"], lw["bmr"], lw["bmi"], lw["cm"],
                            lw["d_skip"], lw["w_glu"], s_valid=s_valid)

    merged = _merge(a_pool.reshape(b * s, POOL_WIDTH), a_att.reshape(b * s, ATT_WIDTH),
                    a_ssm.reshape(b * s, SSM_WIDTH), lw["w_br_pool"], lw["w_br_nsa"], lw["w_br_ssm"], proj2)
    x_new = _outproj(merged, lw["w_out"], lw["g_post"], x2).reshape(b, s, D_MODEL)

    kv_rows = kvn[:, :s_valid, :4 * KV_WIDTH].reshape(b, s_valid, 4, N_KV_HEADS, HEAD_DIM)
    if win_prefix is None:
        assert s_valid >= WINDOW
        win_k, win_v = k_new[:, s_valid - WINDOW:s_valid], v_new[:, s_valid - WINDOW:s_valid]
    else:
        win_k = jnp.concatenate([k_pre, k_new], axis=1)[:, s_valid:s_valid + WINDOW]
        win_v = jnp.concatenate([v_pre, v_new], axis=1)[:, s_valid:s_valid + WINDOW]
    win_state = jnp.stack([win_k.reshape(b, WINDOW, N_KV_HEADS, HEAD_DIM),
                           win_v.reshape(b, WINDOW, N_KV_HEADS, HEAD_DIM)], axis=2)
    return x_new, kv_rows, win_state, pool_state, ssm_state.reshape(b, 2, SSM_GROUPS, SSM_STATE)


def kernel(x_prompt, x_sample, cache_kv, page_table, state_win_kv, state_pool, state_ssm, g_pre, g_post, w_in, w_pool, pool_scale, pe_cmp, w_phi, lam_re, lam_im, log_step, b_re, b_im, c_re, c_im, d_skip, w_glu, w_br_pool, w_br_nsa, w_br_ssm, w_out):
    depth = w_in.shape[0]
    bp, sp, _ = x_prompt.shape
    bd, sd, _ = x_sample.shape
    n_pool = cache_kv.shape[1]
    past_len = page_table.shape[1] * PAGE_SIZE
    assert state_win_kv.shape[2] == WINDOW and past_len >= WINDOW and sp >= WINDOW
    cache5 = jnp.transpose(cache_kv, (0, 1, 3, 4, 5, 2)).reshape(depth, n_pool, 4, KV_WIDTH, PAGE_SIZE)
    sd_pad = -(-sd // SUBLANES) * SUBLANES
    yp = x_prompt
    ys = jnp.pad(x_sample, ((0, 0), (0, sd_pad - sd), (0, 0)))
    zeros_pool = jnp.zeros((bp, POOL_STATE, POOL_WIDTH), F32)
    zeros_ssm = jnp.zeros((bp, 2, SSM_GROUPS, SSM_STATE), F32)
    outs_p, outs_s = [], []
    for l in range(depth):
        lw = _prep_layer_weights(l, g_pre, g_post, w_in, w_pool, pool_scale, pe_cmp, w_phi, lam_re, lam_im, log_step,
                                 b_re, b_im, c_re, c_im, d_skip, w_glu, w_br_pool, w_br_nsa, w_br_ssm, w_out)
        yp, *rp = _layer(yp, lw, pos0=0, s_valid=sp, pool_prefix=zeros_pool, ssm_h0=zeros_ssm,
                         win_prefix=None, paged=None)
        ys, *rs = _layer(ys, lw, pos0=past_len, s_valid=sd, pool_prefix=state_pool[l], ssm_h0=state_ssm[l],
                         win_prefix=state_win_kv[l], paged=(cache5, l, page_table))
        outs_p.append(rp)
        outs_s.append(rs)
    stack = lambda outs, i: jnp.stack([o[i] for o in outs])
    return (yp, ys[:, :sd], stack(outs_p, 0), stack(outs_s, 0), stack(outs_p, 1), stack(outs_s, 1),
            stack(outs_p, 2), stack(outs_s, 2), stack(outs_p, 3), stack(outs_s, 3))
```

```python
import functools

import jax
import jax.numpy as jnp
from jax import lax
from jax.experimental import pallas as pl
from jax.experimental.pallas import tpu as pltpu

F32 = jnp.float32
BF16 = jnp.bfloat16
I32 = jnp.int32

D_MODEL = 2048
PAGE_SIZE = 128
POOL_WIDTH = D_MODEL // 2
POOL_WINDOWS = (2, 4, 8, 16)
POOL_GROUP_DIM = POOL_WIDTH // len(POOL_WINDOWS)
POOL_STATE = max(POOL_WINDOWS) - 1
N_HEADS = 16
HEAD_DIM = 64
N_KV_HEADS = 4
GQA = N_HEADS // N_KV_HEADS
ATT_WIDTH = N_HEADS * HEAD_DIM
KV_WIDTH = N_KV_HEADS * HEAD_DIM
BLOCK = 64
TOP_N = 16
WINDOW = 512
N_ATT_GATES = 3
SSM_WIDTH = D_MODEL // 2
SSM_GROUP_DIM = 16
SSM_GROUPS = SSM_WIDTH // SSM_GROUP_DIM
SSM_STATE = 64
SSM_CH = SSM_GROUPS * SSM_STATE
N_BRANCH = 3
EPS = 1e-6
NEG = -1e30
FORCE = 1e4
ATT_SCALE = HEAD_DIM ** -0.5
LOG2_E = 1.4426950408889634

C_PU, C_PZ, C_Q, C_AZ, C_SU, C_SZ = (i * 1024 for i in range(6))
C_MG = 6 * 1024
C_KV = C_MG + N_BRANCH * D_MODEL
N_MAIN = C_KV + 6 * KV_WIDTH
N_GATE_PAD = 128
KV_COL = C_KV // KV_WIDTH

VMEM_LIMIT_BYTES = 52 * 1024 * 1024
SUBLANES = 8
KEY_TILE = 256
BLOCKS_PER_TILE = KEY_TILE // BLOCK
BLOCKS_PER_PAGE = PAGE_SIZE // BLOCK
PAGES_PER_STEP = 16
SSM_LANE_CHUNK = 1024

NT_DIMS = (((1,), (1,)), ((), ()))


def _params(*sem):
    return pltpu.CompilerParams(dimension_semantics=sem, vmem_limit_bytes=VMEM_LIMIT_BYTES)


def _silu(x):
    return x * jax.nn.sigmoid(x)


def _inproj_kernel(x_ref, g_ref, w_ref, o_ref, h_ref):
    @pl.when(pl.program_id(1) == 0)
    def _():
        x = x_ref[...]
        r = lax.rsqrt(jnp.mean(x * x, axis=-1, keepdims=True) + EPS)
        h_ref[...] = (x * r * g_ref[...]).astype(BF16)

    o_ref[...] = lax.dot_general(h_ref[...], w_ref[0].astype(BF16), NT_DIMS, preferred_element_type=F32)


def _inproj(x2d, g, wt_all, layer, n_out, tn, src_row):
    m, d = x2d.shape
    tm = min(m, 1024)
    w_spec = pl.BlockSpec((pl.Element(1), pl.Element(tn), pl.Element(d)),
                          lambda i, j: (layer, pl.multiple_of(src_row(j), SUBLANES), 0))
    return pl.pallas_call(
        _inproj_kernel,
        grid=(m // tm, n_out // tn),
        in_specs=[pl.BlockSpec((tm, d), lambda i, j: (i, 0)),
                  pl.BlockSpec((1, d), lambda i, j: (0, 0)),
                  w_spec],
        out_specs=pl.BlockSpec((tm, tn), lambda i, j: (i, j)),
        out_shape=jax.ShapeDtypeStruct((m, n_out), F32),
        scratch_shapes=[pltpu.VMEM((tm, d), BF16)],
        compiler_params=_params("parallel", "arbitrary"),
        name="inproj",
    )(x2d, g, wt_all)


_ROW_KV = 2 * POOL_WIDTH + ATT_WIDTH
_ROW_AG = _ROW_KV + 6 * KV_WIDTH
_ROW_AZ = _ROW_AG + N_HEADS * N_ATT_GATES
INPROJ_TILE = 768


def _main_src_row(j):
    n_head = _ROW_KV // INPROJ_TILE
    n_mid = (C_KV - _ROW_KV) // INPROJ_TILE
    return jnp.where(j < n_head, j * INPROJ_TILE,
                     jnp.where(j < n_head + n_mid, _ROW_AZ + (j - n_head) * INPROJ_TILE,
                               _ROW_KV + (j - n_head - n_mid) * INPROJ_TILE))


def _pool_kernel(pu_ref, pz_ref, pre_ref, wp_ref, sc_ref, o_ref, st_ref, e_ref, *, tt, q0, rows_last):
    t = pl.program_id(1)
    hist = POOL_STATE + 1

    @pl.when(t == 0)
    def _():
        e_ref[0:hist, :] = pre_ref[0]

    @pl.when(t > 0)
    def _():
        e_ref[0:hist, :] = e_ref[tt:tt + hist, :]

    u = pu_ref[0]
    e_ref[hist:hist + tt, :] = u
    pos = q0 + t * tt + lax.broadcasted_iota(I32, (tt, 1), 0)
    ys = []
    for gi, w in enumerate(POOL_WINDOWS):
        lo = gi * POOL_GROUP_DIM
        tot = e_ref[hist:hist + tt, lo:lo + POOL_GROUP_DIM]
        for k in range(1, w):
            tot = tot + e_ref[hist - k:hist - k + tt, lo:lo + POOL_GROUP_DIM]
        cnt = jnp.minimum(pos + 1, w).astype(F32)
        diff = tot / cnt - u[:, lo:lo + POOL_GROUP_DIM]
        ys.append(jnp.dot(diff.astype(BF16), wp_ref[gi], preferred_element_type=F32))
    y = jnp.concatenate(ys, axis=-1) * sc_ref[...]
    o_ref[0] = (y * _silu(pz_ref[0])).astype(BF16)

    @pl.when(t == pl.num_programs(1) - 1)
    def _():
        st_ref[0] = e_ref[rows_last:rows_last + hist, :]


def _pool(proj3, prefix16, w_pool, pool_scale, *, q0, s_valid):
    b, s, _ = proj3.shape
    tt = min(s, 512)
    hist = POOL_STATE + 1
    rows_last = ((s_valid - 1) % tt) + 1
    kern = functools.partial(_pool_kernel, tt=tt, q0=q0, rows_last=rows_last)
    return pl.pallas_call(
        kern,
        grid=(b, s // tt),
        in_specs=[pl.BlockSpec((1, tt, POOL_WIDTH), lambda i, t: (i, t, C_PU // POOL_WIDTH)),
                  pl.BlockSpec((1, tt, POOL_WIDTH), lambda i, t: (i, t, C_PZ // POOL_WIDTH)),
                  pl.BlockSpec((1, hist, POOL_WIDTH), lambda i, t: (i, 0, 0)),
                  pl.BlockSpec((len(POOL_WINDOWS), POOL_GROUP_DIM, POOL_GROUP_DIM), lambda i, t: (0, 0, 0)),
                  pl.BlockSpec((1, POOL_WIDTH), lambda i, t: (0, 0))],
        out_specs=[pl.BlockSpec((1, tt, POOL_WIDTH), lambda i, t: (i, t, 0)),
                   pl.BlockSpec((1, hist, POOL_WIDTH), lambda i, t: (i, 0, 0))],
        out_shape=[jax.ShapeDtypeStruct((b, s, POOL_WIDTH), BF16),
                   jax.ShapeDtypeStruct((b, hist, POOL_WIDTH), F32)],
        scratch_shapes=[pltpu.VMEM((hist + tt, POOL_WIDTH), F32)],
        compiler_params=_params("parallel", "arbitrary"),
        name="pool",
    )(proj3, proj3, prefix16, w_pool, pool_scale)


def _compress_kernel(x_ref, pe_ref, w_ref, o_ref):
    def one_row(l, acc):
        x = x_ref[:, l, :] + pe_ref[pl.ds(l, 1), :]
        return acc + jnp.dot(x.astype(BF16), w_ref[l], preferred_element_type=F32)

    o_ref[...] = lax.fori_loop(0, BLOCK, one_row, jnp.zeros(o_ref.shape, F32))


def _compress(x3, col, pe, w):
    nb = x3.shape[0]
    tm = min(nb, 128)
    return pl.pallas_call(
        _compress_kernel,
        grid=(nb // tm,),
        in_specs=[pl.BlockSpec((tm, BLOCK, KV_WIDTH), lambda i: (i, 0, col)),
                  pl.BlockSpec((BLOCK, KV_WIDTH), lambda i: (0, 0)),
                  pl.BlockSpec((BLOCK, KV_WIDTH, KV_WIDTH), lambda i: (0, 0, 0))],
        out_specs=pl.BlockSpec((tm, KV_WIDTH), lambda i: (i, 0)),
        out_shape=jax.ShapeDtypeStruct((nb, KV_WIDTH), F32),
        compiler_params=_params("parallel"),
        name="compress",
    )(x3, pe, w)


def _compress_rows_kernel(x_ref, w_ref, o_ref):
    x = jnp.concatenate([x_ref[l] for l in range(BLOCK)], axis=1)
    o_ref[...] = jnp.dot(x, w_ref[...].reshape(BLOCK * KV_WIDTH, KV_WIDTH), preferred_element_type=F32)


def _compress_rows(xl, w):
    nb = xl.shape[1]
    tm = min(nb, 256)
    return pl.pallas_call(
        _compress_rows_kernel,
        grid=(nb // tm,),
        in_specs=[pl.BlockSpec((BLOCK, tm, KV_WIDTH), lambda i: (0, i, 0)),
                  pl.BlockSpec((BLOCK, KV_WIDTH, KV_WIDTH), lambda i: (0, 0, 0))],
        out_specs=pl.BlockSpec((tm, KV_WIDTH), lambda i: (i, 0)),
        out_shape=jax.ShapeDtypeStruct((nb, KV_WIDTH), F32),
        compiler_params=_params("parallel"),
        name="compress_rows",
    )(xl, w)


def _page_specs(layer, n_pages, comp_block):
    def spec(i):
        def index(bi, st, pt):
            return (layer, pt[bi, jnp.minimum(st * PAGES_PER_STEP + i, n_pages - 1)], comp_block, 0, 0)
        return pl.BlockSpec((1, 1, 2, KV_WIDTH, PAGE_SIZE), index)
    return [spec(i) for i in range(PAGES_PER_STEP)]


def _sublane_transpose8(a):
    sub = lax.broadcasted_iota(I32, a[0].shape, 0)
    for shift in (4, 2, 1):
        low = (sub & shift) == 0
        nxt = list(a)
        for j in range(SUBLANES):
            if j & shift == 0:
                nxt[j] = jnp.where(low, a[j], pltpu.roll(a[j + shift], shift, axis=0))
                nxt[j + shift] = jnp.where(low, pltpu.roll(a[j], SUBLANES - shift, axis=0), a[j + shift])
        a = nxt
    return a


def _gather_cmp_kernel(pt_ref, *refs):
    del pt_ref
    pages, (pe_ref, xk_ref, xv_ref) = refs[:PAGES_PER_STEP], refs[PAGES_PER_STEP:]
    n_groups = PAGES_PER_STEP * BLOCKS_PER_PAGE // SUBLANES
    for comp, dst in enumerate((xk_ref, xv_ref)):
        rows = [page[0, 0, comp].T + pe_ref[comp] for page in pages]
        for m in range(BLOCK // SUBLANES):
            per_group = []
            for grp in range(n_groups):
                pieces = []
                for jj in range(SUBLANES):
                    j = grp * SUBLANES + jj
                    lo = (j % BLOCKS_PER_PAGE) * BLOCK + m * SUBLANES
                    pieces.append(rows[j // BLOCKS_PER_PAGE][lo:lo + SUBLANES, :])
                per_group.append(_sublane_transpose8(pieces))
            for s in range(SUBLANES):
                dst[m * SUBLANES + s] = jnp.concatenate([g[s] for g in per_group], axis=0).astype(BF16)


def _gather_cmp(cache5, layer, page_table, pe2):
    b, n_pages = page_table.shape
    n_steps = n_pages // PAGES_PER_STEP
    blocks_per_step = PAGES_PER_STEP * BLOCKS_PER_PAGE
    assert blocks_per_step % (2 * SUBLANES) == 0
    out_spec = pl.BlockSpec((BLOCK, blocks_per_step, KV_WIDTH), lambda bi, st, pt: (0, bi * n_steps + st, 0))
    out = jax.ShapeDtypeStruct((BLOCK, b * n_pages * BLOCKS_PER_PAGE, KV_WIDTH), BF16)
    return pl.pallas_call(
        _gather_cmp_kernel,
        grid_spec=pltpu.PrefetchScalarGridSpec(
            num_scalar_prefetch=1, grid=(b, n_steps),
            in_specs=_page_specs(layer, n_pages, 0) + [
                pl.BlockSpec((2, PAGE_SIZE, KV_WIDTH), lambda bi, st, pt: (0, 0, 0))],
            out_specs=[out_spec, out_spec]),
        out_shape=[out, out],
        compiler_params=_params("parallel", "arbitrary"),
        name="gather_cmp",
    )(page_table, *([cache5] * PAGES_PER_STEP), pe2)


def _cmpsel_kernel(q_ref, kc_ref, vct_ref, ocmp_ref, sel_ref, s_ref, *, tq, nbp, pos0, n_sel):
    j = pl.program_id(1)
    qs = (q_ref[0] * ATT_SCALE).astype(BF16)
    kc = kc_ref[0].astype(BF16)
    vct = vct_ref[0].astype(BF16)
    nidx = lax.broadcasted_iota(I32, (nbp, tq), 0)
    pos = pos0 + j * tq + lax.broadcasted_iota(I32, (nbp, tq), 1)
    cvalid = nidx < lax.shift_right_logical(pos + 1, 6)
    cur = lax.shift_right_logical(pos, 6)
    forced = (nidx == 0) | (nidx == cur) | (nidx == cur - 1)
    outs = []
    for k in range(N_KV_HEADS):
        ksl = slice(k * HEAD_DIM, (k + 1) * HEAD_DIM)
        imp = jnp.zeros((nbp, tq), F32)
        for g in range(GQA):
            h = k * GQA + g
            sc = lax.dot_general(kc[:, ksl], qs[:, h * HEAD_DIM:(h + 1) * HEAD_DIM], NT_DIMS,
                                 preferred_element_type=F32)
            sc = jnp.where(cvalid, sc, NEG)
            e = jnp.exp(sc - jnp.max(sc, axis=0, keepdims=True))
            pc = jnp.where(cvalid, e / jnp.sum(e, axis=0, keepdims=True), 0.0)
            imp = imp + pc
            outs.append(jnp.dot(vct[ksl, :], pc.astype(BF16), preferred_element_type=F32))
        s_ref[:, k * tq:(k + 1) * tq] = jnp.where(forced, FORCE, jnp.where(nidx < cur, imp, NEG))
    ocmp_ref[0] = jnp.concatenate(outs, axis=0)

    lanes = N_KV_HEADS * tq
    chunk = min(lanes, 128)
    rows = lax.broadcasted_iota(I32, (nbp, chunk), 0)
    for c in range(lanes // chunk):
        score = s_ref[:, c * chunk:(c + 1) * chunk]
        work, sel = score, jnp.zeros_like(score)
        for _ in range(n_sel):
            top = jnp.max(work, axis=0, keepdims=True)
            first = jnp.min(jnp.where(work == top, rows, nbp), axis=0, keepdims=True)
            hit = rows == first
            work, sel = jnp.where(hit, -jnp.inf, work), jnp.where(hit, 1.0, sel)
        sel_ref[0, :, c * chunk:(c + 1) * chunk] = jnp.where(score > NEG / 2, sel, 0.0)


def _cmpsel(proj3, kc, vct, *, pos0, tq, nb_total):
    b, s, _ = proj3.shape
    nbp = kc.shape[1]
    kern = functools.partial(_cmpsel_kernel, tq=tq, nbp=nbp, pos0=pos0, n_sel=min(TOP_N, nb_total))
    lanes = N_KV_HEADS * tq
    return pl.pallas_call(
        kern,
        grid=(b, s // tq),
        in_specs=[pl.BlockSpec((1, tq, ATT_WIDTH), lambda i, j: (i, j, C_Q // ATT_WIDTH)),
                  pl.BlockSpec((1, nbp, KV_WIDTH), lambda i, j: (i, 0, 0)),
                  pl.BlockSpec((1, KV_WIDTH, nbp), lambda i, j: (i, 0, 0))],
        out_specs=[pl.BlockSpec((1, ATT_WIDTH, tq), lambda i, j: (i, 0, j)),
                   pl.BlockSpec((1, nbp, lanes), lambda i, j: (i, 0, j))],
        out_shape=[jax.ShapeDtypeStruct((b, ATT_WIDTH, s), F32),
                   jax.ShapeDtypeStruct((b, nbp, N_KV_HEADS * s), F32)],
        scratch_shapes=[pltpu.VMEM((nbp, lanes), F32)],
        compiler_params=_params("parallel", "parallel"),
        name="cmpsel",
    )(proj3, kc, vct)


def _attn_init(q_ref, qs_ref, m_ref, acc_ref, tq):
    qs = (q_ref[0] * (ATT_SCALE * LOG2_E)).astype(BF16)
    for k in range(N_KV_HEADS):
        for g in range(GQA):
            h = k * GQA + g
            qs_ref[k, g * tq:(g + 1) * tq, :] = qs[:, h * HEAD_DIM:(h + 1) * HEAD_DIM]
    m_ref[...] = jnp.full_like(m_ref, NEG)
    acc_ref[...] = jnp.zeros_like(acc_ref)


def _attn_step(k, lanes, k_tile, vt_tile, block_rows, ok, qs_ref, m_ref, acc_ref):
    ksl = slice(k * HEAD_DIM, (k + 1) * HEAD_DIM)
    s = lax.dot_general(k_tile[:, ksl], qs_ref[k, lanes, :], NT_DIMS, preferred_element_type=F32)
    if block_rows is not None:
        s = jnp.concatenate([jnp.where(row > 0.5, s[i * BLOCK:(i + 1) * BLOCK], NEG)
                             for i, row in enumerate(block_rows)], axis=0)
    if ok is not None:
        w = ok.shape[1]
        s = jnp.concatenate([jnp.where(ok, s[:, i * w:(i + 1) * w], NEG) for i in range(s.shape[1] // w)], axis=1)
    m_old = m_ref[k, :, lanes]
    m_new = jnp.maximum(m_old, jnp.max(s, axis=0, keepdims=True))
    p = jnp.exp2(s - m_new)
    alpha = jnp.exp2(m_old - m_new)
    m_ref[k, :, lanes] = m_new
    vt_ones = jnp.concatenate([vt_tile[ksl, :], jnp.ones((ONES_ROWS, vt_tile.shape[1]), BF16)], axis=0)
    acc_ref[k, :, lanes] = (alpha * acc_ref[k, :, lanes]
                            + jnp.dot(vt_ones, p.astype(BF16), preferred_element_type=F32))


def _attn_finish(o_ref, acc_ref, tq):
    outs = []
    for k in range(N_KV_HEADS):
        o = acc_ref[k, 0:HEAD_DIM, :] / acc_ref[k, HEAD_DIM:HEAD_DIM + 1, :]
        for g in range(GQA):
            outs.append(o[:, g * tq:(g + 1) * tq])
    o_ref[0] = jnp.concatenate(outs, axis=0)


ONES_ROWS = 16


def _attn_scratch(tq):
    r = GQA * tq
    return [pltpu.VMEM((N_KV_HEADS, r, HEAD_DIM), BF16),
            pltpu.VMEM((N_KV_HEADS, 1, r), F32),
            pltpu.VMEM((N_KV_HEADS, HEAD_DIM + ONES_ROWS, r), F32)]


def _lane_query_pos(shape, tq, first):
    return first + (lax.broadcasted_iota(I32, shape, 1) & (tq - 1))


def _selattn_kernel(jt_ref, kt_ref, q_ref, k_ref, vt_ref, sel_ref, o_ref, qs_ref, m_ref, acc_ref, *, tq):
    step = pl.program_id(1)
    j = jt_ref[step]
    kt = kt_ref[step]

    @pl.when(kt == 0)
    def _():
        _attn_init(q_ref, qs_ref, m_ref, acc_ref, tq)

    def key_tile(diagonal):
        k_tile = k_ref[0].astype(BF16)
        vt_tile = vt_ref[0].astype(BF16)
        r = GQA * tq
        causal = None
        if diagonal:
            causal = (lax.broadcasted_iota(I32, (KEY_TILE, tq), 0) <= lax.broadcasted_iota(I32, (KEY_TILE, tq), 1))
        for k in range(N_KV_HEADS):
            rows = [jnp.concatenate([sel_ref[0, pl.ds(kt * BLOCKS_PER_TILE + i, 1), k * tq:(k + 1) * tq]] * GQA,
                                    axis=1) for i in range(BLOCKS_PER_TILE)]
            _attn_step(k, slice(0, r), k_tile, vt_tile, rows, causal, qs_ref, m_ref, acc_ref)

    @pl.when(kt < j)
    def _():
        key_tile(False)

    @pl.when(kt == j)
    def _():
        key_tile(True)
        _attn_finish(o_ref, acc_ref, tq)


def _selattn(proj3, vt, sel, *, tq):
    b, s, _ = proj3.shape
    assert tq == KEY_TILE
    nbp = sel.shape[1]
    pairs = [(j, kt) for j in range(s // tq) for kt in range(j + 1)]
    jt = jnp.asarray([p[0] for p in pairs], I32)
    ktt = jnp.asarray([p[1] for p in pairs], I32)
    kern = functools.partial(_selattn_kernel, tq=tq)
    return pl.pallas_call(
        kern,
        grid_spec=pltpu.PrefetchScalarGridSpec(
            num_scalar_prefetch=2, grid=(b, len(pairs)),
            in_specs=[pl.BlockSpec((1, tq, ATT_WIDTH), lambda i, p, jt, kt: (i, jt[p], C_Q // ATT_WIDTH)),
                      pl.BlockSpec((1, KEY_TILE, KV_WIDTH), lambda i, p, jt, kt: (i, kt[p], KV_COL + 2)),
                      pl.BlockSpec((1, KV_WIDTH, KEY_TILE), lambda i, p, jt, kt: (i, 0, kt[p])),
                      pl.BlockSpec((1, nbp, N_KV_HEADS * tq), lambda i, p, jt, kt: (i, 0, jt[p]))],
            out_specs=pl.BlockSpec((1, ATT_WIDTH, tq), lambda i, p, jt, kt: (i, 0, jt[p])),
            scratch_shapes=_attn_scratch(tq)),
        out_shape=jax.ShapeDtypeStruct((b, ATT_WIDTH, s), F32),
        compiler_params=_params("parallel", "arbitrary"),
        name="selattn",
    )(jt, ktt, proj3, proj3, vt, sel)


def _winattn_kernel(q_ref, k_ref, vt_ref, o_ref, qs_ref, m_ref, acc_ref, *, tq, pos0, row_pos0, n_sub):
    j = pl.program_id(1)
    t = pl.program_id(2)
    r = GQA * tq

    @pl.when(t == 0)
    def _():
        _attn_init(q_ref, qs_ref, m_ref, acc_ref, tq)

    tile = (j * tq) // KEY_TILE + (n_sub - 1 - t)

    def run(ok):
        k_tile = k_ref[0].astype(BF16)
        vt_tile = vt_ref[0].astype(BF16)
        for k in range(N_KV_HEADS):
            _attn_step(k, slice(0, r), k_tile, vt_tile, None, ok, qs_ref, m_ref, acc_ref)

    aligned = (tq == KEY_TILE and WINDOW % KEY_TILE == 0 and row_pos0 % KEY_TILE == 0
               and (pos0 - row_pos0) % KEY_TILE == 0)
    if aligned:
        key_i = lax.broadcasted_iota(I32, (KEY_TILE, tq), 0)
        qry_i = lax.broadcasted_iota(I32, (KEY_TILE, tq), 1)
        in_range = row_pos0 + tile * KEY_TILE >= 0

        @pl.when(t == 0)
        def _():
            run(key_i <= qry_i)

        @pl.when((t > 0) & (t < n_sub - 1) & in_range)
        def _():
            run(None)

        @pl.when((t == n_sub - 1) & in_range)
        def _():
            run(key_i > qry_i)
    else:
        kp = row_pos0 + tile * KEY_TILE + lax.broadcasted_iota(I32, (KEY_TILE, r), 0)
        pq = _lane_query_pos((KEY_TILE, r), tq, pos0 + j * tq)
        run((kp >= 0) & (kp <= pq) & (kp > pq - WINDOW))

    @pl.when(t == n_sub - 1)
    def _():
        _attn_finish(o_ref, acc_ref, tq)


def _winattn(proj3, kwin, kcol, vwin_t, *, pos0, row_pos0, tq, virtual_tiles=0):
    b, s, _ = proj3.shape
    assert tq & (tq - 1) == 0 and (tq % KEY_TILE == 0 or s == tq)
    n_sub = -(-(WINDOW + tq) // KEY_TILE)
    assert kwin.shape[1] >= ((s - tq) // KEY_TILE + n_sub - virtual_tiles) * KEY_TILE
    assert virtual_tiles == 0 or (tq == KEY_TILE and row_pos0 + virtual_tiles * KEY_TILE == 0)

    def tile(j, t):
        return jnp.maximum((j * tq) // KEY_TILE + (n_sub - 1 - t) - virtual_tiles, 0)

    kern = functools.partial(_winattn_kernel, tq=tq, pos0=pos0, row_pos0=row_pos0, n_sub=n_sub)
    return pl.pallas_call(
        kern,
        grid=(b, s // tq, n_sub),
        in_specs=[pl.BlockSpec((1, tq, ATT_WIDTH), lambda i, j, t: (i, j, C_Q // ATT_WIDTH)),
                  pl.BlockSpec((1, KEY_TILE, KV_WIDTH), lambda i, j, t: (i, tile(j, t), kcol)),
                  pl.BlockSpec((1, KV_WIDTH, KEY_TILE), lambda i, j, t: (i, 0, tile(j, t)))],
        out_specs=pl.BlockSpec((1, ATT_WIDTH, tq), lambda i, j, t: (i, 0, j)),
        out_shape=jax.ShapeDtypeStruct((b, ATT_WIDTH, s), F32),
        scratch_shapes=_attn_scratch(tq),
        compiler_params=_params("parallel", "parallel", "arbitrary"),
        name="winattn",
    )(proj3, kwin, vwin_t)


def _selattn_paged_kernel(pt_ref, *refs, tq, n_steps):
    del pt_ref
    pages = refs[:PAGES_PER_STEP]
    q_ref, new_ref, sel_ref, o_ref, qbd_ref, selr_ref, m_ref, l_ref, acc_ref = refs[PAGES_PER_STEP:]
    st = pl.program_id(1)
    rows = N_HEADS * tq

    def flash(s, ok, v, v_is_transposed):
        s = jnp.where(ok, s, NEG)
        m_old = m_ref[...]
        m_new = jnp.maximum(m_old, jnp.max(s, axis=-1, keepdims=True))
        p = jnp.exp(s - m_new)
        alpha = jnp.exp(m_old - m_new)
        l_ref[...] = alpha * l_ref[...] + jnp.sum(p, axis=-1, keepdims=True)
        m_ref[...] = m_new
        if v_is_transposed:
            pv = lax.dot_general(p.astype(BF16), v, NT_DIMS, preferred_element_type=F32)
        else:
            pv = jnp.dot(p.astype(BF16), v, preferred_element_type=F32)
        acc_ref[...] = alpha * acc_ref[...] + pv

    @pl.when(st == 0)
    def _():
        qs = (q_ref[0] * ATT_SCALE).astype(BF16)
        qbd_ref[...] = jnp.zeros_like(qbd_ref)
        for k in range(N_KV_HEADS):
            for g in range(GQA):
                h = k * GQA + g
                qbd_ref[h * tq:(h + 1) * tq, k * HEAD_DIM:(k + 1) * HEAD_DIM] = qs[:, h * HEAD_DIM:(h + 1) * HEAD_DIM]
                selr_ref[:, h * tq:(h + 1) * tq] = sel_ref[0, :, k * tq:(k + 1) * tq]
        m_ref[...] = jnp.full_like(m_ref, NEG)
        l_ref[...] = jnp.zeros_like(l_ref)
        acc_ref[...] = jnp.zeros_like(acc_ref)

    @pl.when(st < n_steps)
    def _():
        blk = st * (PAGES_PER_STEP * BLOCKS_PER_PAGE)
        chosen = jnp.concatenate(
            [jnp.broadcast_to(selr_ref[pl.ds(blk + c, 1), :], (BLOCK, rows))
             for c in range(PAGES_PER_STEP * BLOCKS_PER_PAGE)], axis=0)
        k_t = jnp.concatenate([page[0, 0, 0] for page in pages], axis=1).astype(BF16)
        v_t = jnp.concatenate([page[0, 0, 1] for page in pages], axis=1).astype(BF16)
        s = jnp.dot(qbd_ref[...], k_t, preferred_element_type=F32)
        flash(s, chosen.T > 0.5, v_t, True)

    @pl.when(st == n_steps)
    def _():
        new = new_ref[0]
        k_new = new[:, 2 * KV_WIDTH:3 * KV_WIDTH].astype(BF16)
        v_new = new[:, 3 * KV_WIDTH:4 * KV_WIDTH].astype(BF16)
        s = lax.dot_general(qbd_ref[...], k_new, NT_DIMS, preferred_element_type=F32)
        blk = n_steps * PAGES_PER_STEP * BLOCKS_PER_PAGE
        chosen = jnp.broadcast_to(selr_ref[pl.ds(blk, 1), :], (tq, rows)).T
        causal = (lax.broadcasted_iota(I32, (rows, tq), 1)
                  <= (lax.broadcasted_iota(I32, (rows, tq), 0) & (tq - 1)))
        flash(s, (chosen > 0.5) & causal, v_new, False)
        o = acc_ref[...] / l_ref[...]
        outs = []
        for k in range(N_KV_HEADS):
            for g in range(GQA):
                h = k * GQA + g
                outs.append(o[h * tq:(h + 1) * tq, k * HEAD_DIM:(k + 1) * HEAD_DIM].T)
        o_ref[0] = jnp.concatenate(outs, axis=0)


def _selattn_paged(proj3, cache5, layer, page_table, sel):
    b, tq, _ = proj3.shape
    _, n_pages = page_table.shape
    assert tq & (tq - 1) == 0 and tq <= BLOCK and n_pages % PAGES_PER_STEP == 0
    n_steps = n_pages // PAGES_PER_STEP
    nbp = sel.shape[1]
    rows = N_HEADS * tq
    kern = functools.partial(_selattn_paged_kernel, tq=tq, n_steps=n_steps)
    return pl.pallas_call(
        kern,
        grid_spec=pltpu.PrefetchScalarGridSpec(
            num_scalar_prefetch=1, grid=(b, n_steps + 1),
            in_specs=_page_specs(layer, n_pages, 1) + [
                pl.BlockSpec((1, tq, ATT_WIDTH), lambda bi, st, pt: (bi, 0, C_Q // ATT_WIDTH)),
                pl.BlockSpec((1, tq, 4 * KV_WIDTH), lambda bi, st, pt: (bi, 0, C_KV // (4 * KV_WIDTH))),
                pl.BlockSpec((1, nbp, N_KV_HEADS * tq), lambda bi, st, pt: (bi, 0, 0))],
            out_specs=pl.BlockSpec((1, ATT_WIDTH, tq), lambda bi, st, pt: (bi, 0, 0)),
            scratch_shapes=[pltpu.VMEM((rows, KV_WIDTH), BF16), pltpu.VMEM((nbp, rows), F32),
                            pltpu.VMEM((rows, 1), F32), pltpu.VMEM((rows, 1), F32),
                            pltpu.VMEM((rows, KV_WIDTH), F32)]),
        out_shape=jax.ShapeDtypeStruct((b, ATT_WIDTH, tq), F32),
        compiler_params=_params("parallel", "arbitrary"),
        name="selattn_paged",
    )(page_table, *([cache5] * PAGES_PER_STEP), proj3, proj3, sel)


def _tcols_kernel(x_ref, o_ref):
    o_ref[0] = x_ref[0].T


def _tcols(proj3, col):
    b, s, _ = proj3.shape
    ts = min(s, 512)
    return pl.pallas_call(
        _tcols_kernel,
        grid=(b, s // ts),
        in_specs=[pl.BlockSpec((1, ts, KV_WIDTH), lambda i, t: (i, t, col))],
        out_specs=pl.BlockSpec((1, KV_WIDTH, ts), lambda i, t: (i, 0, t)),
        out_shape=jax.ShapeDtypeStruct((b, KV_WIDTH, s), F32),
        compiler_params=_params("parallel", "parallel"),
        name="tcols",
    )(proj3)


def _attcomb_kernel(oc_ref, os_ref, ow_ref, ag_ref, az_ref, o_ref):
    gate_t = jax.nn.sigmoid(ag_ref[0]).T
    oc, osel, ow = oc_ref[0], os_ref[0], ow_ref[0]
    outs = []
    for h in range(N_HEADS):
        sl = slice(h * HEAD_DIM, (h + 1) * HEAD_DIM)
        c = h * N_ATT_GATES
        outs.append(gate_t[c:c + 1, :] * oc[sl, :] + gate_t[c + 1:c + 2, :] * osel[sl, :]
                    + gate_t[c + 2:c + 3, :] * ow[sl, :])
    o_ref[0] = (jnp.concatenate(outs, axis=0).T * _silu(az_ref[0])).astype(BF16)


def _attcomb(ocmp_t, osel_t, owin_t, gates3, proj3):
    b, s, _ = proj3.shape
    tt = min(s, 256)
    att_t = pl.BlockSpec((1, ATT_WIDTH, tt), lambda i, t: (i, 0, t))
    return pl.pallas_call(
        _attcomb_kernel,
        grid=(b, s // tt),
        in_specs=[att_t, att_t, att_t,
                  pl.BlockSpec((1, tt, N_GATE_PAD), lambda i, t: (i, t, 0)),
                  pl.BlockSpec((1, tt, ATT_WIDTH), lambda i, t: (i, t, C_AZ // ATT_WIDTH))],
        out_specs=pl.BlockSpec((1, tt, ATT_WIDTH), lambda i, t: (i, t, 0)),
        out_shape=jax.ShapeDtypeStruct((b, s, ATT_WIDTH), BF16),
        compiler_params=_params("parallel", "parallel"),
        name="attcomb",
    )(ocmp_t, osel_t, owin_t, gates3, proj3)


def _ssm_disc_kernel(lr_ref, li_ref, ls_ref, brt_ref, bit_ref, pr_ref, pi_ref, bbr_ref, bbi_ref):
    lr, li = lr_ref[...], li_ref[...]
    dt = jnp.exp(ls_ref[...])
    mag = jnp.exp(lr * dt)
    ab_re, ab_im = mag * jnp.cos(li * dt), mag * jnp.sin(li * dt)
    den = lr * lr + li * li
    co_re = ((ab_re - 1.0) * lr + ab_im * li) / den
    co_im = (ab_im * lr - (ab_re - 1.0) * li) / den
    brt, bit = brt_ref[...], bit_ref[...]
    bbr_ref[...] = co_re[:, None, :] * brt - co_im[:, None, :] * bit
    bbi_ref[...] = co_re[:, None, :] * bit + co_im[:, None, :] * brt
    pr, pi = ab_re, ab_im
    pr_ref[0], pi_ref[0] = pr, pi
    for r in range(1, SUBLANES):
        pr, pi = pr * ab_re - pi * ab_im, pr * ab_im + pi * ab_re
        pr_ref[r], pi_ref[r] = pr, pi


def _ssm_disc(lam_re, lam_im, log_step, b_re, b_im):
    g, n = lam_re.shape
    brt = jnp.swapaxes(b_re, 1, 2)
    bit = jnp.swapaxes(b_im, 1, 2)
    pw = jax.ShapeDtypeStruct((SUBLANES, g, n), F32)
    bb = jax.ShapeDtypeStruct(brt.shape, F32)
    return pl.pallas_call(_ssm_disc_kernel, out_shape=[pw, pw, bb, bb], name="ssm_disc")(
        lam_re, lam_im, log_step.reshape(g, 1), brt, bit)


def _ssm_kernel(su_ref, sz_ref, h0_ref, tab_ref, bmr_ref, bmi_ref, cm_ref, ds_ref, wg_ref,
                o_ref, st_ref, hr_ref, hi_ref, c_ref, *, tt, row_last):
    t = pl.program_id(1)

    @pl.when(t == 0)
    def _():
        c_ref[...] = h0_ref[0]

    u = su_ref[0]
    ub = u.astype(BF16)
    n_mm = SSM_WIDTH // 128
    for j in range(n_mm):
        uj = ub[:, j * 128:(j + 1) * 128]
        hr_ref[:, j * 512:(j + 1) * 512] = jnp.dot(uj, bmr_ref[j], preferred_element_type=F32)
        hi_ref[:, j * 512:(j + 1) * 512] = jnp.dot(uj, bmi_ref[j], preferred_element_type=F32)

    for cc in range(SSM_CH // SSM_LANE_CHUNK):
        lanes = slice(cc * SSM_LANE_CHUNK, (cc + 1) * SSM_LANE_CHUNK)
        tabs = [(tab_ref[2 * i, :, lanes], tab_ref[2 * i + 1, :, lanes]) for i in range(4)]

        def group(gi, carry):
            cr, ci = carry
            rows = pl.ds(pl.multiple_of(gi * SUBLANES, SUBLANES), SUBLANES)
            xr, xi = hr_ref[rows, lanes], hi_ref[rows, lanes]
            for lvl, shift in enumerate((1, 2, 4)):
                ar, ai = tabs[lvl]
                sr, si = pltpu.roll(xr, shift, axis=0), pltpu.roll(xi, shift, axis=0)
                xr, xi = xr + (ar * sr - ai * si), xi + (ar * si + ai * sr)
            pr, pi = tabs[3]
            xr, xi = xr + (pr * cr - pi * ci), xi + (pr * ci + pi * cr)
            hr_ref[rows, lanes], hi_ref[rows, lanes] = xr, xi
            return xr[SUBLANES - 1:SUBLANES, :], xi[SUBLANES - 1:SUBLANES, :]

        cr, ci = lax.fori_loop(0, tt // SUBLANES, group, (c_ref[0:1, lanes], c_ref[1:2, lanes]))
        c_ref[0:1, lanes], c_ref[1:2, lanes] = cr, ci

    @pl.when(t == pl.num_programs(1) - 1)
    def _():
        st_ref[0, 0:1, :] = hr_ref[row_last:row_last + 1, :]
        st_ref[0, 1:2, :] = hi_ref[row_last:row_last + 1, :]

    ys = []
    for j in range(n_mm):
        hcat = jnp.concatenate([hr_ref[:, j * 512:(j + 1) * 512], hi_ref[:, j * 512:(j + 1) * 512]], axis=-1)
        ys.append(jnp.dot(hcat.astype(BF16), cm_ref[j], preferred_element_type=F32))
    y = jnp.concatenate(ys, axis=-1) + ds_ref[...] * u
    z = jax.nn.gelu(y)
    out = z * jax.nn.sigmoid(jnp.dot(z.astype(BF16), wg_ref[...], preferred_element_type=F32))
    o_ref[0] = (out * _silu(sz_ref[0])).astype(BF16)


def _ssm(proj3, h0, tab, bmr, bmi, cm, d_skip, w_glu, *, s_valid):
    b, s, _ = proj3.shape
    tt = min(s, 256)
    row_last = (s_valid - 1) % tt
    n_mm = SSM_WIDTH // 128
    kern = functools.partial(_ssm_kernel, tt=tt, row_last=row_last)
    const3 = lambda i, t: (0, 0, 0)
    return pl.pallas_call(
        kern,
        grid=(b, s // tt),
        in_specs=[pl.BlockSpec((1, tt, SSM_WIDTH), lambda i, t: (i, t, C_SU // SSM_WIDTH)),
                  pl.BlockSpec((1, tt, SSM_WIDTH), lambda i, t: (i, t, C_SZ // SSM_WIDTH)),
                  pl.BlockSpec((1, 2, SSM_CH), lambda i, t: (i, 0, 0)),
                  pl.BlockSpec((8, SUBLANES, SSM_CH), const3),
                  pl.BlockSpec((n_mm, 128, 512), const3),
                  pl.BlockSpec((n_mm, 128, 512), const3),
                  pl.BlockSpec((n_mm, 1024, 128), const3),
                  pl.BlockSpec((1, SSM_WIDTH), lambda i, t: (0, 0)),
                  pl.BlockSpec((SSM_WIDTH, SSM_WIDTH), lambda i, t: (0, 0))],
        out_specs=[pl.BlockSpec((1, tt, SSM_WIDTH), lambda i, t: (i, t, 0)),
                   pl.BlockSpec((1, 2, SSM_CH), lambda i, t: (i, 0, 0))],
        out_shape=[jax.ShapeDtypeStruct((b, s, SSM_WIDTH), BF16),
                   jax.ShapeDtypeStruct((b, 2, SSM_CH), F32)],
        scratch_shapes=[pltpu.VMEM((tt, SSM_CH), F32), pltpu.VMEM((tt, SSM_CH), F32),
                        pltpu.VMEM((2, SSM_CH), F32)],
        compiler_params=_params("parallel", "arbitrary"),
        name="ssm",
    )(proj3, proj3, h0, tab, bmr, bmi, cm, d_skip, w_glu)


def _ssm_tables(lam_re, lam_im, log_step, b_re, b_im, c_re, c_im):
    pw_re, pw_im, bbr, bbi = _ssm_disc(lam_re, lam_im, log_step, b_re, b_im)
    pw_re = pw_re.reshape(SUBLANES, SSM_CH)
    pw_im = pw_im.reshape(SUBLANES, SSM_CH)
    row = jnp.arange(SUBLANES)[:, None]
    tabs = []
    for shift in (1, 2, 4):
        keep = row >= shift
        tabs += [jnp.where(keep, pw_re[shift - 1][None], 0.0), jnp.where(keep, pw_im[shift - 1][None], 0.0)]
    tab = jnp.stack(tabs + [pw_re, pw_im])
    eye = jnp.eye(8, dtype=F32)

    def b_blocks(bb):
        x = bb.reshape(SSM_GROUPS // 8, 8, SSM_GROUP_DIM, SSM_STATE)
        return jnp.einsum("jgcn,gh->jgchn", x, eye).reshape(SSM_GROUPS // 8, 128, 512).astype(BF16)

    def c_blocks(c):
        x = c.reshape(SSM_GROUPS // 8, 8, SSM_GROUP_DIM, SSM_STATE)
        return jnp.einsum("jgcn,gh->jgnhc", x, eye).reshape(SSM_GROUPS // 8, 512, 128)

    cm = jnp.concatenate([c_blocks(c_re), -c_blocks(c_im)], axis=1).astype(BF16)
    return tab, b_blocks(bbr), b_blocks(bbi), cm


def _merge_kernel(ap_ref, aa_ref, as_ref, wp_ref, wa_ref, ws_ref, g0_ref, g1_ref, g2_ref, o_ref):
    bp = jnp.dot(ap_ref[...], wp_ref[...], preferred_element_type=F32)
    ba = jnp.dot(aa_ref[...], wa_ref[...], preferred_element_type=F32)
    bs = jnp.dot(as_ref[...], ws_ref[...], preferred_element_type=F32)
    o_ref[...] = (jax.nn.sigmoid(g0_ref[...]) * bp + jax.nn.sigmoid(g1_ref[...]) * ba
                  + jax.nn.sigmoid(g2_ref[...]) * bs).astype(BF16)


def _merge(a_pool, a_att, a_ssm, w_p, w_a, w_s, proj2):
    m = a_pool.shape[0]
    tm = min(m, 512)
    tn = 1024
    act = pl.BlockSpec((tm, 1024), lambda i, n: (i, 0))
    wsp = pl.BlockSpec((1024, tn), lambda i, n: (0, n))
    gate = lambda r: pl.BlockSpec((tm, tn), lambda i, n: (i, (C_MG + r * D_MODEL) // tn + n))
    return pl.pallas_call(
        _merge_kernel,
        grid=(m // tm, D_MODEL // tn),
        in_specs=[act, act, act, wsp, wsp, wsp, gate(0), gate(1), gate(2)],
        out_specs=pl.BlockSpec((tm, tn), lambda i, n: (i, n)),
        out_shape=jax.ShapeDtypeStruct((m, D_MODEL), BF16),
        compiler_params=_params("parallel", "arbitrary"),
        name="merge",
    )(a_pool, a_att, a_ssm, w_p, w_a, w_s, proj2, proj2, proj2)


def _outproj_kernel(m_ref, w_ref, g_ref, x_ref, o_ref):
    y = jnp.dot(m_ref[...], w_ref[...], preferred_element_type=F32)
    r = lax.rsqrt(jnp.mean(y * y, axis=-1, keepdims=True) + EPS)
    o_ref[...] = x_ref[...] + y * r * g_ref[...]


def _outproj(merged, w_out, g_post, x2d):
    m = merged.shape[0]
    tm = min(m, 512)
    row = pl.BlockSpec((tm, D_MODEL), lambda i: (i, 0))
    return pl.pallas_call(
        _outproj_kernel,
        grid=(m // tm,),
        in_specs=[row, pl.BlockSpec((D_MODEL, D_MODEL), lambda i: (0, 0)),
                  pl.BlockSpec((1, D_MODEL), lambda i: (0, 0)), row],
        out_specs=row,
        out_shape=jax.ShapeDtypeStruct((m, D_MODEL), F32),
        compiler_params=_params("parallel"),
        name="outproj",
    )(merged, w_out, g_post, x2d)


def _prep_layer_weights(l, g_pre, g_post, w_in, w_pool, pool_scale, pe_cmp, w_phi, lam_re, lam_im, log_step,
                        b_re, b_im, c_re, c_im, d_skip, w_glu, w_br_pool, w_br_nsa, w_br_ssm, w_out):
    w_cmp = [sum(jnp.pad(w_phi[l, c], ((0, 0), (k * HEAD_DIM, KV_WIDTH - (k + 1) * HEAD_DIM),
                                       (k * HEAD_DIM, KV_WIDTH - (k + 1) * HEAD_DIM)))
                 for k in range(N_KV_HEADS)).astype(BF16) for c in range(2)]
    pe = [jnp.broadcast_to(pe_cmp[l, c][:, None, :], (BLOCK, N_KV_HEADS, HEAD_DIM)).reshape(BLOCK, KV_WIDTH)
          for c in range(2)]
    tab, bmr, bmi, cm = _ssm_tables(lam_re[l], lam_im[l], log_step[l], b_re[l], b_im[l], c_re[l], c_im[l])
    return dict(
        g_pre=g_pre[l].reshape(1, D_MODEL), g_post=g_post[l].reshape(1, D_MODEL),
        layer=l, wt_in=jnp.swapaxes(w_in, 1, 2),
        w_pool=w_pool[l].astype(BF16), pool_scale=pool_scale[l].reshape(1, POOL_WIDTH),
        w_cmp=w_cmp, pe=pe, tab=tab, bmr=bmr, bmi=bmi, cm=cm, d_skip=d_skip[l].reshape(1, SSM_WIDTH),
        w_glu=w_glu[l].astype(BF16), w_br_pool=w_br_pool[l].astype(BF16), w_br_nsa=w_br_nsa[l].astype(BF16),
        w_br_ssm=w_br_ssm[l].astype(BF16), w_out=w_out[l].astype(BF16))


def _layer(x3, lw, *, pos0, s_valid, pool_prefix, ssm_h0, win_prefix, paged):
    b, s, _ = x3.shape
    x2 = x3.reshape(b * s, D_MODEL)
    proj2 = _inproj(x2, lw["g_pre"], lw["wt_in"], lw["layer"], N_MAIN, INPROJ_TILE, _main_src_row)
    gates3 = _inproj(x2, lw["g_pre"], lw["wt_in"], lw["layer"], N_GATE_PAD, N_GATE_PAD,
                     lambda j: _ROW_AG + j * N_GATE_PAD).reshape(b, s, N_GATE_PAD)
    proj3 = proj2.reshape(b, s, N_MAIN)

    prefix16 = jnp.pad(pool_prefix, ((0, 0), (1, 0), (0, 0)))
    a_pool, pool_tail = _pool(proj3, prefix16, lw["w_pool"], lw["pool_scale"], q0=pos0, s_valid=s_valid)
    pool_state = pool_tail[:, 1:]

    kvn = proj3[:, :, C_KV:]
    if paged is None:
        tq = min(s, 256)
        n_cmp = nb_total = nbp = s // BLOCK
        x3k = x3v = proj3.reshape(b * n_cmp, BLOCK, N_MAIN)
        kcol, vcol = KV_COL, KV_COL + 1
    else:
        tq = s
        cache5, layer, page_table = paged
        past_len = page_table.shape[1] * PAGE_SIZE
        n_cmp = past_len // BLOCK
        nb_total = -(-(past_len + s_valid) // BLOCK)
        nbp = -(-nb_total // SUBLANES) * SUBLANES
    if paged is None:
        kc = _compress(x3k, kcol, lw["pe"][0], lw["w_cmp"][0])
        vc = _compress(x3v, vcol, lw["pe"][1], lw["w_cmp"][1])
    else:
        pe2 = jnp.stack([jnp.tile(pe, (BLOCKS_PER_PAGE, 1)) for pe in lw["pe"]])
        xk_rows, xv_rows = _gather_cmp(cache5, layer, page_table, pe2)
        kc = _compress_rows(xk_rows, lw["w_cmp"][0])
        vc = _compress_rows(xv_rows, lw["w_cmp"][1])
    kc = kc.reshape(b, n_cmp, KV_WIDTH)
    vc = vc.reshape(b, n_cmp, KV_WIDTH)
    kc = jnp.pad(kc, ((0, 0), (0, nbp - n_cmp), (0, 0)))
    vct = jnp.swapaxes(jnp.pad(vc, ((0, 0), (0, nbp - n_cmp), (0, 0))), 1, 2)
    o_cmp, sel = _cmpsel(proj3, kc, vct, pos0=pos0, tq=tq, nb_total=nb_total)
    if paged is None:
        assert pos0 == 0
        o_sel = _selattn(proj3, _tcols(proj3, KV_COL + 3), sel, tq=tq)
    else:
        o_sel = _selattn_paged(proj3, cache5, layer, page_table, sel)

    tq_win = min(s, 256)
    n_rows = ((s - tq_win) // KEY_TILE + -(-(WINDOW + tq_win) // KEY_TILE)) * KEY_TILE
    pad_rows = n_rows - WINDOW - s
    k_new, v_new = kvn[:, :, 4 * KV_WIDTH:5 * KV_WIDTH], kvn[:, :, 5 * KV_WIDTH:6 * KV_WIDTH]
    v_new_t = _tcols(proj3, KV_COL + 5)
    if win_prefix is None:
        assert pad_rows == 0 and tq_win == KEY_TILE and WINDOW % KEY_TILE == 0 and pos0 == 0
        o_win = _winattn(proj3, proj3, KV_COL + 4, v_new_t, pos0=pos0, row_pos0=-WINDOW, tq=tq_win,
                         virtual_tiles=WINDOW // KEY_TILE)
    else:
        k_pre = win_prefix[:, :, 0].reshape(b, WINDOW, KV_WIDTH)
        v_pre = win_prefix[:, :, 1].reshape(b, WINDOW, KV_WIDTH)
        kwin = jnp.pad(jnp.concatenate([k_pre, k_new], axis=1), ((0, 0), (0, pad_rows), (0, 0)))
        vwin_t = jnp.pad(jnp.concatenate([jnp.swapaxes(v_pre, 1, 2), v_new_t], axis=2),
                         ((0, 0), (0, 0), (0, pad_rows)))
        o_win = _winattn(proj3, kwin, 0, vwin_t, pos0=pos0, row_pos0=pos0 - WINDOW, tq=tq_win)
    a_att = _attcomb(o_cmp, o_sel, o_win, gates3, proj3)

    a_ssm, ssm_state = _ssm(proj3, ssm_h0.reshape(b, 2, SSM_CH), lw["tab"], lw["bmr"], lw["bmi"], lw["cm"],
                            lw["d_skip"], lw["w_glu"], s_valid=s_valid)

    merged = _merge(a_pool.reshape(b * s, POOL_WIDTH), a_att.reshape(b * s, ATT_WIDTH),
                    a_ssm.reshape(b * s, SSM_WIDTH), lw["w_br_pool"], lw["w_br_nsa"], lw["w_br_ssm"], proj2)
    x_new = _outproj(merged, lw["w_out"], lw["g_post"], x2).reshape(b, s, D_MODEL)

    kv_rows = kvn[:, :s_valid, :4 * KV_WIDTH].reshape(b, s_valid, 4, N_KV_HEADS, HEAD_DIM)
    if win_prefix is None:
        assert s_valid >= WINDOW
        win_k, win_v = k_new[:, s_valid - WINDOW:s_valid], v_new[:, s_valid - WINDOW:s_valid]
    else:
        win_k = jnp.concatenate([k_pre, k_new], axis=1)[:, s_valid:s_valid + WINDOW]
        win_v = jnp.concatenate([v_pre, v_new], axis=1)[:, s_valid:s_valid + WINDOW]
    win_state = jnp.stack([win_k.reshape(b, WINDOW, N_KV_HEADS, HEAD_DIM),
                           win_v.reshape(b, WINDOW, N_KV_HEADS, HEAD_DIM)], axis=2)
    return x_new, kv_rows, win_state, pool_state, ssm_state.reshape(b, 2, SSM_GROUPS, SSM_STATE)


def kernel(x_prompt, x_sample, cache_kv, page_table, state_win_kv, state_pool, state_ssm, g_pre, g_post, w_in, w_pool, pool_scale, pe_cmp, w_phi, lam_re, lam_im, log_step, b_re, b_im, c_re, c_im, d_skip, w_glu, w_br_pool, w_br_nsa, w_br_ssm, w_out):
    depth = w_in.shape[0]
    bp, sp, _ = x_prompt.shape
    bd, sd, _ = x_sample.shape
    n_pool = cache_kv.shape[1]
    past_len = page_table.shape[1] * PAGE_SIZE
    assert state_win_kv.shape[2] == WINDOW and past_len >= WINDOW and sp >= WINDOW
    cache5 = jnp.transpose(cache_kv, (0, 1, 3, 4, 5, 2)).reshape(depth, n_pool, 4, KV_WIDTH, PAGE_SIZE)
    sd_pad = -(-sd // SUBLANES) * SUBLANES
    yp = x_prompt
    ys = jnp.pad(x_sample, ((0, 0), (0, sd_pad - sd), (0, 0)))
    zeros_pool = jnp.zeros((bp, POOL_STATE, POOL_WIDTH), F32)
    zeros_ssm = jnp.zeros((bp, 2, SSM_GROUPS, SSM_STATE), F32)
    outs_p, outs_s = [], []
    for l in range(depth):
        lw = _prep_layer_weights(l, g_pre, g_post, w_in, w_pool, pool_scale, pe_cmp, w_phi, lam_re, lam_im, log_step,
                                 b_re, b_im, c_re, c_im, d_skip, w_glu, w_br_pool, w_br_nsa, w_br_ssm, w_out)
        yp, *rp = _layer(yp, lw, pos0=0, s_valid=sp, pool_prefix=zeros_pool, ssm_h0=zeros_ssm,
                         win_prefix=None, paged=None)
        ys, *rs = _layer(ys, lw, pos0=past_len, s_valid=sd, pool_prefix=state_pool[l], ssm_h0=state_ssm[l],
                         win_prefix=state_win_kv[l], paged=(cache5, l, page_table))
        outs_p.append(rp)
        outs_s.append(rs)
    stack = lambda outs, i: jnp.stack([o[i] for o in outs])
    return (yp, ys[:, :sd], stack(outs_p, 0), stack(outs_s, 0), stack(outs_p, 1), stack(outs_s, 1),
            stack(outs_p, 2), stack(outs_s, 2), stack(outs_p, 3), stack(outs_s, 3))
```

```python
import functools

import jax
import jax.numpy as jnp
from jax import lax
from jax.experimental import pallas as pl
from jax.experimental.pallas import tpu as pltpu

F32 = jnp.float32
BF16 = jnp.bfloat16
I32 = jnp.int32

D_MODEL = 2048
PAGE_SIZE = 128
POOL_WIDTH = D_MODEL // 2
POOL_WINDOWS = (2, 4, 8, 16)
POOL_GROUP_DIM = POOL_WIDTH // len(POOL_WINDOWS)
POOL_STATE = max(POOL_WINDOWS) - 1
N_HEADS = 16
HEAD_DIM = 64
N_KV_HEADS = 4
GQA = N_HEADS // N_KV_HEADS
ATT_WIDTH = N_HEADS * HEAD_DIM
KV_WIDTH = N_KV_HEADS * HEAD_DIM
BLOCK = 64
TOP_N = 16
WINDOW = 512
N_ATT_GATES = 3
SSM_WIDTH = D_MODEL // 2
SSM_GROUP_DIM = 16
SSM_GROUPS = SSM_WIDTH // SSM_GROUP_DIM
SSM_STATE = 64
SSM_CH = SSM_GROUPS * SSM_STATE
N_BRANCH = 3
EPS = 1e-6
NEG = -1e30
FORCE = 1e4
ATT_SCALE = HEAD_DIM ** -0.5
LOG2_E = 1.4426950408889634

C_PU, C_PZ, C_Q, C_AZ, C_SU, C_SZ = (i * 1024 for i in range(6))
C_MG = 6 * 1024
C_KV = C_MG + N_BRANCH * D_MODEL
N_MAIN = C_KV + 6 * KV_WIDTH
N_GATE_PAD = 128
KV_COL = C_KV // KV_WIDTH

VMEM_LIMIT_BYTES = 52 * 1024 * 1024
SUBLANES = 8
KEY_TILE = 256
BLOCKS_PER_TILE = KEY_TILE // BLOCK
BLOCKS_PER_PAGE = PAGE_SIZE // BLOCK
PAGES_PER_STEP = 16
SSM_LANE_CHUNK = 1024
LANES = 128

SSM_MM_GROUPS = 8
SSM_MM_IN = SSM_MM_GROUPS * SSM_GROUP_DIM
SSM_MM_STATES = SSM_MM_GROUPS * SSM_STATE
SSM_N_MM = SSM_GROUPS // SSM_MM_GROUPS

INPROJ_ROWS = 1024
MERGE_ROWS = 512
MERGE_COLS = 1024
OUTPROJ_ROWS = 512
POOL_ROWS = 512
SSM_ROWS = 256
QUERY_TILE = 256
TCOLS_ROWS = 512
COMPRESS_BLOCKS = 128
COMPRESS_ROWS_BLOCKS = 256

NT_DIMS = (((1,), (1,)), ((), ()))


def _params(*sem):
    return pltpu.CompilerParams(dimension_semantics=sem, vmem_limit_bytes=VMEM_LIMIT_BYTES)


def _silu(x):
    return x * jax.nn.sigmoid(x)


def _inproj_kernel(x_ref, g_ref, w_ref, o_ref, h_ref):
    @pl.when(pl.program_id(1) == 0)
    def _():
        x = x_ref[...]
        r = lax.rsqrt(jnp.mean(x * x, axis=-1, keepdims=True) + EPS)
        h_ref[...] = (x * r * g_ref[...]).astype(BF16)

    o_ref[...] = lax.dot_general(h_ref[...], w_ref[0].astype(BF16), NT_DIMS, preferred_element_type=F32)


def _inproj(x2d, g, wt_all, layer, n_out, tn, src_row):
    m, d = x2d.shape
    tm = min(m, INPROJ_ROWS)
    w_spec = pl.BlockSpec((pl.Element(1), pl.Element(tn), pl.Element(d)),
                          lambda i, j: (layer, pl.multiple_of(src_row(j), SUBLANES), 0))
    return pl.pallas_call(
        _inproj_kernel,
        grid=(m // tm, n_out // tn),
        in_specs=[pl.BlockSpec((tm, d), lambda i, j: (i, 0)),
                  pl.BlockSpec((1, d), lambda i, j: (0, 0)),
                  w_spec],
        out_specs=pl.BlockSpec((tm, tn), lambda i, j: (i, j)),
        out_shape=jax.ShapeDtypeStruct((m, n_out), F32),
        scratch_shapes=[pltpu.VMEM((tm, d), BF16)],
        compiler_params=_params("parallel", "arbitrary"),
        name="inproj",
    )(x2d, g, wt_all)


_ROW_KV = 2 * POOL_WIDTH + ATT_WIDTH
_ROW_AG = _ROW_KV + 6 * KV_WIDTH
_ROW_AZ = _ROW_AG + N_HEADS * N_ATT_GATES
INPROJ_TILE = 768


def _main_src_row(j):
    n_head = _ROW_KV // INPROJ_TILE
    n_mid = (C_KV - _ROW_KV) // INPROJ_TILE
    return jnp.where(j < n_head, j * INPROJ_TILE,
                     jnp.where(j < n_head + n_mid, _ROW_AZ + (j - n_head) * INPROJ_TILE,
                               _ROW_KV + (j - n_head - n_mid) * INPROJ_TILE))


def _pool_kernel(pu_ref, pz_ref, pre_ref, wp_ref, sc_ref, o_ref, st_ref, e_ref, *, tt, q0, rows_last):
    t = pl.program_id(1)
    hist = POOL_STATE + 1

    @pl.when(t == 0)
    def _():
        e_ref[0:hist, :] = pre_ref[0]

    @pl.when(t > 0)
    def _():
        e_ref[0:hist, :] = e_ref[tt:tt + hist, :]

    u = pu_ref[0]
    e_ref[hist:hist + tt, :] = u
    pos = q0 + t * tt + lax.broadcasted_iota(I32, (tt, 1), 0)
    ys = []
    for gi, w in enumerate(POOL_WINDOWS):
        lo = gi * POOL_GROUP_DIM
        tot = e_ref[hist:hist + tt, lo:lo + POOL_GROUP_DIM]
        for k in range(1, w):
            tot = tot + e_ref[hist - k:hist - k + tt, lo:lo + POOL_GROUP_DIM]
        cnt = jnp.minimum(pos + 1, w).astype(F32)
        diff = tot / cnt - u[:, lo:lo + POOL_GROUP_DIM]
        ys.append(jnp.dot(diff.astype(BF16), wp_ref[gi], preferred_element_type=F32))
    y = jnp.concatenate(ys, axis=-1) * sc_ref[...]
    o_ref[0] = (y * _silu(pz_ref[0])).astype(BF16)

    @pl.when(t == pl.num_programs(1) - 1)
    def _():
        st_ref[0] = e_ref[rows_last:rows_last + hist, :]


def _pool(proj3, prefix16, w_pool, pool_scale, *, q0, s_valid):
    b, s, _ = proj3.shape
    tt = min(s, POOL_ROWS)
    hist = POOL_STATE + 1
    rows_last = ((s_valid - 1) % tt) + 1
    kern = functools.partial(_pool_kernel, tt=tt, q0=q0, rows_last=rows_last)
    return pl.pallas_call(
        kern,
        grid=(b, s // tt),
        in_specs=[pl.BlockSpec((1, tt, POOL_WIDTH), lambda i, t: (i, t, C_PU // POOL_WIDTH)),
                  pl.BlockSpec((1, tt, POOL_WIDTH), lambda i, t: (i, t, C_PZ // POOL_WIDTH)),
                  pl.BlockSpec((1, hist, POOL_WIDTH), lambda i, t: (i, 0, 0)),
                  pl.BlockSpec((len(POOL_WINDOWS), POOL_GROUP_DIM, POOL_GROUP_DIM), lambda i, t: (0, 0, 0)),
                  pl.BlockSpec((1, POOL_WIDTH), lambda i, t: (0, 0))],
        out_specs=[pl.BlockSpec((1, tt, POOL_WIDTH), lambda i, t: (i, t, 0)),
                   pl.BlockSpec((1, hist, POOL_WIDTH), lambda i, t: (i, 0, 0))],
        out_shape=[jax.ShapeDtypeStruct((b, s, POOL_WIDTH), BF16),
                   jax.ShapeDtypeStruct((b, hist, POOL_WIDTH), F32)],
        scratch_shapes=[pltpu.VMEM((hist + tt, POOL_WIDTH), F32)],
        compiler_params=_params("parallel", "arbitrary"),
        name="pool",
    )(proj3, proj3, prefix16, w_pool, pool_scale)


def _compress_kernel(x_ref, pe_ref, w_ref, o_ref):
    def one_row(l, acc):
        x = x_ref[:, l, :] + pe_ref[pl.ds(l, 1), :]
        return acc + jnp.dot(x.astype(BF16), w_ref[l], preferred_element_type=F32)

    o_ref[...] = lax.fori_loop(0, BLOCK, one_row, jnp.zeros(o_ref.shape, F32))


def _compress(x3, col, pe, w):
    nb = x3.shape[0]
    tm = min(nb, COMPRESS_BLOCKS)
    return pl.pallas_call(
        _compress_kernel,
        grid=(nb // tm,),
        in_specs=[pl.BlockSpec((tm, BLOCK, KV_WIDTH), lambda i: (i, 0, col)),
                  pl.BlockSpec((BLOCK, KV_WIDTH), lambda i: (0, 0)),
                  pl.BlockSpec((BLOCK, KV_WIDTH, KV_WIDTH), lambda i: (0, 0, 0))],
        out_specs=pl.BlockSpec((tm, KV_WIDTH), lambda i: (i, 0)),
        out_shape=jax.ShapeDtypeStruct((nb, KV_WIDTH), F32),
        compiler_params=_params("parallel"),
        name="compress",
    )(x3, pe, w)


def _compress_rows_kernel(x_ref, w_ref, o_ref):
    x = jnp.concatenate([x_ref[l] for l in range(BLOCK)], axis=1)
    o_ref[...] = jnp.dot(x, w_ref[...].reshape(BLOCK * KV_WIDTH, KV_WIDTH), preferred_element_type=F32)


def _compress_rows(xl, w):
    nb = xl.shape[1]
    tm = min(nb, COMPRESS_ROWS_BLOCKS)
    return pl.pallas_call(
        _compress_rows_kernel,
        grid=(nb // tm,),
        in_specs=[pl.BlockSpec((BLOCK, tm, KV_WIDTH), lambda i: (0, i, 0)),
                  pl.BlockSpec((BLOCK, KV_WIDTH, KV_WIDTH), lambda i: (0, 0, 0))],
        out_specs=pl.BlockSpec((tm, KV_WIDTH), lambda i: (i, 0)),
        out_shape=jax.ShapeDtypeStruct((nb, KV_WIDTH), F32),
        compiler_params=_params("parallel"),
        name="compress_rows",
    )(xl, w)


def _page_specs(layer, n_pages, comp_block):
    def spec(i):
        def index(bi, st, pt):
            return (layer, pt[bi, jnp.minimum(st * PAGES_PER_STEP + i, n_pages - 1)], comp_block, 0, 0)
        return pl.BlockSpec((1, 1, 2, KV_WIDTH, PAGE_SIZE), index)
    return [spec(i) for i in range(PAGES_PER_STEP)]


def _sublane_transpose8(a):
    sub = lax.broadcasted_iota(I32, a[0].shape, 0)
    for shift in (4, 2, 1):
        low = (sub & shift) == 0
        nxt = list(a)
        for j in range(SUBLANES):
            if j & shift == 0:
                nxt[j] = jnp.where(low, a[j], pltpu.roll(a[j + shift], shift, axis=0))
                nxt[j + shift] = jnp.where(low, pltpu.roll(a[j], SUBLANES - shift, axis=0), a[j + shift])
        a = nxt
    return a


def _gather_cmp_kernel(pt_ref, *refs):
    del pt_ref
    pages, (pe_ref, xk_ref, xv_ref) = refs[:PAGES_PER_STEP], refs[PAGES_PER_STEP:]
    n_groups = PAGES_PER_STEP * BLOCKS_PER_PAGE // SUBLANES
    for comp, dst in enumerate((xk_ref, xv_ref)):
        rows = [page[0, 0, comp].T + pe_ref[comp] for page in pages]
        for m in range(BLOCK // SUBLANES):
            per_group = []
            for grp in range(n_groups):
                pieces = []
                for jj in range(SUBLANES):
                    j = grp * SUBLANES + jj
                    lo = (j % BLOCKS_PER_PAGE) * BLOCK + m * SUBLANES
                    pieces.append(rows[j // BLOCKS_PER_PAGE][lo:lo + SUBLANES, :])
                per_group.append(_sublane_transpose8(pieces))
            for s in range(SUBLANES):
                dst[m * SUBLANES + s] = jnp.concatenate([g[s] for g in per_group], axis=0).astype(BF16)


def _gather_cmp(cache5, layer, page_table, pe2):
    b, n_pages = page_table.shape
    n_steps = n_pages // PAGES_PER_STEP
    blocks_per_step = PAGES_PER_STEP * BLOCKS_PER_PAGE
    assert blocks_per_step % (2 * SUBLANES) == 0
    out_spec = pl.BlockSpec((BLOCK, blocks_per_step, KV_WIDTH), lambda bi, st, pt: (0, bi * n_steps + st, 0))
    out = jax.ShapeDtypeStruct((BLOCK, b * n_pages * BLOCKS_PER_PAGE, KV_WIDTH), BF16)
    return pl.pallas_call(
        _gather_cmp_kernel,
        grid_spec=pltpu.PrefetchScalarGridSpec(
            num_scalar_prefetch=1, grid=(b, n_steps),
            in_specs=_page_specs(layer, n_pages, 0) + [
                pl.BlockSpec((2, PAGE_SIZE, KV_WIDTH), lambda bi, st, pt: (0, 0, 0))],
            out_specs=[out_spec, out_spec]),
        out_shape=[out, out],
        compiler_params=_params("parallel", "arbitrary"),
        name="gather_cmp",
    )(page_table, *([cache5] * PAGES_PER_STEP), pe2)


def _cmpsel_kernel(q_ref, kc_ref, vct_ref, ocmp_ref, sel_ref, s_ref, *, tq, nbp, pos0, n_sel):
    j = pl.program_id(1)
    qs = (q_ref[0] * ATT_SCALE).astype(BF16)
    kc = kc_ref[0].astype(BF16)
    vct = vct_ref[0].astype(BF16)
    nidx = lax.broadcasted_iota(I32, (nbp, tq), 0)
    pos = pos0 + j * tq + lax.broadcasted_iota(I32, (nbp, tq), 1)
    cvalid = nidx < lax.shift_right_logical(pos + 1, 6)
    cur = lax.shift_right_logical(pos, 6)
    forced = (nidx == 0) | (nidx == cur) | (nidx == cur - 1)
    outs = []
    for k in range(N_KV_HEADS):
        ksl = slice(k * HEAD_DIM, (k + 1) * HEAD_DIM)
        imp = jnp.zeros((nbp, tq), F32)
        for g in range(GQA):
            h = k * GQA + g
            sc = lax.dot_general(kc[:, ksl], qs[:, h * HEAD_DIM:(h + 1) * HEAD_DIM], NT_DIMS,
                                 preferred_element_type=F32)
            sc = jnp.where(cvalid, sc, NEG)
            e = jnp.exp(sc - jnp.max(sc, axis=0, keepdims=True))
            pc = jnp.where(cvalid, e / jnp.sum(e, axis=0, keepdims=True), 0.0)
            imp = imp + pc
            outs.append(jnp.dot(vct[ksl, :], pc.astype(BF16), preferred_element_type=F32))
        s_ref[:, k * tq:(k + 1) * tq] = jnp.where(forced, FORCE, jnp.where(nidx < cur, imp, NEG))
    ocmp_ref[0] = jnp.concatenate(outs, axis=0)

    lanes = N_KV_HEADS * tq
    chunk = min(lanes, LANES)
    rows = lax.broadcasted_iota(I32, (nbp, chunk), 0)
    for c in range(lanes // chunk):
        score = s_ref[:, c * chunk:(c + 1) * chunk]
        work, sel = score, jnp.zeros_like(score)
        for _ in range(n_sel):
            top = jnp.max(work, axis=0, keepdims=True)
            first = jnp.min(jnp.where(work == top, rows, nbp), axis=0, keepdims=True)
            hit = rows == first
            work, sel = jnp.where(hit, -jnp.inf, work), jnp.where(hit, 1.0, sel)
        sel_ref[0, :, c * chunk:(c + 1) * chunk] = jnp.where(score > NEG / 2, sel, 0.0)


def _cmpsel(proj3, kc, vct, *, pos0, tq, nb_total):
    b, s, _ = proj3.shape
    nbp = kc.shape[1]
    kern = functools.partial(_cmpsel_kernel, tq=tq, nbp=nbp, pos0=pos0, n_sel=min(TOP_N, nb_total))
    lanes = N_KV_HEADS * tq
    return pl.pallas_call(
        kern,
        grid=(b, s // tq),
        in_specs=[pl.BlockSpec((1, tq, ATT_WIDTH), lambda i, j: (i, j, C_Q // ATT_WIDTH)),
                  pl.BlockSpec((1, nbp, KV_WIDTH), lambda i, j: (i, 0, 0)),
                  pl.BlockSpec((1, KV_WIDTH, nbp), lambda i, j: (i, 0, 0))],
        out_specs=[pl.BlockSpec((1, ATT_WIDTH, tq), lambda i, j: (i, 0, j)),
                   pl.BlockSpec((1, nbp, lanes), lambda i, j: (i, 0, j))],
        out_shape=[jax.ShapeDtypeStruct((b, ATT_WIDTH, s), F32),
                   jax.ShapeDtypeStruct((b, nbp, N_KV_HEADS * s), F32)],
        scratch_shapes=[pltpu.VMEM((nbp, lanes), F32)],
        compiler_params=_params("parallel", "parallel"),
        name="cmpsel",
    )(proj3, kc, vct)


def _attn_init(q_ref, qs_ref, m_ref, acc_ref, tq):
    qs = (q_ref[0] * (ATT_SCALE * LOG2_E)).astype(BF16)
    for k in range(N_KV_HEADS):
        for g in range(GQA):
            h = k * GQA + g
            qs_ref[k, g * tq:(g + 1) * tq, :] = qs[:, h * HEAD_DIM:(h + 1) * HEAD_DIM]
    m_ref[...] = jnp.full_like(m_ref, NEG)
    acc_ref[...] = jnp.zeros_like(acc_ref)


def _attn_step(k, lanes, k_tile, vt_tile, block_rows, ok, qs_ref, m_ref, acc_ref):
    ksl = slice(k * HEAD_DIM, (k + 1) * HEAD_DIM)
    s = lax.dot_general(k_tile[:, ksl], qs_ref[k, lanes, :], NT_DIMS, preferred_element_type=F32)
    if block_rows is not None:
        s = jnp.concatenate([jnp.where(row > 0.5, s[i * BLOCK:(i + 1) * BLOCK], NEG)
                             for i, row in enumerate(block_rows)], axis=0)
    if ok is not None:
        w = ok.shape[1]
        s = jnp.concatenate([jnp.where(ok, s[:, i * w:(i + 1) * w], NEG) for i in range(s.shape[1] // w)], axis=1)
    m_old = m_ref[k, :, lanes]
    m_new = jnp.maximum(m_old, jnp.max(s, axis=0, keepdims=True))
    p = jnp.exp2(s - m_new)
    alpha = jnp.exp2(m_old - m_new)
    m_ref[k, :, lanes] = m_new
    vt_ones = jnp.concatenate([vt_tile[ksl, :], jnp.ones((ONES_ROWS, vt_tile.shape[1]), BF16)], axis=0)
    acc_ref[k, :, lanes] = (alpha * acc_ref[k, :, lanes]
                            + jnp.dot(vt_ones, p.astype(BF16), preferred_element_type=F32))


def _attn_finish(o_ref, acc_ref, tq):
    outs = []
    for k in range(N_KV_HEADS):
        o = acc_ref[k, 0:HEAD_DIM, :] / acc_ref[k, HEAD_DIM:HEAD_DIM + 1, :]
        for g in range(GQA):
            outs.append(o[:, g * tq:(g + 1) * tq])
    o_ref[0] = jnp.concatenate(outs, axis=0)


ONES_ROWS = 16


def _attn_scratch(tq):
    r = GQA * tq
    return [pltpu.VMEM((N_KV_HEADS, r, HEAD_DIM), BF16),
            pltpu.VMEM((N_KV_HEADS, 1, r), F32),
            pltpu.VMEM((N_KV_HEADS, HEAD_DIM + ONES_ROWS, r), F32)]


def _lane_query_pos(shape, tq, first):
    return first + (lax.broadcasted_iota(I32, shape, 1) & (tq - 1))


def _selattn_kernel(jt_ref, kt_ref, q_ref, k_ref, vt_ref, sel_ref, o_ref, qs_ref, m_ref, acc_ref, *, tq):
    step = pl.program_id(1)
    j = jt_ref[step]
    kt = kt_ref[step]

    @pl.when(kt == 0)
    def _():
        _attn_init(q_ref, qs_ref, m_ref, acc_ref, tq)

    def key_tile(diagonal):
        k_tile = k_ref[0].astype(BF16)
        vt_tile = vt_ref[0].astype(BF16)
        r = GQA * tq
        causal = None
        if diagonal:
            causal = (lax.broadcasted_iota(I32, (KEY_TILE, tq), 0) <= lax.broadcasted_iota(I32, (KEY_TILE, tq), 1))
        for k in range(N_KV_HEADS):
            rows = [jnp.concatenate([sel_ref[0, pl.ds(kt * BLOCKS_PER_TILE + i, 1), k * tq:(k + 1) * tq]] * GQA,
                                    axis=1) for i in range(BLOCKS_PER_TILE)]
            _attn_step(k, slice(0, r), k_tile, vt_tile, rows, causal, qs_ref, m_ref, acc_ref)

    @pl.when(kt < j)
    def _():
        key_tile(False)

    @pl.when(kt == j)
    def _():
        key_tile(True)
        _attn_finish(o_ref, acc_ref, tq)


def _selattn(proj3, vt, sel, *, tq):
    b, s, _ = proj3.shape
    assert tq == KEY_TILE
    nbp = sel.shape[1]
    pairs = [(j, kt) for j in range(s // tq) for kt in range(j + 1)]
    jt = jnp.asarray([p[0] for p in pairs], I32)
    ktt = jnp.asarray([p[1] for p in pairs], I32)
    kern = functools.partial(_selattn_kernel, tq=tq)
    return pl.pallas_call(
        kern,
        grid_spec=pltpu.PrefetchScalarGridSpec(
            num_scalar_prefetch=2, grid=(b, len(pairs)),
            in_specs=[pl.BlockSpec((1, tq, ATT_WIDTH), lambda i, p, jt, kt: (i, jt[p], C_Q // ATT_WIDTH)),
                      pl.BlockSpec((1, KEY_TILE, KV_WIDTH), lambda i, p, jt, kt: (i, kt[p], KV_COL + 2)),
                      pl.BlockSpec((1, KV_WIDTH, KEY_TILE), lambda i, p, jt, kt: (i, 0, kt[p])),
                      pl.BlockSpec((1, nbp, N_KV_HEADS * tq), lambda i, p, jt, kt: (i, 0, jt[p]))],
            out_specs=pl.BlockSpec((1, ATT_WIDTH, tq), lambda i, p, jt, kt: (i, 0, jt[p])),
            scratch_shapes=_attn_scratch(tq)),
        out_shape=jax.ShapeDtypeStruct((b, ATT_WIDTH, s), F32),
        compiler_params=_params("parallel", "arbitrary"),
        name="selattn",
    )(jt, ktt, proj3, proj3, vt, sel)


def _winattn_kernel(q_ref, k_ref, vt_ref, o_ref, qs_ref, m_ref, acc_ref, *, tq, pos0, row_pos0, n_sub):
    j = pl.program_id(1)
    t = pl.program_id(2)
    r = GQA * tq

    @pl.when(t == 0)
    def _():
        _attn_init(q_ref, qs_ref, m_ref, acc_ref, tq)

    tile = (j * tq) // KEY_TILE + (n_sub - 1 - t)

    def run(ok):
        k_tile = k_ref[0].astype(BF16)
        vt_tile = vt_ref[0].astype(BF16)
        for k in range(N_KV_HEADS):
            _attn_step(k, slice(0, r), k_tile, vt_tile, None, ok, qs_ref, m_ref, acc_ref)

    aligned = (tq == KEY_TILE and WINDOW % KEY_TILE == 0 and row_pos0 % KEY_TILE == 0
               and (pos0 - row_pos0) % KEY_TILE == 0)
    if aligned:
        key_i = lax.broadcasted_iota(I32, (KEY_TILE, tq), 0)
        qry_i = lax.broadcasted_iota(I32, (KEY_TILE, tq), 1)
        in_range = row_pos0 + tile * KEY_TILE >= 0

        @pl.when(t == 0)
        def _():
            run(key_i <= qry_i)

        @pl.when((t > 0) & (t < n_sub - 1) & in_range)
        def _():
            run(None)

        @pl.when((t == n_sub - 1) & in_range)
        def _():
            run(key_i > qry_i)
    else:
        kp = row_pos0 + tile * KEY_TILE + lax.broadcasted_iota(I32, (KEY_TILE, r), 0)
        pq = _lane_query_pos((KEY_TILE, r), tq, pos0 + j * tq)
        run((kp >= 0) & (kp <= pq) & (kp > pq - WINDOW))

    @pl.when(t == n_sub - 1)
    def _():
        _attn_finish(o_ref, acc_ref, tq)


def _winattn(proj3, kwin, kcol, vwin_t, *, pos0, row_pos0, tq, virtual_tiles=0):
    b, s, _ = proj3.shape
    assert tq & (tq - 1) == 0 and (tq % KEY_TILE == 0 or s == tq)
    n_sub = -(-(WINDOW + tq) // KEY_TILE)
    assert kwin.shape[1] >= ((s - tq) // KEY_TILE + n_sub - virtual_tiles) * KEY_TILE
    assert virtual_tiles == 0 or (tq == KEY_TILE and row_pos0 + virtual_tiles * KEY_TILE == 0)

    def tile(j, t):
        return jnp.maximum((j * tq) // KEY_TILE + (n_sub - 1 - t) - virtual_tiles, 0)

    kern = functools.partial(_winattn_kernel, tq=tq, pos0=pos0, row_pos0=row_pos0, n_sub=n_sub)
    return pl.pallas_call(
        kern,
        grid=(b, s // tq, n_sub),
        in_specs=[pl.BlockSpec((1, tq, ATT_WIDTH), lambda i, j, t: (i, j, C_Q // ATT_WIDTH)),
                  pl.BlockSpec((1, KEY_TILE, KV_WIDTH), lambda i, j, t: (i, tile(j, t), kcol)),
                  pl.BlockSpec((1, KV_WIDTH, KEY_TILE), lambda i, j, t: (i, 0, tile(j, t)))],
        out_specs=pl.BlockSpec((1, ATT_WIDTH, tq), lambda i, j, t: (i, 0, j)),
        out_shape=jax.ShapeDtypeStruct((b, ATT_WIDTH, s), F32),
        scratch_shapes=_attn_scratch(tq),
        compiler_params=_params("parallel", "parallel", "arbitrary"),
        name="winattn",
    )(proj3, kwin, vwin_t)


def _selattn_paged_kernel(pt_ref, *refs, tq, n_steps):
    del pt_ref
    pages = refs[:PAGES_PER_STEP]
    q_ref, new_ref, sel_ref, o_ref, qbd_ref, selr_ref, m_ref, l_ref, acc_ref = refs[PAGES_PER_STEP:]
    st = pl.program_id(1)
    rows = N_HEADS * tq

    def flash(s, ok, v, v_is_transposed):
        s = jnp.where(ok, s, NEG)
        m_old = m_ref[...]
        m_new = jnp.maximum(m_old, jnp.max(s, axis=-1, keepdims=True))
        p = jnp.exp(s - m_new)
        alpha = jnp.exp(m_old - m_new)
        l_ref[...] = alpha * l_ref[...] + jnp.sum(p, axis=-1, keepdims=True)
        m_ref[...] = m_new
        if v_is_transposed:
            pv = lax.dot_general(p.astype(BF16), v, NT_DIMS, preferred_element_type=F32)
        else:
            pv = jnp.dot(p.astype(BF16), v, preferred_element_type=F32)
        acc_ref[...] = alpha * acc_ref[...] + pv

    @pl.when(st == 0)
    def _():
        qs = (q_ref[0] * ATT_SCALE).astype(BF16)
        qbd_ref[...] = jnp.zeros_like(qbd_ref)
        for k in range(N_KV_HEADS):
            for g in range(GQA):
                h = k * GQA + g
                qbd_ref[h * tq:(h + 1) * tq, k * HEAD_DIM:(k + 1) * HEAD_DIM] = qs[:, h * HEAD_DIM:(h + 1) * HEAD_DIM]
                selr_ref[:, h * tq:(h + 1) * tq] = sel_ref[0, :, k * tq:(k + 1) * tq]
        m_ref[...] = jnp.full_like(m_ref, NEG)
        l_ref[...] = jnp.zeros_like(l_ref)
        acc_ref[...] = jnp.zeros_like(acc_ref)

    @pl.when(st < n_steps)
    def _():
        blk = st * (PAGES_PER_STEP * BLOCKS_PER_PAGE)
        chosen = jnp.concatenate(
            [jnp.broadcast_to(selr_ref[pl.ds(blk + c, 1), :], (BLOCK, rows))
             for c in range(PAGES_PER_STEP * BLOCKS_PER_PAGE)], axis=0)
        k_t = jnp.concatenate([page[0, 0, 0] for page in pages], axis=1).astype(BF16)
        v_t = jnp.concatenate([page[0, 0, 1] for page in pages], axis=1).astype(BF16)
        s = jnp.dot(qbd_ref[...], k_t, preferred_element_type=F32)
        flash(s, chosen.T > 0.5, v_t, True)

    @pl.when(st == n_steps)
    def _():
        new = new_ref[0]
        k_new = new[:, 2 * KV_WIDTH:3 * KV_WIDTH].astype(BF16)
        v_new = new[:, 3 * KV_WIDTH:4 * KV_WIDTH].astype(BF16)
        s = lax.dot_general(qbd_ref[...], k_new, NT_DIMS, preferred_element_type=F32)
        blk = n_steps * PAGES_PER_STEP * BLOCKS_PER_PAGE
        chosen = jnp.broadcast_to(selr_ref[pl.ds(blk, 1), :], (tq, rows)).T
        causal = (lax.broadcasted_iota(I32, (rows, tq), 1)
                  <= (lax.broadcasted_iota(I32, (rows, tq), 0) & (tq - 1)))
        flash(s, (chosen > 0.5) & causal, v_new, False)
        o = acc_ref[...] / l_ref[...]
        outs = []
        for k in range(N_KV_HEADS):
            for g in range(GQA):
                h = k * GQA + g
                outs.append(o[h * tq:(h + 1) * tq, k * HEAD_DIM:(k + 1) * HEAD_DIM].T)
        o_ref[0] = jnp.concatenate(outs, axis=0)


def _selattn_paged(proj3, cache5, layer, page_table, sel):
    b, tq, _ = proj3.shape
    _, n_pages = page_table.shape
    assert tq & (tq - 1) == 0 and tq <= BLOCK and n_pages % PAGES_PER_STEP == 0
    n_steps = n_pages // PAGES_PER_STEP
    nbp = sel.shape[1]
    rows = N_HEADS * tq
    kern = functools.partial(_selattn_paged_kernel, tq=tq, n_steps=n_steps)
    return pl.pallas_call(
        kern,
        grid_spec=pltpu.PrefetchScalarGridSpec(
            num_scalar_prefetch=1, grid=(b, n_steps + 1),
            in_specs=_page_specs(layer, n_pages, 1) + [
                pl.BlockSpec((1, tq, ATT_WIDTH), lambda bi, st, pt: (bi, 0, C_Q // ATT_WIDTH)),
                pl.BlockSpec((1, tq, 4 * KV_WIDTH), lambda bi, st, pt: (bi, 0, C_KV // (4 * KV_WIDTH))),
                pl.BlockSpec((1, nbp, N_KV_HEADS * tq), lambda bi, st, pt: (bi, 0, 0))],
            out_specs=pl.BlockSpec((1, ATT_WIDTH, tq), lambda bi, st, pt: (bi, 0, 0)),
            scratch_shapes=[pltpu.VMEM((rows, KV_WIDTH), BF16), pltpu.VMEM((nbp, rows), F32),
                            pltpu.VMEM((rows, 1), F32), pltpu.VMEM((rows, 1), F32),
                            pltpu.VMEM((rows, KV_WIDTH), F32)]),
        out_shape=jax.ShapeDtypeStruct((b, ATT_WIDTH, tq), F32),
        compiler_params=_params("parallel", "arbitrary"),
        name="selattn_paged",
    )(page_table, *([cache5] * PAGES_PER_STEP), proj3, proj3, sel)


def _tcols_kernel(x_ref, o_ref):
    o_ref[0] = x_ref[0].T


def _tcols(proj3, col):
    b, s, _ = proj3.shape
    ts = min(s, TCOLS_ROWS)
    return pl.pallas_call(
        _tcols_kernel,
        grid=(b, s // ts),
        in_specs=[pl.BlockSpec((1, ts, KV_WIDTH), lambda i, t: (i, t, col))],
        out_specs=pl.BlockSpec((1, KV_WIDTH, ts), lambda i, t: (i, 0, t)),
        out_shape=jax.ShapeDtypeStruct((b, KV_WIDTH, s), F32),
        compiler_params=_params("parallel", "parallel"),
        name="tcols",
    )(proj3)


def _attcomb_kernel(oc_ref, os_ref, ow_ref, ag_ref, az_ref, o_ref):
    gate_t = jax.nn.sigmoid(ag_ref[0]).T
    oc, osel, ow = oc_ref[0], os_ref[0], ow_ref[0]
    outs = []
    for h in range(N_HEADS):
        sl = slice(h * HEAD_DIM, (h + 1) * HEAD_DIM)
        c = h * N_ATT_GATES
        outs.append(gate_t[c:c + 1, :] * oc[sl, :] + gate_t[c + 1:c + 2, :] * osel[sl, :]
                    + gate_t[c + 2:c + 3, :] * ow[sl, :])
    o_ref[0] = (jnp.concatenate(outs, axis=0).T * _silu(az_ref[0])).astype(BF16)


def _attcomb(ocmp_t, osel_t, owin_t, gates3, proj3):
    b, s, _ = proj3.shape
    tt = min(s, QUERY_TILE)
    att_t = pl.BlockSpec((1, ATT_WIDTH, tt), lambda i, t: (i, 0, t))
    return pl.pallas_call(
        _attcomb_kernel,
        grid=(b, s // tt),
        in_specs=[att_t, att_t, att_t,
                  pl.BlockSpec((1, tt, N_GATE_PAD), lambda i, t: (i, t, 0)),
                  pl.BlockSpec((1, tt, ATT_WIDTH), lambda i, t: (i, t, C_AZ // ATT_WIDTH))],
        out_specs=pl.BlockSpec((1, tt, ATT_WIDTH), lambda i, t: (i, t, 0)),
        out_shape=jax.ShapeDtypeStruct((b, s, ATT_WIDTH), BF16),
        compiler_params=_params("parallel", "parallel"),
        name="attcomb",
    )(ocmp_t, osel_t, owin_t, gates3, proj3)


def _ssm_disc_kernel(lr_ref, li_ref, ls_ref, brt_ref, bit_ref, pr_ref, pi_ref, bbr_ref, bbi_ref):
    lr, li = lr_ref[...], li_ref[...]
    dt = jnp.exp(ls_ref[...])
    mag = jnp.exp(lr * dt)
    ab_re, ab_im = mag * jnp.cos(li * dt), mag * jnp.sin(li * dt)
    den = lr * lr + li * li
    co_re = ((ab_re - 1.0) * lr + ab_im * li) / den
    co_im = (ab_im * lr - (ab_re - 1.0) * li) / den
    brt, bit = brt_ref[...], bit_ref[...]
    bbr_ref[...] = co_re[:, None, :] * brt - co_im[:, None, :] * bit
    bbi_ref[...] = co_re[:, None, :] * bit + co_im[:, None, :] * brt
    pr, pi = ab_re, ab_im
    pr_ref[0], pi_ref[0] = pr, pi
    for r in range(1, SUBLANES):
        pr, pi = pr * ab_re - pi * ab_im, pr * ab_im + pi * ab_re
        pr_ref[r], pi_ref[r] = pr, pi


def _ssm_disc(lam_re, lam_im, log_step, b_re, b_im):
    g, n = lam_re.shape
    brt = jnp.swapaxes(b_re, 1, 2)
    bit = jnp.swapaxes(b_im, 1, 2)
    pw = jax.ShapeDtypeStruct((SUBLANES, g, n), F32)
    bb = jax.ShapeDtypeStruct(brt.shape, F32)
    return pl.pallas_call(_ssm_disc_kernel, out_shape=[pw, pw, bb, bb], name="ssm_disc")(
        lam_re, lam_im, log_step.reshape(g, 1), brt, bit)


def _ssm_kernel(su_ref, sz_ref, h0_ref, tab_ref, bmr_ref, bmi_ref, cm_ref, ds_ref, wg_ref,
                o_ref, st_ref, hr_ref, hi_ref, c_ref, *, tt, row_last):
    t = pl.program_id(1)

    @pl.when(t == 0)
    def _():
        c_ref[...] = h0_ref[0]

    u = su_ref[0]
    ub = u.astype(BF16)
    for j in range(SSM_N_MM):
        uj = ub[:, j * SSM_MM_IN:(j + 1) * SSM_MM_IN]
        states = slice(j * SSM_MM_STATES, (j + 1) * SSM_MM_STATES)
        hr_ref[:, states] = jnp.dot(uj, bmr_ref[j], preferred_element_type=F32)
        hi_ref[:, states] = jnp.dot(uj, bmi_ref[j], preferred_element_type=F32)

    for cc in range(SSM_CH // SSM_LANE_CHUNK):
        lanes = slice(cc * SSM_LANE_CHUNK, (cc + 1) * SSM_LANE_CHUNK)
        tabs = [(tab_ref[2 * i, :, lanes], tab_ref[2 * i + 1, :, lanes]) for i in range(4)]

        def group(gi, carry):
            cr, ci = carry
            rows = pl.ds(pl.multiple_of(gi * SUBLANES, SUBLANES), SUBLANES)
            xr, xi = hr_ref[rows, lanes], hi_ref[rows, lanes]
            for lvl, shift in enumerate((1, 2, 4)):
                ar, ai = tabs[lvl]
                sr, si = pltpu.roll(xr, shift, axis=0), pltpu.roll(xi, shift, axis=0)
                xr, xi = xr + (ar * sr - ai * si), xi + (ar * si + ai * sr)
            pr, pi = tabs[3]
            xr, xi = xr + (pr * cr - pi * ci), xi + (pr * ci + pi * cr)
            hr_ref[rows, lanes], hi_ref[rows, lanes] = xr, xi
            return xr[SUBLANES - 1:SUBLANES, :], xi[SUBLANES - 1:SUBLANES, :]

        cr, ci = lax.fori_loop(0, tt // SUBLANES, group, (c_ref[0:1, lanes], c_ref[1:2, lanes]))
        c_ref[0:1, lanes], c_ref[1:2, lanes] = cr, ci

    @pl.when(t == pl.num_programs(1) - 1)
    def _():
        st_ref[0, 0:1, :] = hr_ref[row_last:row_last + 1, :]
        st_ref[0, 1:2, :] = hi_ref[row_last:row_last + 1, :]

    ys = []
    for j in range(SSM_N_MM):
        states = slice(j * SSM_MM_STATES, (j + 1) * SSM_MM_STATES)
        hcat = jnp.concatenate([hr_ref[:, states], hi_ref[:, states]], axis=-1)
        ys.append(jnp.dot(hcat.astype(BF16), cm_ref[j], preferred_element_type=F32))
    y = jnp.concatenate(ys, axis=-1) + ds_ref[...] * u
    z = jax.nn.gelu(y)
    out = z * jax.nn.sigmoid(jnp.dot(z.astype(BF16), wg_ref[...], preferred_element_type=F32))
    o_ref[0] = (out * _silu(sz_ref[0])).astype(BF16)


def _ssm(proj3, h0, tab, bmr, bmi, cm, d_skip, w_glu, *, s_valid):
    b, s, _ = proj3.shape
    tt = min(s, SSM_ROWS)
    row_last = (s_valid - 1) % tt
    n_mm = SSM_N_MM
    kern = functools.partial(_ssm_kernel, tt=tt, row_last=row_last)
    const3 = lambda i, t: (0, 0, 0)
    return pl.pallas_call(
        kern,
        grid=(b, s // tt),
        in_specs=[pl.BlockSpec((1, tt, SSM_WIDTH), lambda i, t: (i, t, C_SU // SSM_WIDTH)),
                  pl.BlockSpec((1, tt, SSM_WIDTH), lambda i, t: (i, t, C_SZ // SSM_WIDTH)),
                  pl.BlockSpec((1, 2, SSM_CH), lambda i, t: (i, 0, 0)),
                  pl.BlockSpec(tab.shape, const3),
                  pl.BlockSpec((n_mm, SSM_MM_IN, SSM_MM_STATES), const3),
                  pl.BlockSpec((n_mm, SSM_MM_IN, SSM_MM_STATES), const3),
                  pl.BlockSpec((n_mm, 2 * SSM_MM_STATES, SSM_MM_IN), const3),
                  pl.BlockSpec((1, SSM_WIDTH), lambda i, t: (0, 0)),
                  pl.BlockSpec((SSM_WIDTH, SSM_WIDTH), lambda i, t: (0, 0))],
        out_specs=[pl.BlockSpec((1, tt, SSM_WIDTH), lambda i, t: (i, t, 0)),
                   pl.BlockSpec((1, 2, SSM_CH), lambda i, t: (i, 0, 0))],
        out_shape=[jax.ShapeDtypeStruct((b, s, SSM_WIDTH), BF16),
                   jax.ShapeDtypeStruct((b, 2, SSM_CH), F32)],
        scratch_shapes=[pltpu.VMEM((tt, SSM_CH), F32), pltpu.VMEM((tt, SSM_CH), F32),
                        pltpu.VMEM((2, SSM_CH), F32)],
        compiler_params=_params("parallel", "arbitrary"),
        name="ssm",
    )(proj3, proj3, h0, tab, bmr, bmi, cm, d_skip, w_glu)


def _ssm_tables(lam_re, lam_im, log_step, b_re, b_im, c_re, c_im):
    pw_re, pw_im, bbr, bbi = _ssm_disc(lam_re, lam_im, log_step, b_re, b_im)
    pw_re = pw_re.reshape(SUBLANES, SSM_CH)
    pw_im = pw_im.reshape(SUBLANES, SSM_CH)
    row = jnp.arange(SUBLANES)[:, None]
    tabs = []
    for shift in (1, 2, 4):
        keep = row >= shift
        tabs += [jnp.where(keep, pw_re[shift - 1][None], 0.0), jnp.where(keep, pw_im[shift - 1][None], 0.0)]
    tab = jnp.stack(tabs + [pw_re, pw_im])
    eye = jnp.eye(SSM_MM_GROUPS, dtype=F32)

    def b_blocks(bb):
        x = bb.reshape(SSM_N_MM, SSM_MM_GROUPS, SSM_GROUP_DIM, SSM_STATE)
        return jnp.einsum("jgcn,gh->jgchn", x, eye).reshape(SSM_N_MM, SSM_MM_IN, SSM_MM_STATES).astype(BF16)

    def c_blocks(c):
        x = c.reshape(SSM_N_MM, SSM_MM_GROUPS, SSM_GROUP_DIM, SSM_STATE)
        return jnp.einsum("jgcn,gh->jgnhc", x, eye).reshape(SSM_N_MM, SSM_MM_STATES, SSM_MM_IN)

    cm = jnp.concatenate([c_blocks(c_re), -c_blocks(c_im)], axis=1).astype(BF16)
    return tab, b_blocks(bbr), b_blocks(bbi), cm


def _merge_kernel(ap_ref, aa_ref, as_ref, wp_ref, wa_ref, ws_ref, g0_ref, g1_ref, g2_ref, o_ref):
    bp = jnp.dot(ap_ref[...], wp_ref[...], preferred_element_type=F32)
    ba = jnp.dot(aa_ref[...], wa_ref[...], preferred_element_type=F32)
    bs = jnp.dot(as_ref[...], ws_ref[...], preferred_element_type=F32)
    o_ref[...] = (jax.nn.sigmoid(g0_ref[...]) * bp + jax.nn.sigmoid(g1_ref[...]) * ba
                  + jax.nn.sigmoid(g2_ref[...]) * bs).astype(BF16)


def _merge(a_pool, a_att, a_ssm, w_p, w_a, w_s, proj2):
    m = a_pool.shape[0]
    tm = min(m, MERGE_ROWS)
    tn = MERGE_COLS
    act = pl.BlockSpec((tm, POOL_WIDTH), lambda i, n: (i, 0))
    wsp = pl.BlockSpec((POOL_WIDTH, tn), lambda i, n: (0, n))
    gate = lambda r: pl.BlockSpec((tm, tn), lambda i, n: (i, (C_MG + r * D_MODEL) // tn + n))
    return pl.pallas_call(
        _merge_kernel,
        grid=(m // tm, D_MODEL // tn),
        in_specs=[act, act, act, wsp, wsp, wsp, gate(0), gate(1), gate(2)],
        out_specs=pl.BlockSpec((tm, tn), lambda i, n: (i, n)),
        out_shape=jax.ShapeDtypeStruct((m, D_MODEL), BF16),
        compiler_params=_params("parallel", "arbitrary"),
        name="merge",
    )(a_pool, a_att, a_ssm, w_p, w_a, w_s, proj2, proj2, proj2)


def _outproj_kernel(m_ref, w_ref, g_ref, x_ref, o_ref):
    y = jnp.dot(m_ref[...], w_ref[...], preferred_element_type=F32)
    r = lax.rsqrt(jnp.mean(y * y, axis=-1, keepdims=True) + EPS)
    o_ref[...] = x_ref[...] + y * r * g_ref[...]


def _outproj(merged, w_out, g_post, x2d):
    m = merged.shape[0]
    tm = min(m, OUTPROJ_ROWS)
    row = pl.BlockSpec((tm, D_MODEL), lambda i: (i, 0))
    return pl.pallas_call(
        _outproj_kernel,
        grid=(m // tm,),
        in_specs=[row, pl.BlockSpec((D_MODEL, D_MODEL), lambda i: (0, 0)),
                  pl.BlockSpec((1, D_MODEL), lambda i: (0, 0)), row],
        out_specs=row,
        out_shape=jax.ShapeDtypeStruct((m, D_MODEL), F32),
        compiler_params=_params("parallel"),
        name="outproj",
    )(merged, w_out, g_post, x2d)


def _prep_layer_weights(l, g_pre, g_post, w_in, w_pool, pool_scale, pe_cmp, w_phi, lam_re, lam_im, log_step,
                        b_re, b_im, c_re, c_im, d_skip, w_glu, w_br_pool, w_br_nsa, w_br_ssm, w_out):
    w_cmp = [sum(jnp.pad(w_phi[l, c], ((0, 0), (k * HEAD_DIM, KV_WIDTH - (k + 1) * HEAD_DIM),
                                       (k * HEAD_DIM, KV_WIDTH - (k + 1) * HEAD_DIM)))
                 for k in range(N_KV_HEADS)).astype(BF16) for c in range(2)]
    pe = [jnp.broadcast_to(pe_cmp[l, c][:, None, :], (BLOCK, N_KV_HEADS, HEAD_DIM)).reshape(BLOCK, KV_WIDTH)
          for c in range(2)]
    tab, bmr, bmi, cm = _ssm_tables(lam_re[l], lam_im[l], log_step[l], b_re[l], b_im[l], c_re[l], c_im[l])
    return dict(
        g_pre=g_pre[l].reshape(1, D_MODEL), g_post=g_post[l].reshape(1, D_MODEL),
        layer=l, wt_in=jnp.swapaxes(w_in, 1, 2),
        w_pool=w_pool[l].astype(BF16), pool_scale=pool_scale[l].reshape(1, POOL_WIDTH),
        w_cmp=w_cmp, pe=pe, tab=tab, bmr=bmr, bmi=bmi, cm=cm, d_skip=d_skip[l].reshape(1, SSM_WIDTH),
        w_glu=w_glu[l].astype(BF16), w_br_pool=w_br_pool[l].astype(BF16), w_br_nsa=w_br_nsa[l].astype(BF16),
        w_br_ssm=w_br_ssm[l].astype(BF16), w_out=w_out[l].astype(BF16))


def _layer(x3, lw, *, pos0, s_valid, pool_prefix, ssm_h0, win_prefix, paged):
    b, s, _ = x3.shape
    x2 = x3.reshape(b * s, D_MODEL)
    proj2 = _inproj(x2, lw["g_pre"], lw["wt_in"], lw["layer"], N_MAIN, INPROJ_TILE, _main_src_row)
    gates3 = _inproj(x2, lw["g_pre"], lw["wt_in"], lw["layer"], N_GATE_PAD, N_GATE_PAD,
                     lambda j: _ROW_AG + j * N_GATE_PAD).reshape(b, s, N_GATE_PAD)
    proj3 = proj2.reshape(b, s, N_MAIN)

    prefix16 = jnp.pad(pool_prefix, ((0, 0), (1, 0), (0, 0)))
    a_pool, pool_tail = _pool(proj3, prefix16, lw["w_pool"], lw["pool_scale"], q0=pos0, s_valid=s_valid)
    pool_state = pool_tail[:, 1:]

    kvn = proj3[:, :, C_KV:]
    if paged is None:
        tq = min(s, QUERY_TILE)
        n_cmp = nb_total = nbp = s // BLOCK
        x3k = x3v = proj3.reshape(b * n_cmp, BLOCK, N_MAIN)
        kcol, vcol = KV_COL, KV_COL + 1
    else:
        tq = s
        cache5, layer, page_table = paged
        past_len = page_table.shape[1] * PAGE_SIZE
        n_cmp = past_len // BLOCK
        nb_total = -(-(past_len + s_valid) // BLOCK)
        nbp = -(-nb_total // SUBLANES) * SUBLANES
    if paged is None:
        kc = _compress(x3k, kcol, lw["pe"][0], lw["w_cmp"][0])
        vc = _compress(x3v, vcol, lw["pe"][1], lw["w_cmp"][1])
    else:
        pe2 = jnp.stack([jnp.tile(pe, (BLOCKS_PER_PAGE, 1)) for pe in lw["pe"]])
        xk_rows, xv_rows = _gather_cmp(cache5, layer, page_table, pe2)
        kc = _compress_rows(xk_rows, lw["w_cmp"][0])
        vc = _compress_rows(xv_rows, lw["w_cmp"][1])
    kc = kc.reshape(b, n_cmp, KV_WIDTH)
    vc = vc.reshape(b, n_cmp, KV_WIDTH)
    kc = jnp.pad(kc, ((0, 0), (0, nbp - n_cmp), (0, 0)))
    vct = jnp.swapaxes(jnp.pad(vc, ((0, 0), (0, nbp - n_cmp), (0, 0))), 1, 2)
    o_cmp, sel = _cmpsel(proj3, kc, vct, pos0=pos0, tq=tq, nb_total=nb_total)
    if paged is None:
        assert pos0 == 0
        o_sel = _selattn(proj3, _tcols(proj3, KV_COL + 3), sel, tq=tq)
    else:
        o_sel = _selattn_paged(proj3, cache5, layer, page_table, sel)

    tq_win = min(s, QUERY_TILE)
    n_rows = ((s - tq_win) // KEY_TILE + -(-(WINDOW + tq_win) // KEY_TILE)) * KEY_TILE
    pad_rows = n_rows - WINDOW - s
    k_new, v_new = kvn[:, :, 4 * KV_WIDTH:5 * KV_WIDTH], kvn[:, :, 5 * KV_WIDTH:6 * KV_WIDTH]
    v_new_t = _tcols(proj3, KV_COL + 5)
    if win_prefix is None:
        assert pad_rows == 0 and tq_win == KEY_TILE and WINDOW % KEY_TILE == 0 and pos0 == 0
        o_win = _winattn(proj3, proj3, KV_COL + 4, v_new_t, pos0=pos0, row_pos0=-WINDOW, tq=tq_win,
                         virtual_tiles=WINDOW // KEY_TILE)
    else:
        k_pre = win_prefix[:, :, 0].reshape(b, WINDOW, KV_WIDTH)
        v_pre = win_prefix[:, :, 1].reshape(b, WINDOW, KV_WIDTH)
        kwin = jnp.pad(jnp.concatenate([k_pre, k_new], axis=1), ((0, 0), (0, pad_rows), (0, 0)))
        vwin_t = jnp.pad(jnp.concatenate([jnp.swapaxes(v_pre, 1, 2), v_new_t], axis=2),
                         ((0, 0), (0, 0), (0, pad_rows)))
        o_win = _winattn(proj3, kwin, 0, vwin_t, pos0=pos0, row_pos0=pos0 - WINDOW, tq=tq_win)
    a_att = _attcomb(o_cmp, o_sel, o_win, gates3, proj3)

    a_ssm, ssm_state = _ssm(proj3, ssm_h0.reshape(b, 2, SSM_CH), lw["tab"], lw["bmr"], lw["bmi"], lw["cm"],
                            lw["d_skip"], lw["w_glu"], s_valid=s_valid)

    merged = _merge(a_pool.reshape(b * s, POOL_WIDTH), a_att.reshape(b * s, ATT_WIDTH),
                    a_ssm.reshape(b * s, SSM_WIDTH), lw["w_br_pool"], lw["w_br_nsa"], lw["w_br_ssm"], proj2)
    x_new = _outproj(merged, lw["w_out"], lw["g_post"], x2).reshape(b, s, D_MODEL)

    kv_rows = kvn[:, :s_valid, :4 * KV_WIDTH].reshape(b, s_valid, 4, N_KV_HEADS, HEAD_DIM)
    if win_prefix is None:
        assert s_valid >= WINDOW
        win_k, win_v = k_new[:, s_valid - WINDOW:s_valid], v_new[:, s_valid - WINDOW:s_valid]
    else:
        win_k = jnp.concatenate([k_pre, k_new], axis=1)[:, s_valid:s_valid + WINDOW]
        win_v = jnp.concatenate([v_pre, v_new], axis=1)[:, s_valid:s_valid + WINDOW]
    win_state = jnp.stack([win_k.reshape(b, WINDOW, N_KV_HEADS, HEAD_DIM),
                           win_v.reshape(b, WINDOW, N_KV_HEADS, HEAD_DIM)], axis=2)
    return x_new, kv_rows, win_state, pool_state, ssm_state.reshape(b, 2, SSM_GROUPS, SSM_STATE)


def kernel(x_prompt, x_sample, cache_kv, page_table, state_win_kv, state_pool, state_ssm, g_pre, g_post, w_in, w_pool, pool_scale, pe_cmp, w_phi, lam_re, lam_im, log_step, b_re, b_im, c_re, c_im, d_skip, w_glu, w_br_pool, w_br_nsa, w_br_ssm, w_out):
    depth = w_in.shape[0]
    bp, sp, _ = x_prompt.shape
    bd, sd, _ = x_sample.shape
    n_pool = cache_kv.shape[1]
    past_len = page_table.shape[1] * PAGE_SIZE
    assert state_win_kv.shape[2] == WINDOW and past_len >= WINDOW and sp >= WINDOW
    cache5 = jnp.transpose(cache_kv, (0, 1, 3, 4, 5, 2)).reshape(depth, n_pool, 4, KV_WIDTH, PAGE_SIZE)
    sd_pad = -(-sd // SUBLANES) * SUBLANES
    yp = x_prompt
    ys = jnp.pad(x_sample, ((0, 0), (0, sd_pad - sd), (0, 0)))
    zeros_pool = jnp.zeros((bp, POOL_STATE, POOL_WIDTH), F32)
    zeros_ssm = jnp.zeros((bp, 2, SSM_GROUPS, SSM_STATE), F32)
    outs_p, outs_s = [], []
    for l in range(depth):
        lw = _prep_layer_weights(l, g_pre, g_post, w_in, w_pool, pool_scale, pe_cmp, w_phi, lam_re, lam_im, log_step,
                                 b_re, b_im, c_re, c_im, d_skip, w_glu, w_br_pool, w_br_nsa, w_br_ssm, w_out)
        yp, *rp = _layer(yp, lw, pos0=0, s_valid=sp, pool_prefix=zeros_pool, ssm_h0=zeros_ssm,
                         win_prefix=None, paged=None)
        ys, *rs = _layer(ys, lw, pos0=past_len, s_valid=sd, pool_prefix=state_pool[l], ssm_h0=state_ssm[l],
                         win_prefix=state_win_kv[l], paged=(cache5, l, page_table))
        outs_p.append(rp)
        outs_s.append(rs)
    stack = lambda outs, i: jnp.stack([o[i] for o in outs])
    return (yp, ys[:, :sd], stack(outs_p, 0), stack(outs_s, 0), stack(outs_p, 1), stack(outs_s, 1),
            stack(outs_p, 2), stack(outs_s, 2), stack(outs_p, 3), stack(outs_s, 3))
```

```python
import functools

import jax
import jax.numpy as jnp
from jax import lax
from jax.experimental import pallas as pl
from jax.experimental.pallas import tpu as pltpu

F32 = jnp.float32
BF16 = jnp.bfloat16
I32 = jnp.int32

D_MODEL = 2048
PAGE_SIZE = 128
POOL_WIDTH = D_MODEL // 2
POOL_WINDOWS = (2, 4, 8, 16)
POOL_GROUP_DIM = POOL_WIDTH // len(POOL_WINDOWS)
POOL_STATE = max(POOL_WINDOWS) - 1
N_HEADS = 16
HEAD_DIM = 64
N_KV_HEADS = 4
GQA = N_HEADS // N_KV_HEADS
ATT_WIDTH = N_HEADS * HEAD_DIM
KV_WIDTH = N_KV_HEADS * HEAD_DIM
BLOCK = 64
TOP_N = 16
WINDOW = 512
N_ATT_GATES = 3
SSM_WIDTH = D_MODEL // 2
SSM_GROUP_DIM = 16
SSM_GROUPS = SSM_WIDTH // SSM_GROUP_DIM
SSM_STATE = 64
SSM_CH = SSM_GROUPS * SSM_STATE
N_BRANCH = 3
EPS = 1e-6
NEG = -1e30
FORCE = 1e4
ATT_SCALE = HEAD_DIM ** -0.5
LOG2_E = 1.4426950408889634

C_PU, C_PZ, C_Q, C_AZ, C_SU, C_SZ = (i * 1024 for i in range(6))
C_MG = 6 * 1024
C_KV = C_MG + N_BRANCH * D_MODEL
N_MAIN = C_KV + 6 * KV_WIDTH
N_GATE_PAD = 128
KV_COL = C_KV // KV_WIDTH

VMEM_LIMIT_BYTES = 52 * 1024 * 1024
SUBLANES = 8
KEY_TILE = 256
BLOCKS_PER_TILE = KEY_TILE // BLOCK
BLOCKS_PER_PAGE = PAGE_SIZE // BLOCK
PAGES_PER_STEP = 16
SSM_LANE_CHUNK = 1024
LANES = 128

SSM_MM_GROUPS = 8
SSM_MM_IN = SSM_MM_GROUPS * SSM_GROUP_DIM
SSM_MM_STATES = SSM_MM_GROUPS * SSM_STATE
SSM_N_MM = SSM_GROUPS // SSM_MM_GROUPS

INPROJ_ROWS = 1024
MERGE_ROWS = 512
MERGE_COLS = 1024
OUTPROJ_ROWS = 512
POOL_ROWS = 512
SSM_ROWS = 256
QUERY_TILE = 256
TCOLS_ROWS = 512
BLOCK_ROWS_BLOCKS = 32
COMPRESS_ROWS_BLOCKS = 256

NT_DIMS = (((1,), (1,)), ((), ()))


def _params(*sem):
    return pltpu.CompilerParams(dimension_semantics=sem, vmem_limit_bytes=VMEM_LIMIT_BYTES)


def _silu(x):
    return x * jax.nn.sigmoid(x)


def _inproj_kernel(x_ref, g_ref, w_ref, o_ref, h_ref):
    @pl.when(pl.program_id(1) == 0)
    def _():
        x = x_ref[...]
        r = lax.rsqrt(jnp.mean(x * x, axis=-1, keepdims=True) + EPS)
        h_ref[...] = (x * r * g_ref[...]).astype(BF16)

    o_ref[...] = lax.dot_general(h_ref[...], w_ref[0].astype(BF16), NT_DIMS, preferred_element_type=F32)


def _inproj(x2d, g, wt_all, layer, n_out, tn, src_row):
    m, d = x2d.shape
    tm = min(m, INPROJ_ROWS)
    w_spec = pl.BlockSpec((pl.Element(1), pl.Element(tn), pl.Element(d)),
                          lambda i, j: (layer, pl.multiple_of(src_row(j), SUBLANES), 0))
    return pl.pallas_call(
        _inproj_kernel,
        grid=(m // tm, n_out // tn),
        in_specs=[pl.BlockSpec((tm, d), lambda i, j: (i, 0)),
                  pl.BlockSpec((1, d), lambda i, j: (0, 0)),
                  w_spec],
        out_specs=pl.BlockSpec((tm, tn), lambda i, j: (i, j)),
        out_shape=jax.ShapeDtypeStruct((m, n_out), F32),
        scratch_shapes=[pltpu.VMEM((tm, d), BF16)],
        compiler_params=_params("parallel", "arbitrary"),
        name="inproj",
    )(x2d, g, wt_all)


_ROW_KV = 2 * POOL_WIDTH + ATT_WIDTH
_ROW_AG = _ROW_KV + 6 * KV_WIDTH
_ROW_AZ = _ROW_AG + N_HEADS * N_ATT_GATES
INPROJ_TILE = 768


def _main_src_row(j):
    n_head = _ROW_KV // INPROJ_TILE
    n_mid = (C_KV - _ROW_KV) // INPROJ_TILE
    return jnp.where(j < n_head, j * INPROJ_TILE,
                     jnp.where(j < n_head + n_mid, _ROW_AZ + (j - n_head) * INPROJ_TILE,
                               _ROW_KV + (j - n_head - n_mid) * INPROJ_TILE))


def _pool_kernel(pu_ref, pz_ref, pre_ref, wp_ref, sc_ref, o_ref, st_ref, e_ref, *, tt, q0, rows_last):
    t = pl.program_id(1)
    hist = POOL_STATE + 1

    @pl.when(t == 0)
    def _():
        e_ref[0:hist, :] = pre_ref[0]

    @pl.when(t > 0)
    def _():
        e_ref[0:hist, :] = e_ref[tt:tt + hist, :]

    u = pu_ref[0]
    e_ref[hist:hist + tt, :] = u
    pos = q0 + t * tt + lax.broadcasted_iota(I32, (tt, 1), 0)
    ys = []
    for gi, w in enumerate(POOL_WINDOWS):
        lo = gi * POOL_GROUP_DIM
        tot = e_ref[hist:hist + tt, lo:lo + POOL_GROUP_DIM]
        for k in range(1, w):
            tot = tot + e_ref[hist - k:hist - k + tt, lo:lo + POOL_GROUP_DIM]
        cnt = jnp.minimum(pos + 1, w).astype(F32)
        diff = tot / cnt - u[:, lo:lo + POOL_GROUP_DIM]
        ys.append(jnp.dot(diff.astype(BF16), wp_ref[gi], preferred_element_type=F32))
    y = jnp.concatenate(ys, axis=-1) * sc_ref[...]
    o_ref[0] = (y * _silu(pz_ref[0])).astype(BF16)

    @pl.when(t == pl.num_programs(1) - 1)
    def _():
        st_ref[0] = e_ref[rows_last:rows_last + hist, :]


def _pool(proj3, prefix16, w_pool, pool_scale, *, q0, s_valid):
    b, s, _ = proj3.shape
    tt = min(s, POOL_ROWS)
    hist = POOL_STATE + 1
    rows_last = ((s_valid - 1) % tt) + 1
    kern = functools.partial(_pool_kernel, tt=tt, q0=q0, rows_last=rows_last)
    return pl.pallas_call(
        kern,
        grid=(b, s // tt),
        in_specs=[pl.BlockSpec((1, tt, POOL_WIDTH), lambda i, t: (i, t, C_PU // POOL_WIDTH)),
                  pl.BlockSpec((1, tt, POOL_WIDTH), lambda i, t: (i, t, C_PZ // POOL_WIDTH)),
                  pl.BlockSpec((1, hist, POOL_WIDTH), lambda i, t: (i, 0, 0)),
                  pl.BlockSpec((len(POOL_WINDOWS), POOL_GROUP_DIM, POOL_GROUP_DIM), lambda i, t: (0, 0, 0)),
                  pl.BlockSpec((1, POOL_WIDTH), lambda i, t: (0, 0))],
        out_specs=[pl.BlockSpec((1, tt, POOL_WIDTH), lambda i, t: (i, t, 0)),
                   pl.BlockSpec((1, hist, POOL_WIDTH), lambda i, t: (i, 0, 0))],
        out_shape=[jax.ShapeDtypeStruct((b, s, POOL_WIDTH), BF16),
                   jax.ShapeDtypeStruct((b, hist, POOL_WIDTH), F32)],
        scratch_shapes=[pltpu.VMEM((hist + tt, POOL_WIDTH), F32)],
        compiler_params=_params("parallel", "arbitrary"),
        name="pool",
    )(proj3, proj3, prefix16, w_pool, pool_scale)


def _block_rows_kernel(x_ref, pe_ref, o_ref):
    n_groups = x_ref.shape[0] // SUBLANES
    for m in range(BLOCK // SUBLANES):
        rows = slice(m * SUBLANES, (m + 1) * SUBLANES)
        per_group = [_sublane_transpose8([x_ref[grp * SUBLANES + jj, rows, :] + pe_ref[rows, :]
                                          for jj in range(SUBLANES)]) for grp in range(n_groups)]
        for s in range(SUBLANES):
            o_ref[m * SUBLANES + s] = jnp.concatenate([g[s] for g in per_group], axis=0).astype(BF16)


def _block_rows(x3, col, pe):
    nb = x3.shape[0]
    tm = min(nb, BLOCK_ROWS_BLOCKS)
    return pl.pallas_call(
        _block_rows_kernel,
        grid=(nb // tm,),
        in_specs=[pl.BlockSpec((tm, BLOCK, KV_WIDTH), lambda i: (i, 0, col)),
                  pl.BlockSpec((BLOCK, KV_WIDTH), lambda i: (0, 0))],
        out_specs=pl.BlockSpec((BLOCK, tm, KV_WIDTH), lambda i: (0, i, 0)),
        out_shape=jax.ShapeDtypeStruct((BLOCK, nb, KV_WIDTH), BF16),
        compiler_params=_params("parallel"),
        name="block_rows",
    )(x3, pe)


def _compress_rows_kernel(x_ref, w_ref, o_ref):
    x = jnp.concatenate([x_ref[l] for l in range(BLOCK)], axis=1)
    o_ref[...] = jnp.dot(x, w_ref[...].reshape(BLOCK * KV_WIDTH, KV_WIDTH), preferred_element_type=F32)


def _compress_rows(xl, w):
    nb = xl.shape[1]
    tm = min(nb, COMPRESS_ROWS_BLOCKS)
    return pl.pallas_call(
        _compress_rows_kernel,
        grid=(nb // tm,),
        in_specs=[pl.BlockSpec((BLOCK, tm, KV_WIDTH), lambda i: (0, i, 0)),
                  pl.BlockSpec((BLOCK, KV_WIDTH, KV_WIDTH), lambda i: (0, 0, 0))],
        out_specs=pl.BlockSpec((tm, KV_WIDTH), lambda i: (i, 0)),
        out_shape=jax.ShapeDtypeStruct((nb, KV_WIDTH), F32),
        compiler_params=_params("parallel"),
        name="compress_rows",
    )(xl, w)


def _page_specs(layer, n_pages, comp_block):
    def spec(i):
        def index(bi, st, pt):
            return (layer, pt[bi, jnp.minimum(st * PAGES_PER_STEP + i, n_pages - 1)], comp_block, 0, 0)
        return pl.BlockSpec((1, 1, 2, KV_WIDTH, PAGE_SIZE), index)
    return [spec(i) for i in range(PAGES_PER_STEP)]


def _sublane_transpose8(a):
    sub = lax.broadcasted_iota(I32, a[0].shape, 0)
    for shift in (4, 2, 1):
        low = (sub & shift) == 0
        nxt = list(a)
        for j in range(SUBLANES):
            if j & shift == 0:
                nxt[j] = jnp.where(low, a[j], pltpu.roll(a[j + shift], shift, axis=0))
                nxt[j + shift] = jnp.where(low, pltpu.roll(a[j], SUBLANES - shift, axis=0), a[j + shift])
        a = nxt
    return a


def _gather_cmp_kernel(pt_ref, *refs):
    del pt_ref
    pages, (pe_ref, xk_ref, xv_ref) = refs[:PAGES_PER_STEP], refs[PAGES_PER_STEP:]
    n_groups = PAGES_PER_STEP * BLOCKS_PER_PAGE // SUBLANES
    for comp, dst in enumerate((xk_ref, xv_ref)):
        rows = [page[0, 0, comp].T + pe_ref[comp] for page in pages]
        for m in range(BLOCK // SUBLANES):
            per_group = []
            for grp in range(n_groups):
                pieces = []
                for jj in range(SUBLANES):
                    j = grp * SUBLANES + jj
                    lo = (j % BLOCKS_PER_PAGE) * BLOCK + m * SUBLANES
                    pieces.append(rows[j // BLOCKS_PER_PAGE][lo:lo + SUBLANES, :])
                per_group.append(_sublane_transpose8(pieces))
            for s in range(SUBLANES):
                dst[m * SUBLANES + s] = jnp.concatenate([g[s] for g in per_group], axis=0).astype(BF16)


def _gather_cmp(cache5, layer, page_table, pe2):
    b, n_pages = page_table.shape
    n_steps = n_pages // PAGES_PER_STEP
    blocks_per_step = PAGES_PER_STEP * BLOCKS_PER_PAGE
    assert blocks_per_step % (2 * SUBLANES) == 0
    out_spec = pl.BlockSpec((BLOCK, blocks_per_step, KV_WIDTH), lambda bi, st, pt: (0, bi * n_steps + st, 0))
    out = jax.ShapeDtypeStruct((BLOCK, b * n_pages * BLOCKS_PER_PAGE, KV_WIDTH), BF16)
    return pl.pallas_call(
        _gather_cmp_kernel,
        grid_spec=pltpu.PrefetchScalarGridSpec(
            num_scalar_prefetch=1, grid=(b, n_steps),
            in_specs=_page_specs(layer, n_pages, 0) + [
                pl.BlockSpec((2, PAGE_SIZE, KV_WIDTH), lambda bi, st, pt: (0, 0, 0))],
            out_specs=[out_spec, out_spec]),
        out_shape=[out, out],
        compiler_params=_params("parallel", "arbitrary"),
        name="gather_cmp",
    )(page_table, *([cache5] * PAGES_PER_STEP), pe2)


def _cmpsel_kernel(q_ref, kc_ref, vct_ref, ocmp_ref, sel_ref, s_ref, *, tq, nbp, pos0, n_sel):
    j = pl.program_id(1)
    qs = (q_ref[0] * ATT_SCALE).astype(BF16)
    kc = kc_ref[0].astype(BF16)
    vct = vct_ref[0].astype(BF16)
    nidx = lax.broadcasted_iota(I32, (nbp, tq), 0)
    pos = pos0 + j * tq + lax.broadcasted_iota(I32, (nbp, tq), 1)
    cvalid = nidx < lax.shift_right_logical(pos + 1, 6)
    cur = lax.shift_right_logical(pos, 6)
    forced = (nidx == 0) | (nidx == cur) | (nidx == cur - 1)
    outs = []
    for k in range(N_KV_HEADS):
        ksl = slice(k * HEAD_DIM, (k + 1) * HEAD_DIM)
        imp = jnp.zeros((nbp, tq), F32)
        for g in range(GQA):
            h = k * GQA + g
            sc = lax.dot_general(kc[:, ksl], qs[:, h * HEAD_DIM:(h + 1) * HEAD_DIM], NT_DIMS,
                                 preferred_element_type=F32)
            sc = jnp.where(cvalid, sc, NEG)
            e = jnp.exp(sc - jnp.max(sc, axis=0, keepdims=True))
            pc = jnp.where(cvalid, e / jnp.sum(e, axis=0, keepdims=True), 0.0)
            imp = imp + pc
            outs.append(jnp.dot(vct[ksl, :], pc.astype(BF16), preferred_element_type=F32))
        s_ref[:, k * tq:(k + 1) * tq] = jnp.where(forced, FORCE, jnp.where(nidx < cur, imp, NEG))
    ocmp_ref[0] = jnp.concatenate(outs, axis=0)

    lanes = N_KV_HEADS * tq
    chunk = min(lanes, LANES)
    rows = lax.broadcasted_iota(I32, (nbp, chunk), 0)
    for c in range(lanes // chunk):
        score = s_ref[:, c * chunk:(c + 1) * chunk]
        work, sel = score, jnp.zeros_like(score)
        for _ in range(n_sel):
            top = jnp.max(work, axis=0, keepdims=True)
            first = jnp.min(jnp.where(work == top, rows, nbp), axis=0, keepdims=True)
            hit = rows == first
            work, sel = jnp.where(hit, -jnp.inf, work), jnp.where(hit, 1.0, sel)
        sel_ref[0, :, c * chunk:(c + 1) * chunk] = jnp.where(score > NEG / 2, sel, 0.0)


def _cmpsel(proj3, kc, vct, *, pos0, tq, nb_total):
    b, s, _ = proj3.shape
    nbp = kc.shape[1]
    kern = functools.partial(_cmpsel_kernel, tq=tq, nbp=nbp, pos0=pos0, n_sel=min(TOP_N, nb_total))
    lanes = N_KV_HEADS * tq
    return pl.pallas_call(
        kern,
        grid=(b, s // tq),
        in_specs=[pl.BlockSpec((1, tq, ATT_WIDTH), lambda i, j: (i, j, C_Q // ATT_WIDTH)),
                  pl.BlockSpec((1, nbp, KV_WIDTH), lambda i, j: (i, 0, 0)),
                  pl.BlockSpec((1, KV_WIDTH, nbp), lambda i, j: (i, 0, 0))],
        out_specs=[pl.BlockSpec((1, ATT_WIDTH, tq), lambda i, j: (i, 0, j)),
                   pl.BlockSpec((1, nbp, lanes), lambda i, j: (i, 0, j))],
        out_shape=[jax.ShapeDtypeStruct((b, ATT_WIDTH, s), F32),
                   jax.ShapeDtypeStruct((b, nbp, N_KV_HEADS * s), F32)],
        scratch_shapes=[pltpu.VMEM((nbp, lanes), F32)],
        compiler_params=_params("parallel", "parallel"),
        name="cmpsel",
    )(proj3, kc, vct)


def _attn_init(q_ref, qs_ref, m_ref, acc_ref, tq):
    qs = (q_ref[0] * (ATT_SCALE * LOG2_E)).astype(BF16)
    for k in range(N_KV_HEADS):
        for g in range(GQA):
            h = k * GQA + g
            qs_ref[k, g * tq:(g + 1) * tq, :] = qs[:, h * HEAD_DIM:(h + 1) * HEAD_DIM]
    m_ref[...] = jnp.full_like(m_ref, NEG)
    acc_ref[...] = jnp.zeros_like(acc_ref)


def _attn_step(k, lanes, k_tile, vt_tile, block_rows, ok, qs_ref, m_ref, acc_ref):
    ksl = slice(k * HEAD_DIM, (k + 1) * HEAD_DIM)
    s = lax.dot_general(k_tile[:, ksl], qs_ref[k, lanes, :], NT_DIMS, preferred_element_type=F32)
    if block_rows is not None:
        s = jnp.concatenate([jnp.where(row > 0.5, s[i * BLOCK:(i + 1) * BLOCK], NEG)
                             for i, row in enumerate(block_rows)], axis=0)
    if ok is not None:
        w = ok.shape[1]
        s = jnp.concatenate([jnp.where(ok, s[:, i * w:(i + 1) * w], NEG) for i in range(s.shape[1] // w)], axis=1)
    m_old = m_ref[k, :, lanes]
    m_new = jnp.maximum(m_old, jnp.max(s, axis=0, keepdims=True))
    p = jnp.exp2(s - m_new)
    alpha = jnp.exp2(m_old - m_new)
    m_ref[k, :, lanes] = m_new
    vt_ones = jnp.concatenate([vt_tile[ksl, :], jnp.ones((ONES_ROWS, vt_tile.shape[1]), BF16)], axis=0)
    acc_ref[k, :, lanes] = (alpha * acc_ref[k, :, lanes]
                            + jnp.dot(vt_ones, p.astype(BF16), preferred_element_type=F32))


def _attn_finish(o_ref, acc_ref, tq):
    outs = []
    for k in range(N_KV_HEADS):
        o = acc_ref[k, 0:HEAD_DIM, :] / acc_ref[k, HEAD_DIM:HEAD_DIM + 1, :]
        for g in range(GQA):
            outs.append(o[:, g * tq:(g + 1) * tq])
    o_ref[0] = jnp.concatenate(outs, axis=0)


ONES_ROWS = 16


def _attn_scratch(tq):
    r = GQA * tq
    return [pltpu.VMEM((N_KV_HEADS, r, HEAD_DIM), BF16),
            pltpu.VMEM((N_KV_HEADS, 1, r), F32),
            pltpu.VMEM((N_KV_HEADS, HEAD_DIM + ONES_ROWS, r), F32)]


def _lane_query_pos(shape, tq, first):
    return first + (lax.broadcasted_iota(I32, shape, 1) & (tq - 1))


def _selattn_kernel(jt_ref, kt_ref, q_ref, k_ref, vt_ref, sel_ref, o_ref, qs_ref, m_ref, acc_ref, *, tq):
    step = pl.program_id(1)
    j = jt_ref[step]
    kt = kt_ref[step]

    @pl.when(kt == 0)
    def _():
        _attn_init(q_ref, qs_ref, m_ref, acc_ref, tq)

    def key_tile(diagonal):
        k_tile = k_ref[0].astype(BF16)
        vt_tile = vt_ref[0].astype(BF16)
        r = GQA * tq
        causal = None
        if diagonal:
            causal = (lax.broadcasted_iota(I32, (KEY_TILE, tq), 0) <= lax.broadcasted_iota(I32, (KEY_TILE, tq), 1))
        for k in range(N_KV_HEADS):
            rows = [jnp.concatenate([sel_ref[0, pl.ds(kt * BLOCKS_PER_TILE + i, 1), k * tq:(k + 1) * tq]] * GQA,
                                    axis=1) for i in range(BLOCKS_PER_TILE)]
            _attn_step(k, slice(0, r), k_tile, vt_tile, rows, causal, qs_ref, m_ref, acc_ref)

    @pl.when(kt < j)
    def _():
        key_tile(False)

    @pl.when(kt == j)
    def _():
        key_tile(True)
        _attn_finish(o_ref, acc_ref, tq)


def _selattn(proj3, vt, sel, *, tq):
    b, s, _ = proj3.shape
    assert tq == KEY_TILE
    nbp = sel.shape[1]
    pairs = [(j, kt) for j in range(s // tq) for kt in range(j + 1)]
    jt = jnp.asarray([p[0] for p in pairs], I32)
    ktt = jnp.asarray([p[1] for p in pairs], I32)
    kern = functools.partial(_selattn_kernel, tq=tq)
    return pl.pallas_call(
        kern,
        grid_spec=pltpu.PrefetchScalarGridSpec(
            num_scalar_prefetch=2, grid=(b, len(pairs)),
            in_specs=[pl.BlockSpec((1, tq, ATT_WIDTH), lambda i, p, jt, kt: (i, jt[p], C_Q // ATT_WIDTH)),
                      pl.BlockSpec((1, KEY_TILE, KV_WIDTH), lambda i, p, jt, kt: (i, kt[p], KV_COL + 2)),
                      pl.BlockSpec((1, KV_WIDTH, KEY_TILE), lambda i, p, jt, kt: (i, 0, kt[p])),
                      pl.BlockSpec((1, nbp, N_KV_HEADS * tq), lambda i, p, jt, kt: (i, 0, jt[p]))],
            out_specs=pl.BlockSpec((1, ATT_WIDTH, tq), lambda i, p, jt, kt: (i, 0, jt[p])),
            scratch_shapes=_attn_scratch(tq)),
        out_shape=jax.ShapeDtypeStruct((b, ATT_WIDTH, s), F32),
        compiler_params=_params("parallel", "arbitrary"),
        name="selattn",
    )(jt, ktt, proj3, proj3, vt, sel)


def _winattn_kernel(q_ref, k_ref, vt_ref, o_ref, qs_ref, m_ref, acc_ref, *, tq, pos0, row_pos0, n_sub):
    j = pl.program_id(1)
    t = pl.program_id(2)
    r = GQA * tq

    @pl.when(t == 0)
    def _():
        _attn_init(q_ref, qs_ref, m_ref, acc_ref, tq)

    tile = (j * tq) // KEY_TILE + (n_sub - 1 - t)

    def run(ok):
        k_tile = k_ref[0].astype(BF16)
        vt_tile = vt_ref[0].astype(BF16)
        for k in range(N_KV_HEADS):
            _attn_step(k, slice(0, r), k_tile, vt_tile, None, ok, qs_ref, m_ref, acc_ref)

    aligned = (tq == KEY_TILE and WINDOW % KEY_TILE == 0 and row_pos0 % KEY_TILE == 0
               and (pos0 - row_pos0) % KEY_TILE == 0)
    if aligned:
        key_i = lax.broadcasted_iota(I32, (KEY_TILE, tq), 0)
        qry_i = lax.broadcasted_iota(I32, (KEY_TILE, tq), 1)
        in_range = row_pos0 + tile * KEY_TILE >= 0

        @pl.when(t == 0)
        def _():
            run(key_i <= qry_i)

        @pl.when((t > 0) & (t < n_sub - 1) & in_range)
        def _():
            run(None)

        @pl.when((t == n_sub - 1) & in_range)
        def _():
            run(key_i > qry_i)
    else:
        kp = row_pos0 + tile * KEY_TILE + lax.broadcasted_iota(I32, (KEY_TILE, r), 0)
        pq = _lane_query_pos((KEY_TILE, r), tq, pos0 + j * tq)
        run((kp >= 0) & (kp <= pq) & (kp > pq - WINDOW))

    @pl.when(t == n_sub - 1)
    def _():
        _attn_finish(o_ref, acc_ref, tq)


def _winattn(proj3, kwin, kcol, vwin_t, *, pos0, row_pos0, tq, virtual_tiles=0):
    b, s, _ = proj3.shape
    assert tq & (tq - 1) == 0 and (tq % KEY_TILE == 0 or s == tq)
    n_sub = -(-(WINDOW + tq) // KEY_TILE)
    assert kwin.shape[1] >= ((s - tq) // KEY_TILE + n_sub - virtual_tiles) * KEY_TILE
    assert virtual_tiles == 0 or (tq == KEY_TILE and row_pos0 + virtual_tiles * KEY_TILE == 0)

    def tile(j, t):
        return jnp.maximum((j * tq) // KEY_TILE + (n_sub - 1 - t) - virtual_tiles, 0)

    kern = functools.partial(_winattn_kernel, tq=tq, pos0=pos0, row_pos0=row_pos0, n_sub=n_sub)
    return pl.pallas_call(
        kern,
        grid=(b, s // tq, n_sub),
        in_specs=[pl.BlockSpec((1, tq, ATT_WIDTH), lambda i, j, t: (i, j, C_Q // ATT_WIDTH)),
                  pl.BlockSpec((1, KEY_TILE, KV_WIDTH), lambda i, j, t: (i, tile(j, t), kcol)),
                  pl.BlockSpec((1, KV_WIDTH, KEY_TILE), lambda i, j, t: (i, 0, tile(j, t)))],
        out_specs=pl.BlockSpec((1, ATT_WIDTH, tq), lambda i, j, t: (i, 0, j)),
        out_shape=jax.ShapeDtypeStruct((b, ATT_WIDTH, s), F32),
        scratch_shapes=_attn_scratch(tq),
        compiler_params=_params("parallel", "parallel", "arbitrary"),
        name="winattn",
    )(proj3, kwin, vwin_t)


def _selattn_paged_kernel(pt_ref, *refs, tq, n_steps):
    del pt_ref
    pages = refs[:PAGES_PER_STEP]
    q_ref, new_ref, sel_ref, o_ref, qbd_ref, selr_ref, m_ref, l_ref, acc_ref = refs[PAGES_PER_STEP:]
    st = pl.program_id(1)
    rows = N_HEADS * tq

    def flash(s, ok, v, v_is_transposed):
        s = jnp.where(ok, s, NEG)
        m_old = m_ref[...]
        m_new = jnp.maximum(m_old, jnp.max(s, axis=-1, keepdims=True))
        p = jnp.exp(s - m_new)
        alpha = jnp.exp(m_old - m_new)
        l_ref[...] = alpha * l_ref[...] + jnp.sum(p, axis=-1, keepdims=True)
        m_ref[...] = m_new
        if v_is_transposed:
            pv = lax.dot_general(p.astype(BF16), v, NT_DIMS, preferred_element_type=F32)
        else:
            pv = jnp.dot(p.astype(BF16), v, preferred_element_type=F32)
        acc_ref[...] = alpha * acc_ref[...] + pv

    @pl.when(st == 0)
    def _():
        qs = (q_ref[0] * ATT_SCALE).astype(BF16)
        qbd_ref[...] = jnp.zeros_like(qbd_ref)
        for k in range(N_KV_HEADS):
            for g in range(GQA):
                h = k * GQA + g
                qbd_ref[h * tq:(h + 1) * tq, k * HEAD_DIM:(k + 1) * HEAD_DIM] = qs[:, h * HEAD_DIM:(h + 1) * HEAD_DIM]
                selr_ref[:, h * tq:(h + 1) * tq] = sel_ref[0, :, k * tq:(k + 1) * tq]
        m_ref[...] = jnp.full_like(m_ref, NEG)
        l_ref[...] = jnp.zeros_like(l_ref)
        acc_ref[...] = jnp.zeros_like(acc_ref)

    @pl.when(st < n_steps)
    def _():
        blk = st * (PAGES_PER_STEP * BLOCKS_PER_PAGE)
        chosen = jnp.concatenate(
            [jnp.broadcast_to(selr_ref[pl.ds(blk + c, 1), :], (BLOCK, rows))
             for c in range(PAGES_PER_STEP * BLOCKS_PER_PAGE)], axis=0)
        k_t = jnp.concatenate([page[0, 0, 0] for page in pages], axis=1).astype(BF16)
        v_t = jnp.concatenate([page[0, 0, 1] for page in pages], axis=1).astype(BF16)
        s = jnp.dot(qbd_ref[...], k_t, preferred_element_type=F32)
        flash(s, chosen.T > 0.5, v_t, True)

    @pl.when(st == n_steps)
    def _():
        new = new_ref[0]
        k_new = new[:, 2 * KV_WIDTH:3 * KV_WIDTH].astype(BF16)
        v_new = new[:, 3 * KV_WIDTH:4 * KV_WIDTH].astype(BF16)
        s = lax.dot_general(qbd_ref[...], k_new, NT_DIMS, preferred_element_type=F32)
        blk = n_steps * PAGES_PER_STEP * BLOCKS_PER_PAGE
        chosen = jnp.broadcast_to(selr_ref[pl.ds(blk, 1), :], (tq, rows)).T
        causal = (lax.broadcasted_iota(I32, (rows, tq), 1)
                  <= (lax.broadcasted_iota(I32, (rows, tq), 0) & (tq - 1)))
        flash(s, (chosen > 0.5) & causal, v_new, False)
        o = acc_ref[...] / l_ref[...]
        outs = []
        for k in range(N_KV_HEADS):
            for g in range(GQA):
                h = k * GQA + g
                outs.append(o[h * tq:(h + 1) * tq, k * HEAD_DIM:(k + 1) * HEAD_DIM].T)
        o_ref[0] = jnp.concatenate(outs, axis=0)


def _selattn_paged(proj3, cache5, layer, page_table, sel):
    b, tq, _ = proj3.shape
    _, n_pages = page_table.shape
    assert tq & (tq - 1) == 0 and tq <= BLOCK and n_pages % PAGES_PER_STEP == 0
    n_steps = n_pages // PAGES_PER_STEP
    nbp = sel.shape[1]
    rows = N_HEADS * tq
    kern = functools.partial(_selattn_paged_kernel, tq=tq, n_steps=n_steps)
    return pl.pallas_call(
        kern,
        grid_spec=pltpu.PrefetchScalarGridSpec(
            num_scalar_prefetch=1, grid=(b, n_steps + 1),
            in_specs=_page_specs(layer, n_pages, 1) + [
                pl.BlockSpec((1, tq, ATT_WIDTH), lambda bi, st, pt: (bi, 0, C_Q // ATT_WIDTH)),
                pl.BlockSpec((1, tq, 4 * KV_WIDTH), lambda bi, st, pt: (bi, 0, C_KV // (4 * KV_WIDTH))),
                pl.BlockSpec((1, nbp, N_KV_HEADS * tq), lambda bi, st, pt: (bi, 0, 0))],
            out_specs=pl.BlockSpec((1, ATT_WIDTH, tq), lambda bi, st, pt: (bi, 0, 0)),
            scratch_shapes=[pltpu.VMEM((rows, KV_WIDTH), BF16), pltpu.VMEM((nbp, rows), F32),
                            pltpu.VMEM((rows, 1), F32), pltpu.VMEM((rows, 1), F32),
                            pltpu.VMEM((rows, KV_WIDTH), F32)]),
        out_shape=jax.ShapeDtypeStruct((b, ATT_WIDTH, tq), F32),
        compiler_params=_params("parallel", "arbitrary"),
        name="selattn_paged",
    )(page_table, *([cache5] * PAGES_PER_STEP), proj3, proj3, sel)


def _tcols_kernel(x_ref, o_ref):
    o_ref[0] = x_ref[0].T


def _tcols(proj3, col):
    b, s, _ = proj3.shape
    ts = min(s, TCOLS_ROWS)
    return pl.pallas_call(
        _tcols_kernel,
        grid=(b, s // ts),
        in_specs=[pl.BlockSpec((1, ts, KV_WIDTH), lambda i, t: (i, t, col))],
        out_specs=pl.BlockSpec((1, KV_WIDTH, ts), lambda i, t: (i, 0, t)),
        out_shape=jax.ShapeDtypeStruct((b, KV_WIDTH, s), F32),
        compiler_params=_params("parallel", "parallel"),
        name="tcols",
    )(proj3)


def _attcomb_kernel(oc_ref, os_ref, ow_ref, ag_ref, az_ref, o_ref):
    gate_t = jax.nn.sigmoid(ag_ref[0]).T
    oc, osel, ow = oc_ref[0], os_ref[0], ow_ref[0]
    outs = []
    for h in range(N_HEADS):
        sl = slice(h * HEAD_DIM, (h + 1) * HEAD_DIM)
        c = h * N_ATT_GATES
        outs.append(gate_t[c:c + 1, :] * oc[sl, :] + gate_t[c + 1:c + 2, :] * osel[sl, :]
                    + gate_t[c + 2:c + 3, :] * ow[sl, :])
    o_ref[0] = (jnp.concatenate(outs, axis=0).T * _silu(az_ref[0])).astype(BF16)


def _attcomb(ocmp_t, osel_t, owin_t, gates3, proj3):
    b, s, _ = proj3.shape
    tt = min(s, QUERY_TILE)
    att_t = pl.BlockSpec((1, ATT_WIDTH, tt), lambda i, t: (i, 0, t))
    return pl.pallas_call(
        _attcomb_kernel,
        grid=(b, s // tt),
        in_specs=[att_t, att_t, att_t,
                  pl.BlockSpec((1, tt, N_GATE_PAD), lambda i, t: (i, t, 0)),
                  pl.BlockSpec((1, tt, ATT_WIDTH), lambda i, t: (i, t, C_AZ // ATT_WIDTH))],
        out_specs=pl.BlockSpec((1, tt, ATT_WIDTH), lambda i, t: (i, t, 0)),
        out_shape=jax.ShapeDtypeStruct((b, s, ATT_WIDTH), BF16),
        compiler_params=_params("parallel", "parallel"),
        name="attcomb",
    )(ocmp_t, osel_t, owin_t, gates3, proj3)


def _ssm_disc_kernel(lr_ref, li_ref, ls_ref, brt_ref, bit_ref, pr_ref, pi_ref, bbr_ref, bbi_ref):
    lr, li = lr_ref[...], li_ref[...]
    dt = jnp.exp(ls_ref[...])
    mag = jnp.exp(lr * dt)
    ab_re, ab_im = mag * jnp.cos(li * dt), mag * jnp.sin(li * dt)
    den = lr * lr + li * li
    co_re = ((ab_re - 1.0) * lr + ab_im * li) / den
    co_im = (ab_im * lr - (ab_re - 1.0) * li) / den
    brt, bit = brt_ref[...], bit_ref[...]
    bbr_ref[...] = co_re[:, None, :] * brt - co_im[:, None, :] * bit
    bbi_ref[...] = co_re[:, None, :] * bit + co_im[:, None, :] * brt
    pr, pi = ab_re, ab_im
    pr_ref[0], pi_ref[0] = pr, pi
    for r in range(1, SUBLANES):
        pr, pi = pr * ab_re - pi * ab_im, pr * ab_im + pi * ab_re
        pr_ref[r], pi_ref[r] = pr, pi


def _ssm_disc(lam_re, lam_im, log_step, b_re, b_im):
    g, n = lam_re.shape
    brt = jnp.swapaxes(b_re, 1, 2)
    bit = jnp.swapaxes(b_im, 1, 2)
    pw = jax.ShapeDtypeStruct((SUBLANES, g, n), F32)
    bb = jax.ShapeDtypeStruct(brt.shape, F32)
    return pl.pallas_call(_ssm_disc_kernel, out_shape=[pw, pw, bb, bb], name="ssm_disc")(
        lam_re, lam_im, log_step.reshape(g, 1), brt, bit)


def _ssm_kernel(su_ref, sz_ref, h0_ref, tab_ref, bmr_ref, bmi_ref, cm_ref, ds_ref, wg_ref,
                o_ref, st_ref, hr_ref, hi_ref, c_ref, *, tt, row_last):
    t = pl.program_id(1)

    @pl.when(t == 0)
    def _():
        c_ref[...] = h0_ref[0]

    u = su_ref[0]
    ub = u.astype(BF16)
    for j in range(SSM_N_MM):
        uj = ub[:, j * SSM_MM_IN:(j + 1) * SSM_MM_IN]
        states = slice(j * SSM_MM_STATES, (j + 1) * SSM_MM_STATES)
        hr_ref[:, states] = jnp.dot(uj, bmr_ref[j], preferred_element_type=F32)
        hi_ref[:, states] = jnp.dot(uj, bmi_ref[j], preferred_element_type=F32)

    for cc in range(SSM_CH // SSM_LANE_CHUNK):
        lanes = slice(cc * SSM_LANE_CHUNK, (cc + 1) * SSM_LANE_CHUNK)
        tabs = [(tab_ref[2 * i, :, lanes], tab_ref[2 * i + 1, :, lanes]) for i in range(4)]

        def group(gi, carry):
            cr, ci = carry
            rows = pl.ds(pl.multiple_of(gi * SUBLANES, SUBLANES), SUBLANES)
            xr, xi = hr_ref[rows, lanes], hi_ref[rows, lanes]
            for lvl, shift in enumerate((1, 2, 4)):
                ar, ai = tabs[lvl]
                sr, si = pltpu.roll(xr, shift, axis=0), pltpu.roll(xi, shift, axis=0)
                xr, xi = xr + (ar * sr - ai * si), xi + (ar * si + ai * sr)
            pr, pi = tabs[3]
            xr, xi = xr + (pr * cr - pi * ci), xi + (pr * ci + pi * cr)
            hr_ref[rows, lanes], hi_ref[rows, lanes] = xr, xi
            return xr[SUBLANES - 1:SUBLANES, :], xi[SUBLANES - 1:SUBLANES, :]

        cr, ci = lax.fori_loop(0, tt // SUBLANES, group, (c_ref[0:1, lanes], c_ref[1:2, lanes]))
        c_ref[0:1, lanes], c_ref[1:2, lanes] = cr, ci

    @pl.when(t == pl.num_programs(1) - 1)
    def _():
        st_ref[0, 0:1, :] = hr_ref[row_last:row_last + 1, :]
        st_ref[0, 1:2, :] = hi_ref[row_last:row_last + 1, :]

    ys = []
    for j in range(SSM_N_MM):
        states = slice(j * SSM_MM_STATES, (j + 1) * SSM_MM_STATES)
        hcat = jnp.concatenate([hr_ref[:, states], hi_ref[:, states]], axis=-1)
        ys.append(jnp.dot(hcat.astype(BF16), cm_ref[j], preferred_element_type=F32))
    y = jnp.concatenate(ys, axis=-1) + ds_ref[...] * u
    z = jax.nn.gelu(y)
    out = z * jax.nn.sigmoid(jnp.dot(z.astype(BF16), wg_ref[...], preferred_element_type=F32))
    o_ref[0] = (out * _silu(sz_ref[0])).astype(BF16)


def _ssm(proj3, h0, tab, bmr, bmi, cm, d_skip, w_glu, *, s_valid):
    b, s, _ = proj3.shape
    tt = min(s, SSM_ROWS)
    row_last = (s_valid - 1) % tt
    n_mm = SSM_N_MM
    kern = functools.partial(_ssm_kernel, tt=tt, row_last=row_last)
    const3 = lambda i, t: (0, 0, 0)
    return pl.pallas_call(
        kern,
        grid=(b, s // tt),
        in_specs=[pl.BlockSpec((1, tt, SSM_WIDTH), lambda i, t: (i, t, C_SU // SSM_WIDTH)),
                  pl.BlockSpec((1, tt, SSM_WIDTH), lambda i, t: (i, t, C_SZ // SSM_WIDTH)),
                  pl.BlockSpec((1, 2, SSM_CH), lambda i, t: (i, 0, 0)),
                  pl.BlockSpec(tab.shape, const3),
                  pl.BlockSpec((n_mm, SSM_MM_IN, SSM_MM_STATES), const3),
                  pl.BlockSpec((n_mm, SSM_MM_IN, SSM_MM_STATES), const3),
                  pl.BlockSpec((n_mm, 2 * SSM_MM_STATES, SSM_MM_IN), const3),
                  pl.BlockSpec((1, SSM_WIDTH), lambda i, t: (0, 0)),
                  pl.BlockSpec((SSM_WIDTH, SSM_WIDTH), lambda i, t: (0, 0))],
        out_specs=[pl.BlockSpec((1, tt, SSM_WIDTH), lambda i, t: (i, t, 0)),
                   pl.BlockSpec((1, 2, SSM_CH), lambda i, t: (i, 0, 0))],
        out_shape=[jax.ShapeDtypeStruct((b, s, SSM_WIDTH), BF16),
                   jax.ShapeDtypeStruct((b, 2, SSM_CH), F32)],
        scratch_shapes=[pltpu.VMEM((tt, SSM_CH), F32), pltpu.VMEM((tt, SSM_CH), F32),
                        pltpu.VMEM((2, SSM_CH), F32)],
        compiler_params=_params("parallel", "arbitrary"),
        name="ssm",
    )(proj3, proj3, h0, tab, bmr, bmi, cm, d_skip, w_glu)


def _ssm_tables(lam_re, lam_im, log_step, b_re, b_im, c_re, c_im):
    pw_re, pw_im, bbr, bbi = _ssm_disc(lam_re, lam_im, log_step, b_re, b_im)
    pw_re = pw_re.reshape(SUBLANES, SSM_CH)
    pw_im = pw_im.reshape(SUBLANES, SSM_CH)
    row = jnp.arange(SUBLANES)[:, None]
    tabs = []
    for shift in (1, 2, 4):
        keep = row >= shift
        tabs += [jnp.where(keep, pw_re[shift - 1][None], 0.0), jnp.where(keep, pw_im[shift - 1][None], 0.0)]
    tab = jnp.stack(tabs + [pw_re, pw_im])
    eye = jnp.eye(SSM_MM_GROUPS, dtype=F32)

    def b_blocks(bb):
        x = bb.reshape(SSM_N_MM, SSM_MM_GROUPS, SSM_GROUP_DIM, SSM_STATE)
        return jnp.einsum("jgcn,gh->jgchn", x, eye).reshape(SSM_N_MM, SSM_MM_IN, SSM_MM_STATES).astype(BF16)

    def c_blocks(c):
        x = c.reshape(SSM_N_MM, SSM_MM_GROUPS, SSM_GROUP_DIM, SSM_STATE)
        return jnp.einsum("jgcn,gh->jgnhc", x, eye).reshape(SSM_N_MM, SSM_MM_STATES, SSM_MM_IN)

    cm = jnp.concatenate([c_blocks(c_re), -c_blocks(c_im)], axis=1).astype(BF16)
    return tab, b_blocks(bbr), b_blocks(bbi), cm


def _merge_kernel(ap_ref, aa_ref, as_ref, wp_ref, wa_ref, ws_ref, g0_ref, g1_ref, g2_ref, o_ref):
    bp = jnp.dot(ap_ref[...], wp_ref[...], preferred_element_type=F32)
    ba = jnp.dot(aa_ref[...], wa_ref[...], preferred_element_type=F32)
    bs = jnp.dot(as_ref[...], ws_ref[...], preferred_element_type=F32)
    o_ref[...] = (jax.nn.sigmoid(g0_ref[...]) * bp + jax.nn.sigmoid(g1_ref[...]) * ba
                  + jax.nn.sigmoid(g2_ref[...]) * bs).astype(BF16)


def _merge(a_pool, a_att, a_ssm, w_p, w_a, w_s, proj2):
    m = a_pool.shape[0]
    tm = min(m, MERGE_ROWS)
    tn = MERGE_COLS
    act = pl.BlockSpec((tm, POOL_WIDTH), lambda i, n: (i, 0))
    wsp = pl.BlockSpec((POOL_WIDTH, tn), lambda i, n: (0, n))
    gate = lambda r: pl.BlockSpec((tm, tn), lambda i, n: (i, (C_MG + r * D_MODEL) // tn + n))
    return pl.pallas_call(
        _merge_kernel,
        grid=(m // tm, D_MODEL // tn),
        in_specs=[act, act, act, wsp, wsp, wsp, gate(0), gate(1), gate(2)],
        out_specs=pl.BlockSpec((tm, tn), lambda i, n: (i, n)),
        out_shape=jax.ShapeDtypeStruct((m, D_MODEL), BF16),
        compiler_params=_params("parallel", "arbitrary"),
        name="merge",
    )(a_pool, a_att, a_ssm, w_p, w_a, w_s, proj2, proj2, proj2)


def _outproj_kernel(m_ref, w_ref, g_ref, x_ref, o_ref):
    y = jnp.dot(m_ref[...], w_ref[...], preferred_element_type=F32)
    r = lax.rsqrt(jnp.mean(y * y, axis=-1, keepdims=True) + EPS)
    o_ref[...] = x_ref[...] + y * r * g_ref[...]


def _outproj(merged, w_out, g_post, x2d):
    m = merged.shape[0]
    tm = min(m, OUTPROJ_ROWS)
    row = pl.BlockSpec((tm, D_MODEL), lambda i: (i, 0))
    return pl.pallas_call(
        _outproj_kernel,
        grid=(m // tm,),
        in_specs=[row, pl.BlockSpec((D_MODEL, D_MODEL), lambda i: (0, 0)),
                  pl.BlockSpec((1, D_MODEL), lambda i: (0, 0)), row],
        out_specs=row,
        out_shape=jax.ShapeDtypeStruct((m, D_MODEL), F32),
        compiler_params=_params("parallel"),
        name="outproj",
    )(merged, w_out, g_post, x2d)


def _prep_layer_weights(l, g_pre, g_post, w_in, w_pool, pool_scale, pe_cmp, w_phi, lam_re, lam_im, log_step,
                        b_re, b_im, c_re, c_im, d_skip, w_glu, w_br_pool, w_br_nsa, w_br_ssm, w_out):
    w_cmp = [sum(jnp.pad(w_phi[l, c], ((0, 0), (k * HEAD_DIM, KV_WIDTH - (k + 1) * HEAD_DIM),
                                       (k * HEAD_DIM, KV_WIDTH - (k + 1) * HEAD_DIM)))
                 for k in range(N_KV_HEADS)).astype(BF16) for c in range(2)]
    pe = [jnp.broadcast_to(pe_cmp[l, c][:, None, :], (BLOCK, N_KV_HEADS, HEAD_DIM)).reshape(BLOCK, KV_WIDTH)
          for c in range(2)]
    tab, bmr, bmi, cm = _ssm_tables(lam_re[l], lam_im[l], log_step[l], b_re[l], b_im[l], c_re[l], c_im[l])
    return dict(
        g_pre=g_pre[l].reshape(1, D_MODEL), g_post=g_post[l].reshape(1, D_MODEL),
        layer=l, wt_in=jnp.swapaxes(w_in, 1, 2),
        w_pool=w_pool[l].astype(BF16), pool_scale=pool_scale[l].reshape(1, POOL_WIDTH),
        w_cmp=w_cmp, pe=pe, tab=tab, bmr=bmr, bmi=bmi, cm=cm, d_skip=d_skip[l].reshape(1, SSM_WIDTH),
        w_glu=w_glu[l].astype(BF16), w_br_pool=w_br_pool[l].astype(BF16), w_br_nsa=w_br_nsa[l].astype(BF16),
        w_br_ssm=w_br_ssm[l].astype(BF16), w_out=w_out[l].astype(BF16))


def _layer(x3, lw, *, pos0, s_valid, pool_prefix, ssm_h0, win_prefix, paged):
    b, s, _ = x3.shape
    x2 = x3.reshape(b * s, D_MODEL)
    proj2 = _inproj(x2, lw["g_pre"], lw["wt_in"], lw["layer"], N_MAIN, INPROJ_TILE, _main_src_row)
    gates3 = _inproj(x2, lw["g_pre"], lw["wt_in"], lw["layer"], N_GATE_PAD, N_GATE_PAD,
                     lambda j: _ROW_AG + j * N_GATE_PAD).reshape(b, s, N_GATE_PAD)
    proj3 = proj2.reshape(b, s, N_MAIN)

    prefix16 = jnp.pad(pool_prefix, ((0, 0), (1, 0), (0, 0)))
    a_pool, pool_tail = _pool(proj3, prefix16, lw["w_pool"], lw["pool_scale"], q0=pos0, s_valid=s_valid)
    pool_state = pool_tail[:, 1:]

    kvn = proj3[:, :, C_KV:]
    if paged is None:
        tq = min(s, QUERY_TILE)
        n_cmp = nb_total = nbp = s // BLOCK
        x3k = x3v = proj3.reshape(b * n_cmp, BLOCK, N_MAIN)
        kcol, vcol = KV_COL, KV_COL + 1
    else:
        tq = s
        cache5, layer, page_table = paged
        past_len = page_table.shape[1] * PAGE_SIZE
        n_cmp = past_len // BLOCK
        nb_total = -(-(past_len + s_valid) // BLOCK)
        nbp = -(-nb_total // SUBLANES) * SUBLANES
    if paged is None:
        kc = _compress_rows(_block_rows(x3k, kcol, lw["pe"][0]), lw["w_cmp"][0])
        vc = _compress_rows(_block_rows(x3v, vcol, lw["pe"][1]), lw["w_cmp"][1])
    else:
        pe2 = jnp.stack([jnp.tile(pe, (BLOCKS_PER_PAGE, 1)) for pe in lw["pe"]])
        xk_rows, xv_rows = _gather_cmp(cache5, layer, page_table, pe2)
        kc = _compress_rows(xk_rows, lw["w_cmp"][0])
        vc = _compress_rows(xv_rows, lw["w_cmp"][1])
    kc = kc.reshape(b, n_cmp, KV_WIDTH)
    vc = vc.reshape(b, n_cmp, KV_WIDTH)
    kc = jnp.pad(kc, ((0, 0), (0, nbp - n_cmp), (0, 0)))
    vct = jnp.swapaxes(jnp.pad(vc, ((0, 0), (0, nbp - n_cmp), (0, 0))), 1, 2)
    o_cmp, sel = _cmpsel(proj3, kc, vct, pos0=pos0, tq=tq, nb_total=nb_total)
    if paged is None:
        assert pos0 == 0
        o_sel = _selattn(proj3, _tcols(proj3, KV_COL + 3), sel, tq=tq)
    else:
        o_sel = _selattn_paged(proj3, cache5, layer, page_table, sel)

    tq_win = min(s, QUERY_TILE)
    n_rows = ((s - tq_win) // KEY_TILE + -(-(WINDOW + tq_win) // KEY_TILE)) * KEY_TILE
    pad_rows = n_rows - WINDOW - s
    k_new, v_new = kvn[:, :, 4 * KV_WIDTH:5 * KV_WIDTH], kvn[:, :, 5 * KV_WIDTH:6 * KV_WIDTH]
    v_new_t = _tcols(proj3, KV_COL + 5)
    if win_prefix is None:
        assert pad_rows == 0 and tq_win == KEY_TILE and WINDOW % KEY_TILE == 0 and pos0 == 0
        o_win = _winattn(proj3, proj3, KV_COL + 4, v_new_t, pos0=pos0, row_pos0=-WINDOW, tq=tq_win,
                         virtual_tiles=WINDOW // KEY_TILE)
    else:
        k_pre = win_prefix[:, :, 0].reshape(b, WINDOW, KV_WIDTH)
        v_pre = win_prefix[:, :, 1].reshape(b, WINDOW, KV_WIDTH)
        kwin = jnp.pad(jnp.concatenate([k_pre, k_new], axis=1), ((0, 0), (0, pad_rows), (0, 0)))
        vwin_t = jnp.pad(jnp.concatenate([jnp.swapaxes(v_pre, 1, 2), v_new_t], axis=2),
                         ((0, 0), (0, 0), (0, pad_rows)))
        o_win = _winattn(proj3, kwin, 0, vwin_t, pos0=pos0, row_pos0=pos0 - WINDOW, tq=tq_win)
    a_att = _attcomb(o_cmp, o_sel, o_win, gates3, proj3)

    a_ssm, ssm_state = _ssm(proj3, ssm_h0.reshape(b, 2, SSM_CH), lw["tab"], lw["bmr"], lw["bmi"], lw["cm"],
                            lw["d_skip"], lw["w_glu"], s_valid=s_valid)

    merged = _merge(a_pool.reshape(b * s, POOL_WIDTH), a_att.reshape(b * s, ATT_WIDTH),
                    a_ssm.reshape(b * s, SSM_WIDTH), lw["w_br_pool"], lw["w_br_nsa"], lw["w_br_ssm"], proj2)
    x_new = _outproj(merged, lw["w_out"], lw["g_post"], x2).reshape(b, s, D_MODEL)

    kv_rows = kvn[:, :s_valid, :4 * KV_WIDTH].reshape(b, s_valid, 4, N_KV_HEADS, HEAD_DIM)
    if win_prefix is None:
        assert s_valid >= WINDOW
        win_k, win_v = k_new[:, s_valid - WINDOW:s_valid], v_new[:, s_valid - WINDOW:s_valid]
    else:
        win_k = jnp.concatenate([k_pre, k_new], axis=1)[:, s_valid:s_valid + WINDOW]
        win_v = jnp.concatenate([v_pre, v_new], axis=1)[:, s_valid:s_valid + WINDOW]
    win_state = jnp.stack([win_k.reshape(b, WINDOW, N_KV_HEADS, HEAD_DIM),
                           win_v.reshape(b, WINDOW, N_KV_HEADS, HEAD_DIM)], axis=2)
    return x_new, kv_rows, win_state, pool_state, ssm_state.reshape(b, 2, SSM_GROUPS, SSM_STATE)


def kernel(x_prompt, x_sample, cache_kv, page_table, state_win_kv, state_pool, state_ssm, g_pre, g_post, w_in, w_pool, pool_scale, pe_cmp, w_phi, lam_re, lam_im, log_step, b_re, b_im, c_re, c_im, d_skip, w_glu, w_br_pool, w_br_nsa, w_br_ssm, w_out):
    depth = w_in.shape[0]
    bp, sp, _ = x_prompt.shape
    bd, sd, _ = x_sample.shape
    n_pool = cache_kv.shape[1]
    past_len = page_table.shape[1] * PAGE_SIZE
    assert state_win_kv.shape[2] == WINDOW and past_len >= WINDOW and sp >= WINDOW
    cache5 = jnp.transpose(cache_kv, (0, 1, 3, 4, 5, 2)).reshape(depth, n_pool, 4, KV_WIDTH, PAGE_SIZE)
    sd_pad = -(-sd // SUBLANES) * SUBLANES
    yp = x_prompt
    ys = jnp.pad(x_sample, ((0, 0), (0, sd_pad - sd), (0, 0)))
    zeros_pool = jnp.zeros((bp, POOL_STATE, POOL_WIDTH), F32)
    zeros_ssm = jnp.zeros((bp, 2, SSM_GROUPS, SSM_STATE), F32)
    outs_p, outs_s = [], []
    for l in range(depth):
        lw = _prep_layer_weights(l, g_pre, g_post, w_in, w_pool, pool_scale, pe_cmp, w_phi, lam_re, lam_im, log_step,
                                 b_re, b_im, c_re, c_im, d_skip, w_glu, w_br_pool, w_br_nsa, w_br_ssm, w_out)
        yp, *rp = _layer(yp, lw, pos0=0, s_valid=sp, pool_prefix=zeros_pool, ssm_h0=zeros_ssm,
                         win_prefix=None, paged=None)
        ys, *rs = _layer(ys, lw, pos0=past_len, s_valid=sd, pool_prefix=state_pool[l], ssm_h0=state_ssm[l],
                         win_prefix=state_win_kv[l], paged=(cache5, l, page_table))
        outs_p.append(rp)
        outs_s.append(rs)
    stack = lambda outs, i: jnp.stack([o[i] for o in outs])
    return (yp, ys[:, :sd], stack(outs_p, 0), stack(outs_s, 0), stack(outs_p, 1), stack(outs_s, 1),
            stack(outs_p, 2), stack(outs_s, 2), stack(outs_p, 3), stack(outs_s, 3))
```
